```python
import jax, jax.numpy as jnp
from jax import lax
import numpy as np

D_MODEL = 2048
BATCH = 1
SEQ = 8192
DEPTH = 1
DEC_BATCH = 128
DEC_SEQ = 8
PAST_LEN = 16384
PAGE_SIZE = 128

HEAD_DIM = 64
RW_HEADS = D_MODEL // HEAD_DIM
RW_WIDTH = RW_HEADS * HEAD_DIM
D_DECAY_LORA = 96
D_AAA_LORA = 96
D_GATE_LORA = 256
GN_EPS = 64e-5
SHIFT_W = 3 * RW_WIDTH + D_DECAY_LORA + D_AAA_LORA + D_GATE_LORA
ATT_Q_HEADS = D_MODEL // HEAD_DIM
ATT_KV_HEADS = 4
ATT_GROUP = ATT_Q_HEADS // ATT_KV_HEADS
ATT_Q_W = ATT_Q_HEADS * HEAD_DIM
ATT_KV_W = ATT_KV_HEADS * HEAD_DIM
WINDOW = 128
ATT_BLOCK = 128
ATT_SCALE = HEAD_DIM ** -0.5
GATE_W = 2 * D_MODEL
IN_W = SHIFT_W + ATT_Q_W + 2 * ATT_KV_W + GATE_W
PEER_HEADS = 8
N_KEYS = 128
N_EXPERTS = N_KEYS * N_KEYS
PEER_TOPK = 16
PEER_QDIM = 256
PEER_BLOCK = 64
DN_ALPHA = (2 * DEPTH) ** 0.25
DN_BETA = (8 * DEPTH) ** -0.25

kernel_name = 'rwkv7_swa_sink_peer_hybrid_step'

F32 = jnp.float32


def _split_cols(a, widths):
    offs = np.cumsum(np.array(widths))[:-1].tolist()
    return jnp.split(a, offs, axis=-1)


def layer_norm(x, g, b, eps=1e-5):
    xf = x.astype(F32)
    mu = jnp.mean(xf, -1, keepdims=True)
    var = jnp.mean(jnp.square(xf - mu), -1, keepdims=True)
    return ((xf - mu) * lax.rsqrt(var + eps) * g + b).astype(x.dtype)


def rwkv7_mix(p, p_prev, s_prev, mu, w0, w_up, a0, a_up, g_up, k_k, k_a, r_k, gn_g, gn_b):
    B, T, _ = p.shape
    p_shift = jnp.concatenate([p_prev[:, None, :].astype(p.dtype), p[:, :-1]], axis=1)
    ps = p + (p_shift - p) * mu
    r, k, v, wd, ad, gd = _split_cols(ps, (RW_WIDTH, RW_WIDTH, RW_WIDTH, D_DECAY_LORA, D_AAA_LORA, D_GATE_LORA))
    w_log = -jax.nn.softplus(-(w0 + jnp.tanh(wd) @ w_up)) - 0.5
    decay = jnp.exp(-jnp.exp(w_log.astype(F32)))
    a = jax.nn.sigmoid(a0 + ad @ a_up)
    g = jax.nn.sigmoid(gd) @ g_up
    heads = lambda t: t.reshape(B, T, RW_HEADS, HEAD_DIM)
    kk = heads(k * k_k).astype(F32)
    kk = kk / jnp.maximum(jnp.sqrt(jnp.sum(kk * kk, -1, keepdims=True)), 1e-12)
    k = k * (1.0 + (a - 1.0) * k_a)
    rh, kh, vh, ah, dh = heads(r), heads(k), heads(v), heads(a), heads(decay)
    tm = lambda t: jnp.moveaxis(t.astype(F32), 1, 0)

    def step(S, inp):
        r_t, w_t, k_t, v_t, kk_t, a_t = inp
        sk = jnp.einsum('bhvk,bhk->bhv', S, kk_t)
        S = (S * w_t[:, :, None, :]
             - sk[..., None] * (kk_t * a_t)[:, :, None, :]
             + v_t[..., None] * k_t[:, :, None, :])
        return S, jnp.einsum('bhvk,bhk->bhv', S, r_t)

    s_last, y = lax.scan(step, s_prev.astype(F32), (tm(rh), tm(dh), tm(kh), tm(vh), tm(kk), tm(ah)))
    y = jnp.moveaxis(y, 0, 1)
    ym = jnp.mean(y, -1, keepdims=True)
    yv = jnp.mean(jnp.square(y - ym), -1, keepdims=True)
    yn = ((y - ym) * lax.rsqrt(yv + GN_EPS)).reshape(B, T, RW_WIDTH) * gn_g + gn_b
    bonus = jnp.sum(rh * kh * r_k, -1, keepdims=True) * vh
    out = (yn.astype(p.dtype) + bonus.reshape(B, T, RW_WIDTH)) * g
    return out, p[:, -1], s_last.astype(s_prev.dtype)


def sink_softmax(s, sink):
    sink = sink.astype(F32)
    m = jnp.maximum(jnp.max(s, -1, keepdims=True), sink)
    e = jnp.exp(s - m)
    return e / (jnp.sum(e, -1, keepdims=True) + jnp.exp(sink - m))


def swa_prompt(q, k, v, sinks):
    B, T = q.shape[:2]
    nb = T // ATT_BLOCK
    qb = q.reshape(B, nb, ATT_BLOCK, ATT_KV_HEADS, ATT_GROUP, HEAD_DIM)

    def band(t):
        tb = t.reshape(B, nb, ATT_BLOCK, ATT_KV_HEADS, HEAD_DIM)
        prev = jnp.pad(tb, ((0, 0), (1, 0), (0, 0), (0, 0), (0, 0)))[:, :nb]
        return jnp.concatenate([prev, tb], axis=2)

    kb, vb = band(k), band(v)
    s = jnp.einsum('bnqkgd,bnskd->bnkgqs', qb, kb).astype(F32) * ATT_SCALE
    qpos = jnp.arange(ATT_BLOCK)[:, None] + ATT_BLOCK
    kpos = jnp.arange(2 * ATT_BLOCK)[None, :]
    in_win = (kpos <= qpos) & (kpos > qpos - WINDOW)
    has_prev = (jnp.arange(nb) > 0)[:, None, None] | (kpos >= ATT_BLOCK)[None]
    mask = in_win[None] & has_prev
    s = jnp.where(mask[None, :, None, None], s, -jnp.inf)
    pr = sink_softmax(s, sinks[None, None, :, :, None, None])
    o = jnp.einsum('bnkgqs,bnskd->bnqkgd', pr.astype(vb.dtype), vb)
    return o.reshape(B, T, ATT_Q_W)


def swa_sample(q, k, v, win_k, win_v, sinks):
    B, L = q.shape[:2]
    C = win_k.shape[1]
    kc = jnp.concatenate([win_k.astype(k.dtype), k], axis=1)
    vc = jnp.concatenate([win_v.astype(v.dtype), v], axis=1)
    s = jnp.einsum('bqkgd,bskd->bkgqs', q, kc).astype(F32) * ATT_SCALE
    qpos = jnp.arange(L)[:, None]
    kpos = jnp.concatenate([jnp.arange(C) - C, jnp.arange(L)])[None, :]
    mask = (kpos <= qpos) & (kpos > qpos - WINDOW)
    s = jnp.where(mask, s, -jnp.inf)
    pr = sink_softmax(s, sinks[None, :, :, None, None])
    o = jnp.einsum('bkgqs,bskd->bqkgd', pr.astype(vc.dtype), vc).reshape(B, L, ATT_Q_W)
    return o, kc[:, L:], vc[:, L:]


def peer(x, wq, sub_keys, u_tab, v_tab):
    B, T, D = x.shape
    n = B * T
    xt = x.reshape(n, D)
    q = (xt @ wq).reshape(n, PEER_HEADS, 2, PEER_QDIM // 2)
    s = jnp.einsum('nhcd,hckd->nhck', q, sub_keys).astype(F32)
    sv, si = lax.top_k(s, PEER_TOPK)
    cand = (sv[:, :, 0, :, None] + sv[:, :, 1, None, :]).reshape(n, PEER_HEADS, PEER_TOPK * PEER_TOPK)
    cid = (si[:, :, 0, :, None] * N_KEYS + si[:, :, 1, None, :]).reshape(n, PEER_HEADS, PEER_TOPK * PEER_TOPK)
    top_s, top_pos = lax.top_k(cand, PEER_TOPK)
    eid = jnp.take_along_axis(cid, top_pos, axis=-1)
    gate = jax.nn.softmax(top_s, axis=-1).astype(x.dtype)
    pad = (-n) % PEER_BLOCK
    xt_p = jnp.pad(xt, ((0, pad), (0, 0)))
    eid_p = jnp.pad(eid, ((0, pad), (0, 0), (0, 0)))
    gate_p = jnp.pad(gate, ((0, pad), (0, 0), (0, 0)))
    nblk = (n + pad) // PEER_BLOCK

    def blk(args):
        xb, eb, gb = args
        act = jax.nn.gelu(jnp.einsum('nhkd,nd->nhk', u_tab[eb], xb), approximate=False)
        return jnp.einsum('nhk,nhkd->nd', gb * act, v_tab[eb])

    out = lax.map(blk, (xt_p.reshape(nblk, PEER_BLOCK, D),
                        eid_p.reshape(nblk, PEER_BLOCK, PEER_HEADS, PEER_TOPK),
                        gate_p.reshape(nblk, PEER_BLOCK, PEER_HEADS, PEER_TOPK)))
    return out.reshape(nblk * PEER_BLOCK, D)[:n].reshape(B, T, D)


def trunk_layer(x, p_prev, s_prev, win_k, win_v,
                w_in, rw_mu, rw_w0, rw_w_up, rw_a0, rw_a_up, rw_g_up, rw_k_k, rw_k_a, rw_r_k,
                rw_gn_g, rw_gn_b, att_sinks, w_out, ln1_g, ln1_b,
                peer_wq, peer_keys, peer_u, peer_v, ln2_g, ln2_b):
    B, T, _ = x.shape
    proj = x @ w_in
    p_rw, q, k, v, gates = _split_cols(proj, (SHIFT_W, ATT_Q_W, ATT_KV_W, ATT_KV_W, GATE_W))
    o_a, p_last, s_last = rwkv7_mix(p_rw, p_prev, s_prev, rw_mu, rw_w0, rw_w_up, rw_a0, rw_a_up,
                                    rw_g_up, rw_k_k, rw_k_a, rw_r_k, rw_gn_g, rw_gn_b)
    q = q.reshape(B, T, ATT_KV_HEADS, ATT_GROUP, HEAD_DIM)
    k = k.reshape(B, T, ATT_KV_HEADS, HEAD_DIM)
    v = v.reshape(B, T, ATT_KV_HEADS, HEAD_DIM)
    sinks = att_sinks.reshape(ATT_KV_HEADS, ATT_GROUP)
    if win_k is None:
        o_b = swa_prompt(q, k, v, sinks)
        keep = min(WINDOW, T)
        new_k, new_v = k[:, T - keep:], v[:, T - keep:]
    else:
        o_b, new_k, new_v = swa_sample(q, k, v, win_k, win_v, sinks)
    g_a, g_b = jnp.split(jax.nn.sigmoid(gates), 2, axis=-1)
    mixed = (g_a * o_a + g_b * o_b) @ w_out
    h = layer_norm(DN_ALPHA * x + mixed, ln1_g, ln1_b)
    y = layer_norm(DN_ALPHA * h + peer(h, peer_wq, peer_keys, peer_u, peer_v), ln2_g, ln2_b)
    return y, p_last, s_last, new_k, new_v


def setup_inputs(seed: int = 0) -> dict:
    key = jax.random.key(seed)
    ks = iter(jax.random.split(key, 40))
    nrm = lambda shape, scale: jax.random.normal(next(ks), shape, F32) * scale
    L = DEPTH
    cache_w = min(WINDOW, PAST_LEN)
    col_scale = np.ones((IN_W,), np.float32)
    col_scale[2 * RW_WIDTH:3 * RW_WIDTH] = DN_BETA
    v_off = SHIFT_W + ATT_Q_W + ATT_KV_W
    col_scale[v_off:v_off + ATT_KV_W] = DN_BETA
    inp = {}
    inp['x_prompt'] = nrm((BATCH, SEQ, D_MODEL), 1.0)
    inp['x_sample'] = nrm((DEC_BATCH, DEC_SEQ, D_MODEL), 1.0)
    inp['state_shift'] = nrm((L, DEC_BATCH, SHIFT_W), 1.0)
    inp['state_wkv'] = nrm((L, DEC_BATCH, RW_HEADS, HEAD_DIM, HEAD_DIM), 0.5)
    inp['cache_k'] = nrm((L, DEC_BATCH, cache_w, ATT_KV_HEADS, HEAD_DIM), 1.0)
    inp['cache_v'] = nrm((L, DEC_BATCH, cache_w, ATT_KV_HEADS, HEAD_DIM), DN_BETA)
    inp['w_in'] = nrm((L, D_MODEL, IN_W), D_MODEL ** -0.5) * jnp.asarray(col_scale)
    inp['rw_mu'] = jax.random.uniform(next(ks), (L, SHIFT_W), F32)
    inp['rw_w0'] = jax.random.uniform(next(ks), (L, RW_WIDTH), F32, minval=-6.5, maxval=-1.5)
    inp['rw_w_up'] = nrm((L, D_DECAY_LORA, RW_WIDTH), 0.5 * D_DECAY_LORA ** -0.5)
    inp['rw_a0'] = nrm((L, RW_WIDTH), 0.1)
    inp['rw_a_up'] = nrm((L, D_AAA_LORA, RW_WIDTH), D_AAA_LORA ** -0.5)
    inp['rw_g_up'] = nrm((L, D_GATE_LORA, RW_WIDTH), D_GATE_LORA ** -0.5)
    inp['rw_k_k'] = 0.85 + nrm((L, RW_WIDTH), 0.05)
    inp['rw_k_a'] = 1.0 + nrm((L, RW_WIDTH), 0.05)
    inp['rw_r_k'] = nrm((L, RW_HEADS, HEAD_DIM), 0.1)
    inp['rw_gn_g'] = 1.0 + nrm((L, RW_WIDTH), 0.05)
    inp['rw_gn_b'] = nrm((L, RW_WIDTH), 0.02)
    inp['att_sinks'] = nrm((L, ATT_Q_HEADS), 0.5)
    inp['w_out'] = nrm((L, D_MODEL, D_MODEL), D_MODEL ** -0.5 * DN_BETA)
    inp['ln1_g'] = 1.0 + nrm((L, D_MODEL), 0.05)
    inp['ln1_b'] = nrm((L, D_MODEL), 0.02)
    inp['peer_wq'] = nrm((L, D_MODEL, PEER_HEADS * PEER_QDIM), D_MODEL ** -0.5)
    inp['peer_keys'] = nrm((L, PEER_HEADS, 2, N_KEYS, PEER_QDIM // 2), (PEER_QDIM // 2) ** -0.5)
    inp['peer_u'] = nrm((L, N_EXPERTS, D_MODEL), D_MODEL ** -0.5 * DN_BETA)
    inp['peer_v'] = nrm((L, N_EXPERTS, D_MODEL), DN_BETA)
    inp['ln2_g'] = 1.0 + nrm((L, D_MODEL), 0.05)
    inp['ln2_b'] = nrm((L, D_MODEL), 0.02)
    return inp


def reference(x_prompt, x_sample, state_shift, state_wkv, cache_k, cache_v,
              w_in, rw_mu, rw_w0, rw_w_up, rw_a0, rw_a_up, rw_g_up, rw_k_k, rw_k_a, rw_r_k,
              rw_gn_g, rw_gn_b, att_sinks, w_out, ln1_g, ln1_b,
              peer_wq, peer_keys, peer_u, peer_v, ln2_g, ln2_b):
    xp, xs = x_prompt, x_sample
    bp = x_prompt.shape[0]
    sh_p, wkv_p, k_p, v_p = [], [], [], []
    sh_s, wkv_s, k_s, v_s = [], [], [], []
    for l in range(DEPTH):
        lw = (w_in[l], rw_mu[l], rw_w0[l], rw_w_up[l], rw_a0[l], rw_a_up[l], rw_g_up[l],
              rw_k_k[l], rw_k_a[l], rw_r_k[l], rw_gn_g[l], rw_gn_b[l], att_sinks[l], w_out[l],
              ln1_g[l], ln1_b[l], peer_wq[l], peer_keys[l], peer_u[l], peer_v[l], ln2_g[l], ln2_b[l])
        p0 = jnp.zeros((bp, SHIFT_W), xp.dtype)
        s0 = jnp.zeros((bp, RW_HEADS, HEAD_DIM, HEAD_DIM), state_wkv.dtype)
        xp, a1, a2, a3, a4 = trunk_layer(xp, p0, s0, None, None, *lw)
        sh_p.append(a1); wkv_p.append(a2); k_p.append(a3); v_p.append(a4)
        xs, b1, b2, b3, b4 = trunk_layer(xs, state_shift[l], state_wkv[l], cache_k[l], cache_v[l], *lw)
        sh_s.append(b1); wkv_s.append(b2); k_s.append(b3); v_s.append(b4)
    y_prompt, y_sample = xp, xs
    new_shift_prompt = jnp.stack(sh_p)
    new_wkv_prompt = jnp.stack(wkv_p)
    new_k_prompt = jnp.stack(k_p)
    new_v_prompt = jnp.stack(v_p)
    new_shift_sample = jnp.stack(sh_s)
    new_wkv_sample = jnp.stack(wkv_s)
    new_k_sample = jnp.stack(k_s)
    new_v_sample = jnp.stack(v_s)
    return (y_prompt, y_sample, new_shift_prompt, new_wkv_prompt, new_k_prompt, new_v_prompt,
            new_shift_sample, new_wkv_sample, new_k_sample, new_v_sample)
```

```python
import functools

import numpy as np
import jax
import jax.numpy as jnp
from jax import lax
from jax.experimental import pallas as pl
from jax.experimental.pallas import tpu as pltpu

F32, BF16 = jnp.float32, jnp.bfloat16

D_MODEL = 2048
HEAD_DIM = 64
HEADS_PER_GROUP = 4
GROUP_W = HEADS_PER_GROUP * HEAD_DIM
N_GROUPS = D_MODEL // GROUP_W
RKV_W = 3 * D_MODEL
LORA_W = 96 + 96 + 256
LORA_PAD = 512
SHIFT_W = RKV_W + LORA_W
KV_W = 256
GN_EPS = 64e-5
LN_EPS = 1e-5
WINDOW = 128
ATT_SCALE = HEAD_DIM ** -0.5
N_KEYS = 128
N_EXPERTS = N_KEYS * N_KEYS
PEER_HEADS = 8
PEER_TOPK = 16
DN_ALPHA = 2.0 ** 0.25

COL_Q = RKV_W
COL_GATE = COL_Q + D_MODEL
COL_LORA = COL_GATE + 2 * D_MODEL
COL_KV = COL_LORA + LORA_PAD
PROJ_W = COL_KV + 2 * KV_W

VMEM_LIMIT = 56 * 1024 * 1024

_NN = (((1,), (0,)), ((), ()))
_NT = (((1,), (1,)), ((), ()))
_TN = (((0,), (0,)), ((), ()))


def _bdot(a, b, dn=_NN):
    return lax.dot_general(a, b, dn, preferred_element_type=F32)


def _dot(a, b, dn=_NN):
    return _bdot(a.astype(BF16), b.astype(BF16), dn)


def _split(x, n):
    parts, r = [], x
    for i in range(n):
        p = r.astype(BF16)
        parts.append(p)
        if i + 1 < n:
            r = r - p.astype(F32)
    return parts


def _dot3(a, b, dn=_NN):
    a1, a2 = _split(a, 2)
    b1, b2 = _split(b, 2)
    return (_bdot(a1, b2, dn) + _bdot(a2, b1, dn)) + _bdot(a1, b1, dn)


def _dot_xl(a_bf16, b, dn=_NN):
    b1, b2, b3 = _split(b, 3)
    return (_bdot(a_bf16, b3, dn) + _bdot(a_bf16, b2, dn)) + _bdot(a_bf16, b1, dn)


def _dot_xr(a, b_bf16, dn=_NN):
    a1, a2, a3 = _split(a, 3)
    return (_bdot(a3, b_bf16, dn) + _bdot(a2, b_bf16, dn)) + _bdot(a1, b_bf16, dn)


def _sigmoid(x):
    return 1.0 / (1.0 + jnp.exp(-x))


def _iota(shape, dim):
    return lax.broadcasted_iota(jnp.int32, shape, dim)


def _head_of(idx):
    return jnp.right_shift(idx, 6)


def _ones_where(mask):
    return jnp.where(mask, 1.0, 0.0).astype(BF16)


def _params(sem):
    return pltpu.CompilerParams(dimension_semantics=sem, vmem_limit_bytes=VMEM_LIMIT)


def _inproj_body(x_ref, w_ref, o_ref, xb_ref):
    @pl.when(pl.program_id(1) == 0)
    def _():
        xb_ref[...] = x_ref[...].astype(BF16)

    o_ref[...] = _bdot(xb_ref[...], w_ref[...])


def _inproj(x2d, w_bf16, tn=512):
    m = x2d.shape[0]
    tm = min(m, 1024)
    n = w_bf16.shape[1]
    return pl.pallas_call(
        _inproj_body,
        grid=(m // tm, n // tn),
        in_specs=[pl.BlockSpec((tm, D_MODEL), lambda i, j: (i, 0)),
                  pl.BlockSpec((D_MODEL, tn), lambda i, j: (0, j))],
        out_specs=pl.BlockSpec((tm, tn), lambda i, j: (i, j)),
        out_shape=jax.ShapeDtypeStruct((m, n), F32),
        scratch_shapes=[pltpu.VMEM((tm, D_MODEL), BF16)],
        compiler_params=_params(("parallel", "arbitrary")),
        name="inproj",
    )(x2d, w_bf16)


def _tri_inverse(l, c):
    eye = jnp.where(_iota((c, c), 0) == _iota((c, c), 1), 1.0, 0.0)
    x = eye + l
    p = l
    n = 1
    while 2 * n < c:
        p = _dot3(p, p)
        x = x + _dot3(p, x)
        n *= 2
    return x


def _rwkv_body(C, nchunk,
               r_ref, k_ref, v_ref, lo_ref, pr_ref, pk_ref, pv_ref, plo_ref, s0_ref,
               mur_ref, muk_ref, muv_ref, mulo_ref, w0_ref, a0_ref, kk_ref, ka_ref, rk_ref, gng_ref, gnb_ref,
               wup_ref, aup_ref, gup_ref,
               o_ref, sout_ref,
               sbd_ref, sh_ref, shl_ref):
    c = pl.program_id(2)
    lane = _head_of(_iota((C, GROUP_W), 1))
    blockmask = _head_of(_iota((GROUP_W, GROUP_W), 0)) == _head_of(_iota((GROUP_W, GROUP_W), 1))
    block_ones = _ones_where(blockmask)
    rep = _ones_where((_iota((HEAD_DIM, GROUP_W), 1) & (HEAD_DIM - 1)) == _iota((HEAD_DIM, GROUP_W), 0))
    rep_t = _ones_where((_iota((GROUP_W, HEAD_DIM), 0) & (HEAD_DIM - 1)) == _iota((GROUP_W, HEAD_DIM), 1))

    @pl.when(c == 0)
    def _init():
        sh_ref[7:8, 0:GROUP_W] = pr_ref[0]
        sh_ref[7:8, GROUP_W:2 * GROUP_W] = pk_ref[0]
        sh_ref[7:8, 2 * GROUP_W:3 * GROUP_W] = pv_ref[0]
        shl_ref[7:8, :] = plo_ref[0]
        sbd_ref[...] = jnp.where(blockmask, _dot_xr(s0_ref[0], rep), 0.0)

    def lerp(x, ref, lo, hi, mu):
        ref[pl.ds(8, C), lo:hi] = x
        xs = ref[pl.ds(7, C), lo:hi]
        ref[7:8, lo:hi] = x[C - 1:C, :]
        return x + (xs - x) * mu

    r = lerp(r_ref[...], sh_ref, 0, GROUP_W, mur_ref[...])
    k = lerp(k_ref[...], sh_ref, GROUP_W, 2 * GROUP_W, muk_ref[...])
    v = lerp(v_ref[...], sh_ref, 2 * GROUP_W, 3 * GROUP_W, muv_ref[...])
    lo = lerp(lo_ref[...], shl_ref, 0, LORA_PAD, mulo_ref[...])

    wl = w0_ref[...] + _dot(jnp.tanh(lo), wup_ref[...])
    w_log = -(jnp.maximum(-wl, 0.0) + jnp.log1p(jnp.exp(-jnp.abs(wl)))) - 0.5
    logw = -jnp.exp(w_log)
    a = _sigmoid(a0_ref[...] + _dot(lo, aup_ref[...]))
    gate = _dot(_sigmoid(lo), gup_ref[...])

    kk = k * kk_ref[...]
    ss = _dot_xr(kk * kk, block_ones)
    kkn = kk / jnp.maximum(jnp.sqrt(ss), 1e-12)
    k2 = k * (1.0 + (a - 1.0) * ka_ref[...])

    tri = (_iota((C, C), 0) >= _iota((C, C), 1))
    cum = _dot_xl(_ones_where(tri), logw)
    gam = jnp.exp(cum)
    ginv = jnp.exp(-cum)
    gprev = jnp.exp(cum - logw)
    a4 = -(gprev * kkn)
    b4 = kkn * a * ginv
    k4 = k2 * ginv
    r4 = gam * r
    g_last = gam[C - 1:C, :]

    strict = _iota((C, C), 0) > _iota((C, C), 1)
    wm4 = jnp.zeros((C, GROUP_W), F32)
    uk4 = jnp.zeros((C, GROUP_W), F32)
    mkv4 = jnp.zeros((C, GROUP_W), F32)
    mbs = []
    for j in range(HEADS_PER_GROUP):
        lm = lane == j
        am = jnp.where(lm, a4, 0.0)
        rm = jnp.where(lm, r4, 0.0)
        vm = jnp.where(lm, v, 0.0)
        arm = jnp.concatenate([am, rm], axis=0)
        lb_mb = _dot3(arm, b4, _NT)
        lk_mk = _dot3(arm, k4, _NT)
        l_b = jnp.where(strict, lb_mb[:C], 0.0)
        m_b = jnp.where(tri, lb_mb[C:], 0.0)
        l_k = jnp.where(strict, lk_mk[:C], 0.0)
        m_k = jnp.where(tri, lk_mk[C:], 0.0)
        t = _tri_inverse(l_b, C)
        wm4 = wm4 + _dot3(t, am)
        uk4 = uk4 + _dot3(t, _dot3(l_k, vm))
        mkv4 = mkv4 + _dot3(m_k, vm)
        mbs.append(m_b)

    s = sbd_ref[...]
    u = _dot3(wm4, s, _NT) + uk4
    y = _dot3(r4, s, _NT) + mkv4
    for j in range(HEADS_PER_GROUP):
        y = y + _dot3(mbs[j], jnp.where(lane == j, u, 0.0))
    ds = _dot3(u, b4, _TN) + _dot3(v, k4, _TN)
    s_new = (s + jnp.where(blockmask, ds, 0.0)) * g_last
    sbd_ref[...] = s_new

    ym = _dot_xr(y, block_ones) * (1.0 / HEAD_DIM)
    yc = y - ym
    yv = _dot_xr(yc * yc, block_ones) * (1.0 / HEAD_DIM)
    yn = yc * lax.rsqrt(yv + GN_EPS) * gng_ref[...] + gnb_ref[...]
    bonus = _dot_xr(r * k2 * rk_ref[...], block_ones) * v
    o_ref[...] = (yn + bonus) * gate

    @pl.when(c == nchunk - 1)
    def _fin():
        sout_ref[0] = _dot_xr(s_new, rep_t)


def _rwkv(proj, nseq, nchunk, C, prev_rkv, prev_lora, state0, prm):
    rows = nseq * nchunk * C
    gblk = RKV_W // GROUP_W // 3

    def tok(off):
        return pl.BlockSpec((C, GROUP_W), lambda s, g, c: (s * nchunk + c, off + g))

    def prev(off):
        return pl.BlockSpec((1, 1, GROUP_W), lambda s, g, c: (s, 0, off + g))

    def vec(off=0):
        return pl.BlockSpec((1, GROUP_W), lambda s, g, c: (0, off + g))

    up = pl.BlockSpec((LORA_PAD, GROUP_W), lambda s, g, c: (0, g))
    in_specs = [
        tok(0), tok(gblk), tok(2 * gblk),
        pl.BlockSpec((C, LORA_PAD), lambda s, g, c: (s * nchunk + c, COL_LORA // LORA_PAD)),
        prev(0), prev(gblk), prev(2 * gblk),
        pl.BlockSpec((1, 1, LORA_PAD), lambda s, g, c: (s, 0, 0)),
        pl.BlockSpec((1, GROUP_W, HEAD_DIM), lambda s, g, c: (s, g, 0)),
        vec(0), vec(gblk), vec(2 * gblk),
        pl.BlockSpec((1, LORA_PAD), lambda s, g, c: (0, 0)),
        vec(), vec(), vec(), vec(), vec(), vec(), vec(),
        up, up, up,
    ]
    out_specs = [pl.BlockSpec((C, GROUP_W), lambda s, g, c: (s * nchunk + c, g)),
                 pl.BlockSpec((1, GROUP_W, HEAD_DIM), lambda s, g, c: (s, g, 0))]
    return pl.pallas_call(
        functools.partial(_rwkv_body, C, nchunk),
        grid=(nseq, N_GROUPS, nchunk),
        in_specs=in_specs,
        out_specs=out_specs,
        out_shape=[jax.ShapeDtypeStruct((rows, D_MODEL), F32),
                   jax.ShapeDtypeStruct((nseq, D_MODEL, HEAD_DIM), F32)],
        scratch_shapes=[pltpu.VMEM((GROUP_W, GROUP_W), F32),
                        pltpu.VMEM((C + 8, 3 * GROUP_W), F32),
                        pltpu.VMEM((C + 8, LORA_PAD), F32)],
        compiler_params=_params(("arbitrary", "arbitrary", "arbitrary")),
        name="rwkv",
    )(proj, proj, proj, proj, prev_rkv, prev_rkv, prev_rkv, prev_lora, state0,
      prm["mu_rkv"], prm["mu_rkv"], prm["mu_rkv"], prm["mu_lora"],
      prm["w0"], prm["a0"], prm["k_k"], prm["k_a"], prm["r_k"], prm["gn_g"], prm["gn_b"],
      prm["w_up"], prm["a_up"], prm["g_up"])


def _attn_rows(q, kcat, vcat, sink_ref, first_key, store):
    m = q.shape[0]
    kt_all = kcat.T
    qi = _iota((m, 2 * WINDOW), 0)
    ki = _iota((m, 2 * WINDOW), 1)
    allowed = (ki >= qi + 1) & (ki <= qi + WINDOW) & (ki >= first_key)
    lane = _iota((2 * WINDOW, 128), 1)
    zk = jnp.zeros((HEAD_DIM, 2 * WINDOW), F32)
    for kv in range(KV_W // HEAD_DIM):
        kt = kt_all[kv * HEAD_DIM:(kv + 1) * HEAD_DIM, :]
        k2 = jnp.concatenate([jnp.concatenate([kt, zk], axis=1),
                              jnp.concatenate([zk, kt], axis=1)], axis=0).astype(BF16)
        slab = vcat[:, (kv // 2) * 128:(kv // 2 + 1) * 128]
        rolled = pltpu.roll(slab, HEAD_DIM, axis=1)
        lo_src, hi_src = (slab, rolled) if kv % 2 == 0 else (rolled, slab)
        v2 = jnp.concatenate([jnp.where(lane < HEAD_DIM, lo_src, 0.0),
                              jnp.where(lane >= HEAD_DIM, hi_src, 0.0)], axis=0).astype(BF16)
        for pr in range(4):
            col = kv * 512 + pr * 128
            qh = (q[:, col:col + 128] * ATT_SCALE).astype(BF16)
            s = _bdot(qh, k2)
            ps = []
            for hb in range(2):
                head = kv * 8 + pr * 2 + hb
                sink = sink_ref[:, head * 128:head * 128 + 1]
                sh = jnp.where(allowed, s[:, hb * 256:(hb + 1) * 256], -jnp.inf)
                mx = jnp.maximum(jnp.max(sh, axis=1, keepdims=True), sink)
                e = jnp.exp(sh - mx)
                den = jnp.sum(e, axis=1, keepdims=True) + jnp.exp(sink - mx)
                ps.append(e / den)
            p = jnp.concatenate(ps, axis=1).astype(BF16)
            store(col, _bdot(p, v2))


def _attn_prompt_body(q_ref, kvc_ref, kvp_ref, sink_ref, o_ref):
    n = pl.program_id(0)
    kvc = kvc_ref[...]
    kvp = kvp_ref[...]
    kcat = jnp.concatenate([kvp[:, :KV_W], kvc[:, :KV_W]], axis=0)
    vcat = jnp.concatenate([kvp[:, KV_W:], kvc[:, KV_W:]], axis=0)

    def store(col, val):
        o_ref[:, col:col + 128] = val

    _attn_rows(q_ref[...], kcat, vcat, sink_ref, jnp.where(n > 0, 0, WINDOW), store)


def _attn_prompt(proj, sinks_e):
    t = proj.shape[0]
    nb = t // WINDOW
    qb, kvb = COL_Q // D_MODEL, COL_KV // (2 * KV_W)
    return pl.pallas_call(
        _attn_prompt_body,
        grid=(nb,),
        in_specs=[pl.BlockSpec((WINDOW, D_MODEL), lambda n: (n, qb)),
                  pl.BlockSpec((WINDOW, 2 * KV_W), lambda n: (n, kvb)),
                  pl.BlockSpec((WINDOW, 2 * KV_W), lambda n: (jnp.maximum(n - 1, 0), kvb)),
                  pl.BlockSpec((1, 32 * 128), lambda n: (0, 0))],
        out_specs=pl.BlockSpec((WINDOW, D_MODEL), lambda n: (n, 0)),
        out_shape=jax.ShapeDtypeStruct((t, D_MODEL), F32),
        compiler_params=_params(("parallel",)),
        name="attn_prompt",
    )(proj, proj, proj, sinks_e)


def _attn_sample_body(L, bt, q_ref, kvn_ref, ck_ref, cv_ref, sink_ref, o_ref):
    pad = jnp.zeros((WINDOW - L, KV_W), F32)

    def one(b, carry):
        rows = pl.ds(pl.multiple_of(b * L, L), L)
        kvn = kvn_ref[rows, :]
        kcat = jnp.concatenate([ck_ref[b], kvn[:, :KV_W], pad], axis=0)
        vcat = jnp.concatenate([cv_ref[b], kvn[:, KV_W:], pad], axis=0)

        def store(col, val):
            o_ref[rows, col:col + 128] = val

        _attn_rows(q_ref[rows, :], kcat, vcat, sink_ref, 0, store)
        return carry

    lax.fori_loop(0, bt, one, 0)


def _attn_sample(proj, ck, cv, sinks_e, L, bt=16):
    rows = proj.shape[0]
    nb = rows // L
    bt = min(bt, nb)
    qb, kvb = COL_Q // D_MODEL, COL_KV // (2 * KV_W)
    return pl.pallas_call(
        functools.partial(_attn_sample_body, L, bt),
        grid=(nb // bt,),
        in_specs=[pl.BlockSpec((bt * L, D_MODEL), lambda i: (i, qb)),
                  pl.BlockSpec((bt * L, 2 * KV_W), lambda i: (i, kvb)),
                  pl.BlockSpec((bt, WINDOW, KV_W), lambda i: (i, 0, 0)),
                  pl.BlockSpec((bt, WINDOW, KV_W), lambda i: (i, 0, 0)),
                  pl.BlockSpec((1, 32 * 128), lambda i: (0, 0))],
        out_specs=pl.BlockSpec((bt * L, D_MODEL), lambda i: (i, 0)),
        out_shape=jax.ShapeDtypeStruct((rows, D_MODEL), F32),
        compiler_params=_params(("parallel",)),
        name="attn_sample",
    )(proj, proj, ck, cv, sinks_e)


def _layer_norm(x, g, b):
    mu = jnp.mean(x, axis=-1, keepdims=True)
    xc = x - mu
    var = jnp.mean(xc * xc, axis=-1, keepdims=True)
    return xc * lax.rsqrt(var + LN_EPS) * g + b


def _merge_body(ga_ref, gb_ref, oa_ref, ob_ref, x_ref, wout_ref, g_ref, b_ref, h_ref, ht_ref):
    mixed = _sigmoid(ga_ref[...]) * oa_ref[...] + _sigmoid(gb_ref[...]) * ob_ref[...]
    y = _bdot(mixed.astype(BF16), wout_ref[...])
    h = _layer_norm(DN_ALPHA * x_ref[...] + y, g_ref[...], b_ref[...])
    h_ref[...] = h
    ht_ref[...] = h.T.astype(BF16)


def _merge(proj, o_a, o_b, x2d, wout_bf16, ln_g, ln_b, tm=256):
    m = x2d.shape[0]
    tm = min(tm, m)
    gab = COL_GATE // D_MODEL
    row = lambda i: (i, 0)
    return pl.pallas_call(
        _merge_body,
        grid=(m // tm,),
        in_specs=[pl.BlockSpec((tm, D_MODEL), lambda i: (i, gab)),
                  pl.BlockSpec((tm, D_MODEL), lambda i: (i, gab + 1)),
                  pl.BlockSpec((tm, D_MODEL), row),
                  pl.BlockSpec((tm, D_MODEL), row),
                  pl.BlockSpec((tm, D_MODEL), row),
                  pl.BlockSpec((D_MODEL, D_MODEL), lambda i: (0, 0)),
                  pl.BlockSpec((1, D_MODEL), lambda i: (0, 0)),
                  pl.BlockSpec((1, D_MODEL), lambda i: (0, 0))],
        out_specs=[pl.BlockSpec((tm, D_MODEL), row),
                   pl.BlockSpec((D_MODEL, tm), lambda i: (0, i))],
        out_shape=[jax.ShapeDtypeStruct((m, D_MODEL), F32),
                   jax.ShapeDtypeStruct((D_MODEL, m), BF16)],
        compiler_params=_params(("parallel",)),
        name="merge",
    )(proj, proj, o_a, o_b, x2d, wout_bf16, ln_g, ln_b)


def _top_rows(x, count):
    rows = _iota(x.shape, 0)
    big = x.shape[0]
    out = []
    for _ in range(count):
        mx = jnp.max(x, axis=0, keepdims=True)
        out.append(mx)
        first = jnp.min(jnp.where(x == mx, rows, big), axis=0, keepdims=True)
        x = jnp.where(rows == first, -jnp.inf, x)
    return out


def _route_body(tm, h_ref, wq_ref, keys_ref, s0_ref, s1_ref, aux_ref, q_s):
    q = _bdot(h_ref[...].astype(BF16), wq_ref[...])
    for hc in range(2 * PEER_HEADS):
        q_s[hc] = q[:, hc * N_KEYS:(hc + 1) * N_KEYS]
    pairs = [(i, j) for i in range(PEER_TOPK) for j in range(PEER_TOPK) if (i + 1) * (j + 1) <= PEER_TOPK]
    npad = -len(pairs) % 8
    neg = jnp.full((1, tm), -jnp.inf, F32)
    zero = jnp.zeros((1, tm), F32)

    def head(hh, carry):
        s0 = _bdot(keys_ref[hh, 0], q_s[2 * hh].astype(BF16), _NT)
        s1 = _bdot(keys_ref[hh, 1], q_s[2 * hh + 1].astype(BF16), _NT)
        s0_ref[hh] = s0
        s1_ref[hh] = s1
        top0 = _top_rows(s0, PEER_TOPK)
        top1 = _top_rows(s1, PEER_TOPK)
        cand = jnp.concatenate([top0[i] + top1[j] for i, j in pairs] + [neg] * npad, axis=0)
        best = _top_rows(cand, PEER_TOPK)
        mx = best[0]
        z = zero
        for bsum in best:
            z = z + jnp.exp(bsum - mx)
        aux_ref[hh] = jnp.concatenate([best[-1], top0[0], top1[0], 1.0 / z, zero, zero, zero, zero], axis=0)
        return carry

    lax.fori_loop(0, PEER_HEADS, head, 0)


def _route(h, wq_bf16, keys_bf16, tm=256):
    m = h.shape[0]
    tm = min(tm, m)
    tok3 = lambda i: (0, 0, i)
    return pl.pallas_call(
        functools.partial(_route_body, tm),
        grid=(m // tm,),
        in_specs=[pl.BlockSpec((tm, D_MODEL), lambda i: (i, 0)),
                  pl.BlockSpec((D_MODEL, D_MODEL), lambda i: (0, 0)),
                  pl.BlockSpec((PEER_HEADS, 2, N_KEYS, N_KEYS), lambda i: (0, 0, 0, 0))],
        out_specs=[pl.BlockSpec((PEER_HEADS, N_KEYS, tm), tok3),
                   pl.BlockSpec((PEER_HEADS, N_KEYS, tm), tok3),
                   pl.BlockSpec((PEER_HEADS, 8, tm), tok3)],
        out_shape=[jax.ShapeDtypeStruct((PEER_HEADS, N_KEYS, m), F32),
                   jax.ShapeDtypeStruct((PEER_HEADS, N_KEYS, m), F32),
                   jax.ShapeDtypeStruct((PEER_HEADS, 8, m), F32)],
        scratch_shapes=[pltpu.VMEM((2 * PEER_HEADS, tm, N_KEYS), F32)],
        compiler_params=_params(("parallel",)),
        name="peer_route",
    )(h, wq_bf16, keys_bf16)


def _peer_body(te, ne, ht_ref, u_ref, vt_ref, s0_ref, s1_ref, aux_ref, h_ref, g_ref, b_ref, y_ref,
               acc_ref, e1_ref, cf_ref):
    j = pl.program_id(1)

    @pl.when(j == 0)
    def _():
        acc_ref[...] = jnp.zeros_like(acc_ref)
        for hh in range(PEER_HEADS):
            aux = aux_ref[hh]
            e1_ref[hh] = jnp.exp(s1_ref[hh] - aux[2:3, :])
            cf_ref[hh] = jnp.exp(s0_ref[hh] - aux[1:2, :]) * aux[3:4, :]

    st = _bdot(u_ref[...], ht_ref[...])
    act = 0.5 * st * (1.0 + lax.erf(st * np.float32(np.sqrt(0.5))))
    ws = []
    for cc in range(te // N_KEYS):
        c = j * (te // N_KEYS) + cc
        gsum = None
        for hh in range(PEER_HEADS):
            s0row = s0_ref[hh, pl.ds(c, 1), :]
            cfrow = cf_ref[hh, pl.ds(c, 1), :]
            tau = aux_ref[hh, 0:1, :]
            sel = (s1_ref[hh] + s0row) >= tau
            term = jnp.where(sel, e1_ref[hh] * cfrow, 0.0)
            gsum = term if gsum is None else gsum + term
        ws.append((gsum * act[cc * N_KEYS:(cc + 1) * N_KEYS, :]).astype(BF16))
    w = jnp.concatenate(ws, axis=0)
    acc_ref[...] += _bdot(vt_ref[...], w)

    @pl.when(j == ne - 1)
    def _():
        out = acc_ref[...].T
        y_ref[...] = _layer_norm(DN_ALPHA * h_ref[...] + out, g_ref[...], b_ref[...])


def _peer(ht, u_bf16, vt_bf16, s0t, s1t, aux, h, ln_g, ln_b, tm=512, te=512):
    m = h.shape[0]
    tm = min(tm, m)
    ne = N_EXPERTS // te
    tok3 = lambda i, j: (0, 0, i)
    return pl.pallas_call(
        functools.partial(_peer_body, te, ne),
        grid=(m // tm, ne),
        in_specs=[pl.BlockSpec((D_MODEL, tm), lambda i, j: (0, i)),
                  pl.BlockSpec((te, D_MODEL), lambda i, j: (j, 0)),
                  pl.BlockSpec((D_MODEL, te), lambda i, j: (0, j)),
                  pl.BlockSpec((PEER_HEADS, N_KEYS, tm), tok3),
                  pl.BlockSpec((PEER_HEADS, N_KEYS, tm), tok3),
                  pl.BlockSpec((PEER_HEADS, 8, tm), tok3),
                  pl.BlockSpec((tm, D_MODEL), lambda i, j: (i, 0)),
                  pl.BlockSpec((1, D_MODEL), lambda i, j: (0, 0)),
                  pl.BlockSpec((1, D_MODEL), lambda i, j: (0, 0))],
        out_specs=pl.BlockSpec((tm, D_MODEL), lambda i, j: (i, 0)),
        out_shape=jax.ShapeDtypeStruct((m, D_MODEL), F32),
        scratch_shapes=[pltpu.VMEM((D_MODEL, tm), F32),
                        pltpu.VMEM((PEER_HEADS, N_KEYS, tm), F32),
                        pltpu.VMEM((PEER_HEADS, N_KEYS, tm), F32)],
        compiler_params=_params(("parallel", "arbitrary")),
        name="peer_dense",
    )(ht, u_bf16, vt_bf16, s0t, s1t, aux, h, ln_g, ln_b)


def _pad_rows(w, lo, total):
    return jnp.zeros((total, w.shape[1]), w.dtype).at[lo:lo + w.shape[0]].set(w)


def _layer(x2d, nseq, seq_len, chunk, prev_shift, state0, cache, wts):
    proj = _inproj(x2d, wts["w_in"])
    prev_rkv = prev_shift[:, None, :RKV_W]
    prev_lora = jnp.pad(prev_shift[:, None, RKV_W:], ((0, 0), (0, 0), (0, LORA_PAD - LORA_W)))
    o_a, s_last = _rwkv(proj, nseq, seq_len // chunk, chunk, prev_rkv, prev_lora, state0, wts)
    if cache is None:
        o_b = _attn_prompt(proj, wts["sinks"])
    else:
        o_b = _attn_sample(proj, cache[0], cache[1], wts["sinks"], seq_len)
    h, ht = _merge(proj, o_a, o_b, x2d, wts["w_out"], wts["ln1_g"], wts["ln1_b"])
    s0t, s1t, aux = _route(h, wts["peer_wq"], wts["peer_keys"])
    y = _peer(ht, wts["peer_u"], wts["peer_vt"], s0t, s1t, aux, h, wts["ln2_g"], wts["ln2_b"])
    return y, proj, s_last


def kernel(x_prompt, x_sample, state_shift, state_wkv, cache_k, cache_v, w_in, rw_mu, rw_w0, rw_w_up, rw_a0,
           rw_a_up, rw_g_up, rw_k_k, rw_k_a, rw_r_k, rw_gn_g, rw_gn_b, att_sinks, w_out, ln1_g, ln1_b,
           peer_wq, peer_keys, peer_u, peer_v, ln2_g, ln2_b):
    depth = w_in.shape[0]
    assert depth == 1
    l = 0
    bp, tp, _ = x_prompt.shape
    bs, ts, _ = x_sample.shape
    assert bp == 1
    win = cache_k.shape[2]
    assert win == WINDOW and tp % WINDOW == 0 and ts <= 8

    w = w_in[l]
    q0 = SHIFT_W
    k0 = q0 + D_MODEL
    g0 = k0 + 2 * KV_W
    w_perm = jnp.concatenate(
        [w[:, :RKV_W], w[:, q0:q0 + D_MODEL], w[:, g0:g0 + 2 * D_MODEL], w[:, RKV_W:SHIFT_W],
         jnp.zeros((D_MODEL, LORA_PAD - LORA_W), w.dtype), w[:, k0:k0 + 2 * KV_W]], axis=1).astype(BF16)
    mu = rw_mu[l]
    row = lambda v: v.reshape(1, -1)
    wts = dict(
        w_in=w_perm,
        mu_rkv=row(mu[:RKV_W]),
        mu_lora=row(jnp.pad(mu[RKV_W:], (0, LORA_PAD - LORA_W))),
        w0=row(rw_w0[l]), a0=row(rw_a0[l]), k_k=row(rw_k_k[l]), k_a=row(rw_k_a[l]), r_k=row(rw_r_k[l]),
        gn_g=row(rw_gn_g[l]), gn_b=row(rw_gn_b[l]),
        w_up=_pad_rows(rw_w_up[l], 0, LORA_PAD).astype(BF16),
        a_up=_pad_rows(rw_a_up[l], 96, LORA_PAD).astype(BF16),
        g_up=_pad_rows(rw_g_up[l], 192, LORA_PAD).astype(BF16),
        sinks=jnp.repeat(att_sinks[l], 128).reshape(1, 32 * 128),
        w_out=w_out[l].astype(BF16),
        ln1_g=row(ln1_g[l]), ln1_b=row(ln1_b[l]), ln2_g=row(ln2_g[l]), ln2_b=row(ln2_b[l]),
        peer_wq=peer_wq[l].astype(BF16),
        peer_keys=peer_keys[l].astype(BF16),
        peer_u=peer_u[l].astype(BF16),
        peer_vt=peer_v[l].T.astype(BF16),
    )

    def unshift(p_last):
        return jnp.concatenate([p_last[..., :RKV_W], p_last[..., COL_LORA:COL_LORA + LORA_W]], axis=-1)

    chunk_p = 64
    y_p, proj_p, s_p = _layer(x_prompt[0], 1, tp, chunk_p,
                              jnp.zeros((1, SHIFT_W), F32), jnp.zeros((1, D_MODEL, HEAD_DIM), F32), None, wts)
    keep = min(WINDOW, tp)
    kv_p = proj_p[tp - keep:, COL_KV:]
    y_prompt = y_p[None]
    new_shift_prompt = unshift(proj_p[tp - 1])[None, None]
    new_wkv_prompt = s_p.reshape(1, 1, D_MODEL // HEAD_DIM, HEAD_DIM, HEAD_DIM)
    new_k_prompt = kv_p[:, :KV_W].reshape(1, 1, keep, KV_W // HEAD_DIM, HEAD_DIM)
    new_v_prompt = kv_p[:, KV_W:].reshape(1, 1, keep, KV_W // HEAD_DIM, HEAD_DIM)

    ck = cache_k[l].reshape(bs, win, KV_W)
    cv = cache_v[l].reshape(bs, win, KV_W)
    y_s, proj_s, s_s = _layer(x_sample.reshape(bs * ts, D_MODEL), bs, ts, ts,
                              state_shift[l], state_wkv[l].reshape(bs, D_MODEL, HEAD_DIM), (ck, cv), wts)
    proj_s3 = proj_s.reshape(bs, ts, PROJ_W)
    y_sample = y_s.reshape(bs, ts, D_MODEL)
    new_shift_sample = unshift(proj_s3[:, ts - 1])[None]
    new_wkv_sample = s_s.reshape(1, bs, D_MODEL // HEAD_DIM, HEAD_DIM, HEAD_DIM)
    kv_s = proj_s3[:, :, COL_KV:]
    new_k_sample = jnp.concatenate([ck, kv_s[:, :, :KV_W]], axis=1)[:, ts:].reshape(
        1, bs, win, KV_W // HEAD_DIM, HEAD_DIM)
    new_v_sample = jnp.concatenate([cv, kv_s[:, :, KV_W:]], axis=1)[:, ts:].reshape(
        1, bs, win, KV_W // HEAD_DIM, HEAD_DIM)

    return (y_prompt, y_sample, new_shift_prompt, new_wkv_prompt, new_k_prompt, new_v_prompt,
            new_shift_sample, new_wkv_sample, new_k_sample, new_v_sample)
```

```python
import functools

import numpy as np
import jax
import jax.numpy as jnp
from jax import lax
from jax.experimental import pallas as pl
from jax.experimental.pallas import tpu as pltpu

F32, BF16 = jnp.float32, jnp.bfloat16

D_MODEL = 2048
HEAD_DIM = 64
HEADS_PER_GROUP = 2
GROUP_W = HEADS_PER_GROUP * HEAD_DIM
N_GROUPS = D_MODEL // GROUP_W
RKV_W = 3 * D_MODEL
LORA_W = 96 + 96 + 256
LORA_PAD = 512
SHIFT_W = RKV_W + LORA_W
KV_W = 256
GN_EPS = 64e-5
LN_EPS = 1e-5
WINDOW = 128
ATT_SCALE = HEAD_DIM ** -0.5
N_KEYS = 128
N_EXPERTS = N_KEYS * N_KEYS
PEER_HEADS = 8
PEER_TOPK = 16
DN_ALPHA = 2.0 ** 0.25

COL_Q = RKV_W
COL_GATE = COL_Q + D_MODEL
COL_LORA = COL_GATE + 2 * D_MODEL
COL_KV = COL_LORA + LORA_PAD
PROJ_W = COL_KV + 2 * KV_W

VMEM_LIMIT = 56 * 1024 * 1024

_NN = (((1,), (0,)), ((), ()))
_NT = (((1,), (1,)), ((), ()))
_TN = (((0,), (0,)), ((), ()))


def _bdot(a, b, dn=_NN):
    return lax.dot_general(a, b, dn, preferred_element_type=F32)


def _dot(a, b, dn=_NN):
    return _bdot(a.astype(BF16), b.astype(BF16), dn)


def _split(x, n):
    parts, r = [], x
    for i in range(n):
        p = r.astype(BF16)
        parts.append(p)
        if i + 1 < n:
            r = r - p.astype(F32)
    return parts


def _dot3(a, b, dn=_NN):
    a1, a2 = _split(a, 2)
    b1, b2 = _split(b, 2)
    return (_bdot(a1, b2, dn) + _bdot(a2, b1, dn)) + _bdot(a1, b1, dn)


def _dot_xl(a_bf16, b, dn=_NN, passes=3):
    out = None
    for part in reversed(_split(b, passes)):
        t = _bdot(a_bf16, part, dn)
        out = t if out is None else out + t
    return out


def _dot_xr(a, b_bf16, dn=_NN, passes=3):
    out = None
    for part in reversed(_split(a, passes)):
        t = _bdot(part, b_bf16, dn)
        out = t if out is None else out + t
    return out


def _mm(passes):
    return _dot if passes == 1 else _dot3


def _sigmoid(x):
    return 1.0 / (1.0 + jnp.exp(-x))


def _iota(shape, dim):
    return lax.broadcasted_iota(jnp.int32, shape, dim)


def _head_of(idx):
    return jnp.right_shift(idx, 6)


def _ones_where(mask):
    return jnp.where(mask, 1.0, 0.0).astype(BF16)


def _params(sem):
    return pltpu.CompilerParams(dimension_semantics=sem, vmem_limit_bytes=VMEM_LIMIT)


def _inproj_body(x_ref, w_ref, o_ref, xb_ref):
    @pl.when(pl.program_id(1) == 0)
    def _():
        xb_ref[...] = x_ref[...].astype(BF16)

    o_ref[...] = _bdot(xb_ref[...], w_ref[...])


def _inproj(x2d, w_bf16, tn=512):
    m = x2d.shape[0]
    tm = min(m, 1024)
    n = w_bf16.shape[1]
    return pl.pallas_call(
        _inproj_body,
        grid=(m // tm, n // tn),
        in_specs=[pl.BlockSpec((tm, D_MODEL), lambda i, j: (i, 0)),
                  pl.BlockSpec((D_MODEL, tn), lambda i, j: (0, j))],
        out_specs=pl.BlockSpec((tm, tn), lambda i, j: (i, j)),
        out_shape=jax.ShapeDtypeStruct((m, n), F32),
        scratch_shapes=[pltpu.VMEM((tm, D_MODEL), BF16)],
        compiler_params=_params(("parallel", "arbitrary")),
        name="inproj",
    )(x2d, w_bf16)


RWKV_PASSES = dict(seg=2, cum=3, lblk=1, inv=1, apply=1, state=1, gn=2)
ROWS = 64


def _rwkv_body(C, nb, nchunk, gpb, ps,
               r_ref, k_ref, v_ref, lo_ref, pr_ref, pk_ref, pv_ref, plo_ref, s0_ref,
               mur_ref, muk_ref, muv_ref, mulo_ref, w0_ref, a0_ref, kk_ref, ka_ref, rk_ref, gng_ref, gnb_ref,
               wup_ref, aup_ref, gup_ref,
               o_ref, sout_ref,
               sbd_ref, sh_ref, shl_ref):
    assert HEADS_PER_GROUP == 2 and HEADS_PER_GROUP * ROWS == GROUP_W
    c = pl.program_id(2)
    w = gpb * GROUP_W
    log2c = C.bit_length() - 1
    st = HEADS_PER_GROUP * ROWS
    sq = (st, st)
    ri, ci = _iota(sq, 0), _iota(sq, 1)
    same_head = _head_of(ri) == _head_of(ci)
    tr, tc = ri & (ROWS - 1), ci & (ROWS - 1)
    blk = same_head & (jnp.right_shift(tr, log2c) == jnp.right_shift(tc, log2c))
    strict = blk & (tr > tc)
    incl = blk & (tr >= tc)
    eye = jnp.where(ri == ci, 1.0, 0.0)
    block_ones = _ones_where(same_head)
    r64, c64 = _iota((ROWS, ROWS), 0), _iota((ROWS, ROWS), 1)
    tri_seq = _ones_where((jnp.right_shift(r64, log2c) == jnp.right_shift(c64, log2c)) & (r64 >= c64))
    lane_head = _head_of(_iota((ROWS, GROUP_W), 1))
    rep = _ones_where((_iota((HEAD_DIM, GROUP_W), 1) & (HEAD_DIM - 1)) == _iota((HEAD_DIM, GROUP_W), 0))
    rep_t = _ones_where((_iota((GROUP_W, HEAD_DIM), 0) & (HEAD_DIM - 1)) == _iota((GROUP_W, HEAD_DIM), 1))
    mm_l, mm_i, mm_a, mm_s = _mm(ps["lblk"]), _mm(ps["inv"]), _mm(ps["apply"]), _mm(ps["state"])

    @pl.when(c == 0)
    def _init():
        if nb == 1:
            sh_ref[7:8, 0:w] = pr_ref[0:1, :]
            sh_ref[7:8, w:2 * w] = pk_ref[0:1, :]
            sh_ref[7:8, 2 * w:3 * w] = pv_ref[0:1, :]
            shl_ref[7:8, :] = plo_ref[0:1, :]
        for b in range(nb):
            for gi in range(gpb):
                s0 = s0_ref[b, gi * GROUP_W:(gi + 1) * GROUP_W, :]
                sbd_ref[b, gi] = jnp.where(same_head, _dot_xr(s0, rep), 0.0)

    def lerp(x, ref, lo, hi, prev_ref, mu):
        ref[pl.ds(8, ROWS), lo:hi] = x
        xs = ref[pl.ds(7, ROWS), lo:hi]
        ref[7:8, lo:hi] = x[ROWS - 1:ROWS, :]
        if nb > 1:
            first = (_iota(x.shape, 0) & (C - 1)) == 0
            xs = jnp.where(first, prev_ref[...], xs)
        return x + (xs - x) * mu

    r = lerp(r_ref[...], sh_ref, 0, w, pr_ref, mur_ref[...])
    k = lerp(k_ref[...], sh_ref, w, 2 * w, pk_ref, muk_ref[...])
    v = lerp(v_ref[...], sh_ref, 2 * w, 3 * w, pv_ref, muv_ref[...])
    lo = lerp(lo_ref[...], shl_ref, 0, LORA_PAD, plo_ref, mulo_ref[...])

    wl = w0_ref[...] + _dot(jnp.tanh(lo), wup_ref[...])
    w_log = -(jnp.maximum(-wl, 0.0) + jnp.log1p(jnp.exp(-jnp.abs(wl)))) - 0.5
    logw_all = -jnp.exp(w_log)
    a_all = _sigmoid(a0_ref[...] + _dot(lo, aup_ref[...]))
    gate_all = _dot(_sigmoid(lo), gup_ref[...])
    kk_all = k * kk_ref[...]
    k2_all = k * (1.0 + (a_all - 1.0) * ka_ref[...])

    def stack(x):
        return jnp.concatenate([jnp.where(lane_head == j, x, 0.0) for j in range(HEADS_PER_GROUP)], axis=0)

    def fold(x):
        return x[0:ROWS] + x[ROWS:2 * ROWS]

    gs = range(gpb)
    sls = [slice(gi * GROUP_W, (gi + 1) * GROUP_W) for gi in gs]
    rg = [r[:, sl] for sl in sls]
    vg = [v[:, sl] for sl in sls]
    k2 = [k2_all[:, sl] for sl in sls]
    kk = [kk_all[:, sl] for sl in sls]
    ss = [_dot_xr(x * x, block_ones, passes=ps["seg"]) for x in kk]
    kkn = [x / jnp.maximum(jnp.sqrt(s2), 1e-12) for x, s2 in zip(kk, ss)]
    logw = [logw_all[:, sl] for sl in sls]
    cum = [_dot_xl(tri_seq, x, passes=ps["cum"]) for x in logw]
    gam = [jnp.exp(x) for x in cum]
    ginv = [jnp.exp(-x) for x in cum]
    a4 = [-(jnp.exp(cm - lw) * kn) for cm, lw, kn in zip(cum, logw, kkn)]
    b4 = [kn * a_all[:, sl] * gi_ for kn, sl, gi_ in zip(kkn, sls, ginv)]
    k4 = [x * gi_ for x, gi_ in zip(k2, ginv)]
    r4 = [gm * x for gm, x in zip(gam, rg)]

    a_st = [stack(x) for x in a4]
    r_st = [stack(x) for x in r4]
    v_st = [stack(x) for x in vg]
    prod = [mm_l(jnp.concatenate([x, y_], axis=0), jnp.concatenate([z, q_], axis=0), _NT)
            for x, y_, z, q_ in zip(a_st, r_st, b4, k4)]
    low = _iota(sq, 1) < HEAD_DIM

    def diag_blocks(x, mask):
        xr = pltpu.roll(x, HEAD_DIM, axis=1)
        return jnp.where(mask, jnp.where(low, x, xr), 0.0), jnp.where(mask, jnp.where(low, xr, x), 0.0)

    l_bk = [diag_blocks(x[:st], strict) for x in prod]
    m_bk = [diag_blocks(x[st:], incl) for x in prod]
    l_b, l_k = [x[0] for x in l_bk], [x[1] for x in l_bk]
    m_b, m_k = [x[0] for x in m_bk], [x[1] for x in m_bk]
    t = [eye + x for x in l_b]
    pw = l_b
    n = 1
    while 2 * n < C:
        pw = [mm_i(x, x) for x in pw]
        t = [x + mm_i(p_, x) for x, p_ in zip(t, pw)]
        n *= 2
    lkv = [mm_a(x, y_) for x, y_ in zip(l_k, v_st)]
    wm4 = [fold(mm_a(x, y_)) for x, y_ in zip(t, a_st)]
    uk4 = [fold(mm_a(x, y_)) for x, y_ in zip(t, lkv)]
    mkv4 = [fold(mm_a(x, y_)) for x, y_ in zip(m_k, v_st)]

    us = [[None] * nb for _ in gs]
    ys = [[None] * nb for _ in gs]
    for b in range(nb):
        rows = slice(b * C, (b + 1) * C)
        s_old = [sbd_ref[b, gi] for gi in gs]
        for gi in gs:
            us[gi][b] = mm_s(wm4[gi][rows], s_old[gi], _NT) + uk4[gi][rows]
            ys[gi][b] = mm_s(r4[gi][rows], s_old[gi], _NT)
        ds = [mm_s(jnp.concatenate([us[gi][b], vg[gi][rows]], axis=0),
                   jnp.concatenate([b4[gi][rows], k4[gi][rows]], axis=0), _TN) for gi in gs]
        for gi in gs:
            s_new = (s_old[gi] + jnp.where(same_head, ds[gi], 0.0)) * gam[gi][(b + 1) * C - 1:(b + 1) * C, :]
            sbd_ref[b, gi] = s_new

            @pl.when(c == nchunk - 1)
            def _fin():
                sout_ref[b, sls[gi], :] = _dot_xr(s_new, rep_t)

    u = [x[0] if nb == 1 else jnp.concatenate(x, axis=0) for x in us]
    ysum = [x[0] if nb == 1 else jnp.concatenate(x, axis=0) for x in ys]
    yb = [fold(mm_s(x, stack(y_))) for x, y_ in zip(m_b, u)]
    y = [x + y_ + z for x, y_, z in zip(ysum, mkv4, yb)]

    ym = [_dot_xr(x, block_ones, passes=ps["gn"]) * (1.0 / HEAD_DIM) for x in y]
    yc = [x - y_ for x, y_ in zip(y, ym)]
    yv = [_dot_xr(x * x, block_ones, passes=ps["gn"]) * (1.0 / HEAD_DIM) for x in yc]
    bonus = [_dot_xr(rg[gi] * k2[gi] * rk_ref[:, sls[gi]], block_ones, passes=ps["gn"]) * vg[gi] for gi in gs]
    for gi in gs:
        sl = sls[gi]
        yn = yc[gi] * lax.rsqrt(yv[gi] + GN_EPS) * gng_ref[:, sl] + gnb_ref[:, sl]
        o_ref[:, sl] = (yn + bonus[gi]) * gate_all[:, sl]


def _rwkv(proj, nseq, nchunk, C, prev_shift, state0, prm, gpb=None):
    nb = ROWS // C
    if gpb is None:
        gpb = N_GROUPS if nb == 1 else N_GROUPS // 2
    assert nb * C == ROWS and nseq % nb == 0 and (nb == 1 or nchunk == 1)
    rows = nseq * nchunk * C
    w = gpb * GROUP_W
    sec = D_MODEL // w
    prev = jnp.zeros((nseq, C, SHIFT_W), F32).at[:, 0].set(prev_shift).reshape(nseq * C, SHIFT_W)
    prev_rkv = prev[:, :RKV_W]
    prev_lora = jnp.pad(prev[:, RKV_W:], ((0, 0), (0, LORA_PAD - LORA_W)))

    def tok(off):
        return pl.BlockSpec((ROWS, w), lambda s, g, c: (s * nchunk + c, off + g))

    def prv(off):
        return pl.BlockSpec((ROWS, w), lambda s, g, c: (s, off + g))

    def vec(off=0):
        return pl.BlockSpec((1, w), lambda s, g, c: (0, off + g))

    up = pl.BlockSpec((LORA_PAD, w), lambda s, g, c: (0, g))
    in_specs = [
        tok(0), tok(sec), tok(2 * sec),
        pl.BlockSpec((ROWS, LORA_PAD), lambda s, g, c: (s * nchunk + c, COL_LORA // LORA_PAD)),
        prv(0), prv(sec), prv(2 * sec),
        pl.BlockSpec((ROWS, LORA_PAD), lambda s, g, c: (s, 0)),
        pl.BlockSpec((nb, w, HEAD_DIM), lambda s, g, c: (s, g, 0)),
        vec(0), vec(sec), vec(2 * sec),
        pl.BlockSpec((1, LORA_PAD), lambda s, g, c: (0, 0)),
        vec(), vec(), vec(), vec(), vec(), vec(), vec(),
        up, up, up,
    ]
    out_specs = [pl.BlockSpec((ROWS, w), lambda s, g, c: (s * nchunk + c, g)),
                 pl.BlockSpec((nb, w, HEAD_DIM), lambda s, g, c: (s, g, 0))]
    return pl.pallas_call(
        functools.partial(_rwkv_body, C, nb, nchunk, gpb, RWKV_PASSES),
        grid=(nseq // nb, N_GROUPS // gpb, nchunk),
        in_specs=in_specs,
        out_specs=out_specs,
        out_shape=[jax.ShapeDtypeStruct((rows, D_MODEL), F32),
                   jax.ShapeDtypeStruct((nseq, D_MODEL, HEAD_DIM), F32)],
        scratch_shapes=[pltpu.VMEM((nb, gpb, GROUP_W, GROUP_W), F32),
                        pltpu.VMEM((ROWS + 8, 3 * w), F32),
                        pltpu.VMEM((ROWS + 8, LORA_PAD), F32)],
        compiler_params=_params(("arbitrary", "arbitrary", "arbitrary")),
        name="rwkv",
    )(proj, proj, proj, proj, prev_rkv, prev_rkv, prev_rkv, prev_lora, state0,
      prm["mu_rkv"], prm["mu_rkv"], prm["mu_rkv"], prm["mu_lora"],
      prm["w0"], prm["a0"], prm["k_k"], prm["k_a"], prm["r_k"], prm["gn_g"], prm["gn_b"],
      prm["w_up"], prm["a_up"], prm["g_up"])


def _attn_rows(q, kcat, vcat, sink_ref, first_key, store, npr):
    m = q.shape[0]
    kt_all = kcat.T
    qi = _iota((npr * m, 2 * WINDOW), 0) & (m - 1)
    ki = _iota((npr * m, 2 * WINDOW), 1)
    allowed = (ki >= qi + 1) & (ki <= qi + WINDOW) & (ki >= first_key)
    lane = _iota((2 * WINDOW, 128), 1)
    zk = jnp.zeros((HEAD_DIM, 2 * WINDOW), F32)
    for kv in range(KV_W // HEAD_DIM):
        kt = kt_all[kv * HEAD_DIM:(kv + 1) * HEAD_DIM, :]
        k2 = jnp.concatenate([jnp.concatenate([kt, zk], axis=1),
                              jnp.concatenate([zk, kt], axis=1)], axis=0).astype(BF16)
        slab = vcat[:, (kv // 2) * 128:(kv // 2 + 1) * 128]
        rolled = pltpu.roll(slab, HEAD_DIM, axis=1)
        lo_src, hi_src = (slab, rolled) if kv % 2 == 0 else (rolled, slab)
        v2 = jnp.concatenate([jnp.where(lane < HEAD_DIM, lo_src, 0.0),
                              jnp.where(lane >= HEAD_DIM, hi_src, 0.0)], axis=0).astype(BF16)
        for p0 in range(0, 4, npr):
            cols = [kv * 512 + (p0 + i) * 128 for i in range(npr)]
            qh = jnp.concatenate([q[:, col:col + 128] for col in cols], axis=0)
            s = _bdot((qh * ATT_SCALE).astype(BF16), k2)
            ps = []
            for hb in range(2):
                sink = jnp.concatenate(
                    [jnp.broadcast_to(sink_ref[:, (kv * 8 + (p0 + i) * 2 + hb) * 128:
                                               (kv * 8 + (p0 + i) * 2 + hb) * 128 + 1], (m, 1))
                     for i in range(npr)], axis=0)
                sh = jnp.where(allowed, s[:, hb * 256:(hb + 1) * 256], -jnp.inf)
                mx = jnp.maximum(jnp.max(sh, axis=1, keepdims=True), sink)
                e = jnp.exp(sh - mx)
                den = jnp.sum(e, axis=1, keepdims=True) + jnp.exp(sink - mx)
                ps.append(e / den)
            o = _bdot(jnp.concatenate(ps, axis=1).astype(BF16), v2)
            for i, col in enumerate(cols):
                store(col, o[i * m:(i + 1) * m])


def _attn_prompt_body(q_ref, kvc_ref, kvp_ref, sink_ref, o_ref):
    n = pl.program_id(0)
    kvc = kvc_ref[...]
    kvp = kvp_ref[...]
    kcat = jnp.concatenate([kvp[:, :KV_W], kvc[:, :KV_W]], axis=0)
    vcat = jnp.concatenate([kvp[:, KV_W:], kvc[:, KV_W:]], axis=0)

    def store(col, val):
        o_ref[:, col:col + 128] = val

    _attn_rows(q_ref[...], kcat, vcat, sink_ref, jnp.where(n > 0, 0, WINDOW), store, npr=1)


def _attn_prompt(proj, sinks_e):
    t = proj.shape[0]
    nb = t // WINDOW
    qb, kvb = COL_Q // D_MODEL, COL_KV // (2 * KV_W)
    return pl.pallas_call(
        _attn_prompt_body,
        grid=(nb,),
        in_specs=[pl.BlockSpec((WINDOW, D_MODEL), lambda n: (n, qb)),
                  pl.BlockSpec((WINDOW, 2 * KV_W), lambda n: (n, kvb)),
                  pl.BlockSpec((WINDOW, 2 * KV_W), lambda n: (jnp.maximum(n - 1, 0), kvb)),
                  pl.BlockSpec((1, 32 * 128), lambda n: (0, 0))],
        out_specs=pl.BlockSpec((WINDOW, D_MODEL), lambda n: (n, 0)),
        out_shape=jax.ShapeDtypeStruct((t, D_MODEL), F32),
        compiler_params=_params(("parallel",)),
        name="attn_prompt",
    )(proj, proj, proj, sinks_e)


def _attn_sample_body(L, bt, q_ref, kvn_ref, ck_ref, cv_ref, sink_ref, o_ref):
    pad = jnp.zeros((WINDOW - L, KV_W), F32)

    def one(b, carry):
        rows = pl.ds(pl.multiple_of(b * L, L), L)
        kvn = kvn_ref[rows, :]
        kcat = jnp.concatenate([ck_ref[b], kvn[:, :KV_W], pad], axis=0)
        vcat = jnp.concatenate([cv_ref[b], kvn[:, KV_W:], pad], axis=0)

        def store(col, val):
            o_ref[rows, col:col + 128] = val

        _attn_rows(q_ref[rows, :], kcat, vcat, sink_ref, 0, store, npr=4)
        return carry

    lax.fori_loop(0, bt, one, 0, unroll=2)


def _attn_sample(proj, ck, cv, sinks_e, L, bt=16):
    rows = proj.shape[0]
    nb = rows // L
    bt = min(bt, nb)
    qb, kvb = COL_Q // D_MODEL, COL_KV // (2 * KV_W)
    return pl.pallas_call(
        functools.partial(_attn_sample_body, L, bt),
        grid=(nb // bt,),
        in_specs=[pl.BlockSpec((bt * L, D_MODEL), lambda i: (i, qb)),
                  pl.BlockSpec((bt * L, 2 * KV_W), lambda i: (i, kvb)),
                  pl.BlockSpec((bt, WINDOW, KV_W), lambda i: (i, 0, 0)),
                  pl.BlockSpec((bt, WINDOW, KV_W), lambda i: (i, 0, 0)),
                  pl.BlockSpec((1, 32 * 128), lambda i: (0, 0))],
        out_specs=pl.BlockSpec((bt * L, D_MODEL), lambda i: (i, 0)),
        out_shape=jax.ShapeDtypeStruct((rows, D_MODEL), F32),
        compiler_params=_params(("parallel",)),
        name="attn_sample",
    )(proj, proj, ck, cv, sinks_e)


def _layer_norm(x, g, b):
    mu = jnp.mean(x, axis=-1, keepdims=True)
    xc = x - mu
    var = jnp.mean(xc * xc, axis=-1, keepdims=True)
    return xc * lax.rsqrt(var + LN_EPS) * g + b


def _merge_body(ga_ref, gb_ref, oa_ref, ob_ref, x_ref, wout_ref, g_ref, b_ref, h_ref, ht_ref):
    mixed = _sigmoid(ga_ref[...]) * oa_ref[...] + _sigmoid(gb_ref[...]) * ob_ref[...]
    y = _bdot(mixed.astype(BF16), wout_ref[...])
    h = _layer_norm(DN_ALPHA * x_ref[...] + y, g_ref[...], b_ref[...])
    h_ref[...] = h
    ht_ref[...] = h.T.astype(BF16)


def _merge(proj, o_a, o_b, x2d, wout_bf16, ln_g, ln_b, tm=256):
    m = x2d.shape[0]
    tm = min(tm, m)
    gab = COL_GATE // D_MODEL
    row = lambda i: (i, 0)
    return pl.pallas_call(
        _merge_body,
        grid=(m // tm,),
        in_specs=[pl.BlockSpec((tm, D_MODEL), lambda i: (i, gab)),
                  pl.BlockSpec((tm, D_MODEL), lambda i: (i, gab + 1)),
                  pl.BlockSpec((tm, D_MODEL), row),
                  pl.BlockSpec((tm, D_MODEL), row),
                  pl.BlockSpec((tm, D_MODEL), row),
                  pl.BlockSpec((D_MODEL, D_MODEL), lambda i: (0, 0)),
                  pl.BlockSpec((1, D_MODEL), lambda i: (0, 0)),
                  pl.BlockSpec((1, D_MODEL), lambda i: (0, 0))],
        out_specs=[pl.BlockSpec((tm, D_MODEL), row),
                   pl.BlockSpec((D_MODEL, tm), lambda i: (0, i))],
        out_shape=[jax.ShapeDtypeStruct((m, D_MODEL), F32),
                   jax.ShapeDtypeStruct((D_MODEL, m), BF16)],
        compiler_params=_params(("parallel",)),
        name="merge",
    )(proj, proj, o_a, o_b, x2d, wout_bf16, ln_g, ln_b)


def _top_rows(x, count):
    rows = _iota(x.shape, 0)
    big = x.shape[0]
    out = []
    for _ in range(count):
        mx = jnp.max(x, axis=0, keepdims=True)
        out.append(mx)
        first = jnp.min(jnp.where(x == mx, rows, big), axis=0, keepdims=True)
        x = jnp.where(rows == first, -jnp.inf, x)
    return out


def _route_body(tm, h_ref, wq_ref, keys_ref, s0_ref, s1_ref, aux_ref, q_s):
    q = _bdot(h_ref[...].astype(BF16), wq_ref[...])
    for hc in range(2 * PEER_HEADS):
        q_s[hc] = q[:, hc * N_KEYS:(hc + 1) * N_KEYS]
    pairs = [(i, j) for i in range(PEER_TOPK) for j in range(PEER_TOPK) if (i + 1) * (j + 1) <= PEER_TOPK]
    npad = -len(pairs) % 8
    neg = jnp.full((1, tm), -jnp.inf, F32)
    zero = jnp.zeros((1, tm), F32)

    def head(hh, carry):
        s0 = _bdot(keys_ref[hh, 0], q_s[2 * hh].astype(BF16), _NT)
        s1 = _bdot(keys_ref[hh, 1], q_s[2 * hh + 1].astype(BF16), _NT)
        s0_ref[hh] = s0
        for lb in range(tm // 128):
            s1_ref[hh, lb] = s1[:, lb * 128:(lb + 1) * 128]
        top0 = _top_rows(s0, PEER_TOPK)
        top1 = _top_rows(s1, PEER_TOPK)
        cand = jnp.concatenate([top0[i] + top1[j] for i, j in pairs] + [neg] * npad, axis=0)
        best = _top_rows(cand, PEER_TOPK)
        mx = best[0]
        z = zero
        for bsum in best:
            z = z + jnp.exp(bsum - mx)
        aux_ref[hh] = jnp.concatenate([best[-1], top0[0], top1[0], 1.0 / z, zero, zero, zero, zero], axis=0)
        return carry

    lax.fori_loop(0, PEER_HEADS, head, 0)


def _route(h, wq_bf16, keys_bf16, tm=256):
    m = h.shape[0]
    tm = min(tm, m)
    tok3 = lambda i: (0, 0, i)
    return pl.pallas_call(
        functools.partial(_route_body, tm),
        grid=(m // tm,),
        in_specs=[pl.BlockSpec((tm, D_MODEL), lambda i: (i, 0)),
                  pl.BlockSpec((D_MODEL, D_MODEL), lambda i: (0, 0)),
                  pl.BlockSpec((PEER_HEADS, 2, N_KEYS, N_KEYS), lambda i: (0, 0, 0, 0))],
        out_specs=[pl.BlockSpec((PEER_HEADS, N_KEYS, tm), tok3),
                   pl.BlockSpec((PEER_HEADS, tm // 128, N_KEYS, 128), lambda i: (0, i, 0, 0)),
                   pl.BlockSpec((PEER_HEADS, 8, tm), tok3)],
        out_shape=[jax.ShapeDtypeStruct((PEER_HEADS, N_KEYS, m), F32),
                   jax.ShapeDtypeStruct((PEER_HEADS, m // 128, N_KEYS, 128), F32),
                   jax.ShapeDtypeStruct((PEER_HEADS, 8, m), F32)],
        scratch_shapes=[pltpu.VMEM((2 * PEER_HEADS, tm, N_KEYS), F32)],
        compiler_params=_params(("parallel",)),
        name="peer_route",
    )(h, wq_bf16, keys_bf16)


def _peer_body(te, ne, ht_ref, u_ref, vt_ref, s0_ref, s1_ref, aux_ref, h_ref, g_ref, b_ref, y_ref,
               acc_ref, e1_ref, cf_ref, st0_ref, st1_ref, w0_ref, w1_ref):
    j = pl.program_id(1)
    tm = acc_ref.shape[1]

    @pl.when(j == 0)
    def _():
        acc_ref[...] = jnp.zeros_like(acc_ref)
        for ref in (st0_ref, st1_ref, w0_ref, w1_ref):
            ref[...] = jnp.zeros_like(ref)
        for hh in range(PEER_HEADS):
            aux = aux_ref[hh]
            for lb in range(tm // 128):
                e1_ref[hh, lb] = jnp.exp(s1_ref[hh, lb] - aux[2:3, lb * 128:(lb + 1) * 128])
            cf_ref[hh] = jnp.exp(s0_ref[hh] - aux[1:2, :]) * aux[3:4, :]

    def stages(st_new, st_old, w_new, w_old):
        jb = j - 1
        live = (jb >= 0) & (jb < ne)
        jb_c = jnp.clip(jb, 0, ne - 1)
        taus = [jnp.where(live, aux_ref[hh, 0:1, :], jnp.inf) for hh in range(PEER_HEADS)]
        ncc = te // N_KEYS
        s0rows = [[s0_ref[hh, pl.ds(jb_c * ncc + cc, 1), :] for hh in range(PEER_HEADS)] for cc in range(ncc)]
        cfrows = [[cf_ref[hh, pl.ds(jb_c * ncc + cc, 1), :] for hh in range(PEER_HEADS)] for cc in range(ncc)]
        tw = min(256, tm)
        for t0 in range(0, tm, tw):
            ts = slice(t0, t0 + tw)
            lbs = range(t0 // 128, (t0 + tw) // 128)
            acc_ref[:, ts] += _bdot(vt_ref[...], jnp.concatenate([w_old[lb] for lb in lbs], axis=1))
            st = _bdot(u_ref[...], ht_ref[:, ts])
            for lb in lbs:
                st_new[lb] = st[:, lb * 128 - t0:(lb + 1) * 128 - t0]
            for lb in lbs:
                ls = slice(lb * 128, (lb + 1) * 128)
                for cc in range(ncc):
                    rows = slice(cc * N_KEYS, (cc + 1) * N_KEYS)
                    gsum = None
                    for hh in range(PEER_HEADS):
                        sel = (s1_ref[hh, lb] + s0rows[cc][hh][:, ls]) >= taus[hh][:, ls]
                        term = jnp.where(sel, e1_ref[hh, lb] * cfrows[cc][hh][:, ls], 0.0)
                        gsum = term if gsum is None else gsum + term
                    so = st_old[lb, rows, :]
                    act = 0.5 * so * (1.0 + lax.erf(so * np.float32(np.sqrt(0.5))))
                    w_new[lb, rows, :] = (gsum * act).astype(BF16)

    @pl.when(j % 2 == 0)
    def _():
        stages(st0_ref, st1_ref, w1_ref, w0_ref)

    @pl.when(j % 2 == 1)
    def _():
        stages(st1_ref, st0_ref, w0_ref, w1_ref)

    @pl.when(j == ne + 1)
    def _():
        out = acc_ref[...].T
        y_ref[...] = _layer_norm(DN_ALPHA * h_ref[...] + out, g_ref[...], b_ref[...])


def _peer(ht, u_bf16, vt_bf16, s0t, s1t, aux, h, ln_g, ln_b, tm=512, te=512):
    m = h.shape[0]
    tm = min(tm, m)
    ne = N_EXPERTS // te
    tok3 = lambda i, j: (0, 0, i)
    return pl.pallas_call(
        functools.partial(_peer_body, te, ne),
        grid=(m // tm, ne + 2),
        in_specs=[pl.BlockSpec((D_MODEL, tm), lambda i, j: (0, i)),
                  pl.BlockSpec((te, D_MODEL), lambda i, j: (jnp.minimum(j, ne - 1), 0)),
                  pl.BlockSpec((D_MODEL, te), lambda i, j: (0, jnp.clip(j - 2, 0, ne - 1))),
                  pl.BlockSpec((PEER_HEADS, N_KEYS, tm), tok3),
                  pl.BlockSpec((PEER_HEADS, tm // 128, N_KEYS, 128), lambda i, j: (0, i, 0, 0)),
                  pl.BlockSpec((PEER_HEADS, 8, tm), tok3),
                  pl.BlockSpec((tm, D_MODEL), lambda i, j: (i, 0)),
                  pl.BlockSpec((1, D_MODEL), lambda i, j: (0, 0)),
                  pl.BlockSpec((1, D_MODEL), lambda i, j: (0, 0))],
        out_specs=pl.BlockSpec((tm, D_MODEL), lambda i, j: (i, 0)),
        out_shape=jax.ShapeDtypeStruct((m, D_MODEL), F32),
        scratch_shapes=[pltpu.VMEM((D_MODEL, tm), F32),
                        pltpu.VMEM((PEER_HEADS, tm // 128, N_KEYS, 128), F32),
                        pltpu.VMEM((PEER_HEADS, N_KEYS, tm), F32),
                        pltpu.VMEM((tm // 128, te, 128), F32),
                        pltpu.VMEM((tm // 128, te, 128), F32),
                        pltpu.VMEM((tm // 128, te, 128), BF16),
                        pltpu.VMEM((tm // 128, te, 128), BF16)],
        compiler_params=_params(("parallel", "arbitrary")),
        name="peer_dense",
    )(ht, u_bf16, vt_bf16, s0t, s1t, aux, h, ln_g, ln_b)


def _pad_rows(w, lo, total):
    return jnp.zeros((total, w.shape[1]), w.dtype).at[lo:lo + w.shape[0]].set(w)


def _layer(x2d, nseq, seq_len, chunk, prev_shift, state0, cache, wts):
    proj = _inproj(x2d, wts["w_in"])
    o_a, s_last = _rwkv(proj, nseq, seq_len // chunk, chunk, prev_shift, state0, wts)
    if cache is None:
        o_b = _attn_prompt(proj, wts["sinks"])
    else:
        o_b = _attn_sample(proj, cache[0], cache[1], wts["sinks"], seq_len)
    h, ht = _merge(proj, o_a, o_b, x2d, wts["w_out"], wts["ln1_g"], wts["ln1_b"])
    s0t, s1t, aux = _route(h, wts["peer_wq"], wts["peer_keys"])
    y = _peer(ht, wts["peer_u"], wts["peer_vt"], s0t, s1t, aux, h, wts["ln2_g"], wts["ln2_b"])
    return y, proj, s_last


def kernel(x_prompt, x_sample, state_shift, state_wkv, cache_k, cache_v, w_in, rw_mu, rw_w0, rw_w_up, rw_a0,
           rw_a_up, rw_g_up, rw_k_k, rw_k_a, rw_r_k, rw_gn_g, rw_gn_b, att_sinks, w_out, ln1_g, ln1_b,
           peer_wq, peer_keys, peer_u, peer_v, ln2_g, ln2_b):
    depth = w_in.shape[0]
    assert depth == 1
    l = 0
    bp, tp, _ = x_prompt.shape
    bs, ts, _ = x_sample.shape
    assert bp == 1
    win = cache_k.shape[2]
    assert win == WINDOW and tp % WINDOW == 0 and ts <= 8

    w = w_in[l]
    q0 = SHIFT_W
    k0 = q0 + D_MODEL
    g0 = k0 + 2 * KV_W
    w_perm = jnp.concatenate(
        [w[:, :RKV_W], w[:, q0:q0 + D_MODEL], w[:, g0:g0 + 2 * D_MODEL], w[:, RKV_W:SHIFT_W],
         jnp.zeros((D_MODEL, LORA_PAD - LORA_W), w.dtype), w[:, k0:k0 + 2 * KV_W]], axis=1).astype(BF16)
    mu = rw_mu[l]
    row = lambda v: v.reshape(1, -1)
    wts = dict(
        w_in=w_perm,
        mu_rkv=row(mu[:RKV_W]),
        mu_lora=row(jnp.pad(mu[RKV_W:], (0, LORA_PAD - LORA_W))),
        w0=row(rw_w0[l]), a0=row(rw_a0[l]), k_k=row(rw_k_k[l]), k_a=row(rw_k_a[l]), r_k=row(rw_r_k[l]),
        gn_g=row(rw_gn_g[l]), gn_b=row(rw_gn_b[l]),
        w_up=_pad_rows(rw_w_up[l], 0, LORA_PAD).astype(BF16),
        a_up=_pad_rows(rw_a_up[l], 96, LORA_PAD).astype(BF16),
        g_up=_pad_rows(rw_g_up[l], 192, LORA_PAD).astype(BF16),
        sinks=jnp.repeat(att_sinks[l], 128).reshape(1, 32 * 128),
        w_out=w_out[l].astype(BF16),
        ln1_g=row(ln1_g[l]), ln1_b=row(ln1_b[l]), ln2_g=row(ln2_g[l]), ln2_b=row(ln2_b[l]),
        peer_wq=peer_wq[l].astype(BF16),
        peer_keys=peer_keys[l].astype(BF16),
        peer_u=peer_u[l].astype(BF16),
        peer_vt=peer_v[l].T.astype(BF16),
    )

    def unshift(p_last):
        return jnp.concatenate([p_last[..., :RKV_W], p_last[..., COL_LORA:COL_LORA + LORA_W]], axis=-1)

    chunk_p = 64
    y_p, proj_p, s_p = _layer(x_prompt[0], 1, tp, chunk_p,
                              jnp.zeros((1, SHIFT_W), F32), jnp.zeros((1, D_MODEL, HEAD_DIM), F32), None, wts)
    keep = min(WINDOW, tp)
    kv_p = proj_p[tp - keep:, COL_KV:]
    y_prompt = y_p[None]
    new_shift_prompt = unshift(proj_p[tp - 1])[None, None]
    new_wkv_prompt = s_p.reshape(1, 1, D_MODEL // HEAD_DIM, HEAD_DIM, HEAD_DIM)
    new_k_prompt = kv_p[:, :KV_W].reshape(1, 1, keep, KV_W // HEAD_DIM, HEAD_DIM)
    new_v_prompt = kv_p[:, KV_W:].reshape(1, 1, keep, KV_W // HEAD_DIM, HEAD_DIM)

    ck = cache_k[l].reshape(bs, win, KV_W)
    cv = cache_v[l].reshape(bs, win, KV_W)
    y_s, proj_s, s_s = _layer(x_sample.reshape(bs * ts, D_MODEL), bs, ts, ts,
                              state_shift[l], state_wkv[l].reshape(bs, D_MODEL, HEAD_DIM), (ck, cv), wts)
    proj_s3 = proj_s.reshape(bs, ts, PROJ_W)
    y_sample = y_s.reshape(bs, ts, D_MODEL)
    new_shift_sample = unshift(proj_s3[:, ts - 1])[None]
    new_wkv_sample = s_s.reshape(1, bs, D_MODEL // HEAD_DIM, HEAD_DIM, HEAD_DIM)
    kv_s = proj_s3[:, :, COL_KV:]
    new_k_sample = jnp.concatenate([ck, kv_s[:, :, :KV_W]], axis=1)[:, ts:].reshape(
        1, bs, win, KV_W // HEAD_DIM, HEAD_DIM)
    new_v_sample = jnp.concatenate([cv, kv_s[:, :, KV_W:]], axis=1)[:, ts:].reshape(
        1, bs, win, KV_W // HEAD_DIM, HEAD_DIM)

    return (y_prompt, y_sample, new_shift_prompt, new_wkv_prompt, new_k_prompt, new_v_prompt,
            new_shift_sample, new_wkv_sample, new_k_sample, new_v_sample)
```

```python
import functools

import numpy as np
import jax
import jax.numpy as jnp
from jax import lax
from jax.experimental import pallas as pl
from jax.experimental.pallas import tpu as pltpu

F32, BF16 = jnp.float32, jnp.bfloat16

D_MODEL = 2048
HEAD_DIM = 64
HEADS_PER_GROUP = 2
GROUP_W = HEADS_PER_GROUP * HEAD_DIM
N_GROUPS = D_MODEL // GROUP_W
RKV_W = 3 * D_MODEL
LORA_W = 96 + 96 + 256
LORA_PAD = 512
SHIFT_W = RKV_W + LORA_W
KV_W = 256
GN_EPS = 64e-5
LN_EPS = 1e-5
WINDOW = 128
ATT_SCALE = HEAD_DIM ** -0.5
N_KEYS = 128
N_EXPERTS = N_KEYS * N_KEYS
PEER_HEADS = 8
PEER_TOPK = 16
DN_ALPHA = 2.0 ** 0.25

COL_Q = RKV_W
COL_GATE = COL_Q + D_MODEL
COL_LORA = COL_GATE + 2 * D_MODEL
COL_KV = COL_LORA + LORA_PAD
PROJ_W = COL_KV + 2 * KV_W

VMEM_LIMIT = 56 * 1024 * 1024

_NN = (((1,), (0,)), ((), ()))
_NT = (((1,), (1,)), ((), ()))
_TN = (((0,), (0,)), ((), ()))


def _bdot(a, b, dn=_NN):
    return lax.dot_general(a, b, dn, preferred_element_type=F32)


def _dot(a, b, dn=_NN):
    return _bdot(a.astype(BF16), b.astype(BF16), dn)


def _split(x, n):
    parts, r = [], x
    for i in range(n):
        p = r.astype(BF16)
        parts.append(p)
        if i + 1 < n:
            r = r - p.astype(F32)
    return parts


def _dot3(a, b, dn=_NN):
    a1, a2 = _split(a, 2)
    b1, b2 = _split(b, 2)
    return (_bdot(a1, b2, dn) + _bdot(a2, b1, dn)) + _bdot(a1, b1, dn)


def _dot_xl(a_bf16, b, dn=_NN, passes=3):
    out = None
    for part in reversed(_split(b, passes)):
        t = _bdot(a_bf16, part, dn)
        out = t if out is None else out + t
    return out


def _dot_xr(a, b_bf16, dn=_NN, passes=3):
    out = None
    for part in reversed(_split(a, passes)):
        t = _bdot(part, b_bf16, dn)
        out = t if out is None else out + t
    return out


def _mm(passes):
    return _dot if passes == 1 else _dot3


def _sigmoid(x):
    return 1.0 / (1.0 + jnp.exp(-x))


def _iota(shape, dim):
    return lax.broadcasted_iota(jnp.int32, shape, dim)


def _head_of(idx):
    return jnp.right_shift(idx, 6)


def _ones_where(mask):
    return jnp.where(mask, 1.0, 0.0).astype(BF16)


def _params(sem):
    return pltpu.CompilerParams(dimension_semantics=sem, vmem_limit_bytes=VMEM_LIMIT)


def _inproj_body(x_ref, w_ref, o_ref, xb_ref):
    @pl.when(pl.program_id(1) == 0)
    def _():
        xb_ref[...] = x_ref[...].astype(BF16)

    o_ref[...] = _bdot(xb_ref[...], w_ref[...])


def _inproj(x2d, w_bf16, tn=1024):
    m = x2d.shape[0]
    tm = min(m, 1024)
    n = w_bf16.shape[1]
    return pl.pallas_call(
        _inproj_body,
        grid=(m // tm, n // tn),
        in_specs=[pl.BlockSpec((tm, D_MODEL), lambda i, j: (i, 0)),
                  pl.BlockSpec((D_MODEL, tn), lambda i, j: (0, j))],
        out_specs=pl.BlockSpec((tm, tn), lambda i, j: (i, j)),
        out_shape=jax.ShapeDtypeStruct((m, n), F32),
        scratch_shapes=[pltpu.VMEM((tm, D_MODEL), BF16)],
        compiler_params=_params(("parallel", "arbitrary")),
        name="inproj",
    )(x2d, w_bf16)


RWKV_PASSES = dict(seg=2, cum=3, lblk=1, inv=1, apply=1, state=1, gn=2)
ROWS = 64


def _rwkv_body(C, nb, nchunk, gpb, ps,
               r_ref, k_ref, v_ref, lo_ref, pr_ref, pk_ref, pv_ref, plo_ref, s0_ref,
               mur_ref, muk_ref, muv_ref, mulo_ref, w0_ref, a0_ref, kk_ref, ka_ref, rk_ref, gng_ref, gnb_ref,
               wup_ref, aup_ref, gup_ref,
               o_ref, sout_ref,
               sbd_ref, sh_ref, shl_ref):
    assert HEADS_PER_GROUP == 2 and HEADS_PER_GROUP * ROWS == GROUP_W
    c = pl.program_id(2)
    w = gpb * GROUP_W
    log2c = C.bit_length() - 1
    st = HEADS_PER_GROUP * ROWS
    sq = (st, st)
    ri, ci = _iota(sq, 0), _iota(sq, 1)
    same_head = _head_of(ri) == _head_of(ci)
    tr, tc = ri & (ROWS - 1), ci & (ROWS - 1)
    blk = same_head & (jnp.right_shift(tr, log2c) == jnp.right_shift(tc, log2c))
    strict = blk & (tr > tc)
    incl = blk & (tr >= tc)
    eye = jnp.where(ri == ci, 1.0, 0.0)
    block_ones = _ones_where(same_head)
    r64, c64 = _iota((ROWS, ROWS), 0), _iota((ROWS, ROWS), 1)
    tri_seq = _ones_where((jnp.right_shift(r64, log2c) == jnp.right_shift(c64, log2c)) & (r64 >= c64))
    lane_head = _head_of(_iota((ROWS, GROUP_W), 1))
    rep = _ones_where((_iota((HEAD_DIM, GROUP_W), 1) & (HEAD_DIM - 1)) == _iota((HEAD_DIM, GROUP_W), 0))
    rep_t = _ones_where((_iota((GROUP_W, HEAD_DIM), 0) & (HEAD_DIM - 1)) == _iota((GROUP_W, HEAD_DIM), 1))
    mm_l, mm_i, mm_a, mm_s = _mm(ps["lblk"]), _mm(ps["inv"]), _mm(ps["apply"]), _mm(ps["state"])

    @pl.when(c == 0)
    def _init():
        if nb == 1:
            sh_ref[7:8, 0:w] = pr_ref[0:1, :]
            sh_ref[7:8, w:2 * w] = pk_ref[0:1, :]
            sh_ref[7:8, 2 * w:3 * w] = pv_ref[0:1, :]
            shl_ref[7:8, :] = plo_ref[0:1, :]
        for b in range(nb):
            for gi in range(gpb):
                s0 = s0_ref[b, gi * GROUP_W:(gi + 1) * GROUP_W, :]
                sbd_ref[b, gi] = jnp.where(same_head, _dot_xr(s0, rep), 0.0)

    def lerp(x, ref, lo, hi, prev_ref, mu):
        ref[pl.ds(8, ROWS), lo:hi] = x
        xs = ref[pl.ds(7, ROWS), lo:hi]
        ref[7:8, lo:hi] = x[ROWS - 1:ROWS, :]
        if nb > 1:
            first = (_iota(x.shape, 0) & (C - 1)) == 0
            xs = jnp.where(first, prev_ref[...], xs)
        return x + (xs - x) * mu

    r = lerp(r_ref[...], sh_ref, 0, w, pr_ref, mur_ref[...])
    k = lerp(k_ref[...], sh_ref, w, 2 * w, pk_ref, muk_ref[...])
    v = lerp(v_ref[...], sh_ref, 2 * w, 3 * w, pv_ref, muv_ref[...])
    lo = lerp(lo_ref[...], shl_ref, 0, LORA_PAD, plo_ref, mulo_ref[...])

    wl = w0_ref[...] + _dot(jnp.tanh(lo), wup_ref[...])
    w_log = -(jnp.maximum(-wl, 0.0) + jnp.log1p(jnp.exp(-jnp.abs(wl)))) - 0.5
    logw_all = -jnp.exp(w_log)
    a_all = _sigmoid(a0_ref[...] + _dot(lo, aup_ref[...]))
    gate_all = _dot(_sigmoid(lo), gup_ref[...])
    kk_all = k * kk_ref[...]
    k2_all = k * (1.0 + (a_all - 1.0) * ka_ref[...])

    def stack(x):
        return jnp.concatenate([jnp.where(lane_head == j, x, 0.0) for j in range(HEADS_PER_GROUP)], axis=0)

    def fold(x):
        return x[0:ROWS] + x[ROWS:2 * ROWS]

    gs = range(gpb)
    sls = [slice(gi * GROUP_W, (gi + 1) * GROUP_W) for gi in gs]
    rg = [r[:, sl] for sl in sls]
    vg = [v[:, sl] for sl in sls]
    k2 = [k2_all[:, sl] for sl in sls]
    kk = [kk_all[:, sl] for sl in sls]
    ss = [_dot_xr(x * x, block_ones, passes=ps["seg"]) for x in kk]
    kkn = [x / jnp.maximum(jnp.sqrt(s2), 1e-12) for x, s2 in zip(kk, ss)]
    logw = [logw_all[:, sl] for sl in sls]
    cum = [_dot_xl(tri_seq, x, passes=ps["cum"]) for x in logw]
    gam = [jnp.exp(x) for x in cum]
    ginv = [jnp.exp(-x) for x in cum]
    a4 = [-(jnp.exp(cm - lw) * kn) for cm, lw, kn in zip(cum, logw, kkn)]
    b4 = [kn * a_all[:, sl] * gi_ for kn, sl, gi_ in zip(kkn, sls, ginv)]
    k4 = [x * gi_ for x, gi_ in zip(k2, ginv)]
    r4 = [gm * x for gm, x in zip(gam, rg)]

    a_st = [stack(x) for x in a4]
    r_st = [stack(x) for x in r4]
    v_st = [stack(x) for x in vg]
    prod = [mm_l(jnp.concatenate([x, y_], axis=0), jnp.concatenate([z, q_], axis=0), _NT)
            for x, y_, z, q_ in zip(a_st, r_st, b4, k4)]
    low = _iota(sq, 1) < HEAD_DIM

    def diag_blocks(x, mask):
        xr = pltpu.roll(x, HEAD_DIM, axis=1)
        return jnp.where(mask, jnp.where(low, x, xr), 0.0), jnp.where(mask, jnp.where(low, xr, x), 0.0)

    l_bk = [diag_blocks(x[:st], strict) for x in prod]
    m_bk = [diag_blocks(x[st:], incl) for x in prod]
    l_b, l_k = [x[0] for x in l_bk], [x[1] for x in l_bk]
    m_b, m_k = [x[0] for x in m_bk], [x[1] for x in m_bk]
    t = [eye + x for x in l_b]
    pw = [mm_i(x, x) for x in l_b]
    cov = 2
    while cov < C:
        if 2 * cov >= C:
            t = [x + mm_i(p_, x) for x, p_ in zip(t, pw)]
        else:
            both = [mm_i(p_, jnp.concatenate([p_, x], axis=1)) for x, p_ in zip(t, pw)]
            pw = [x[:, :st] for x in both]
            t = [x + y_[:, st:] for x, y_ in zip(t, both)]
        cov *= 2
    lm_v = [mm_a(jnp.concatenate([x, y_], axis=0), z) for x, y_, z in zip(l_k, m_k, v_st)]
    mkv4 = [fold(x[st:]) for x in lm_v]
    t_av = [mm_a(x, jnp.concatenate([y_, z[:st]], axis=1)) for x, y_, z in zip(t, a_st, lm_v)]
    wm4 = [fold(x[:, :st]) for x in t_av]
    uk4 = [fold(x[:, st:]) for x in t_av]

    us = [[None] * nb for _ in gs]
    ys = [[None] * nb for _ in gs]
    for b in range(nb):
        rows = slice(b * C, (b + 1) * C)
        s_old = [sbd_ref[b, gi] for gi in gs]
        uy = [mm_s(jnp.concatenate([wm4[gi][rows], r4[gi][rows]], axis=0), s_old[gi], _NT) for gi in gs]
        for gi in gs:
            us[gi][b] = uy[gi][:C] + uk4[gi][rows]
            ys[gi][b] = uy[gi][C:]
        ds = [mm_s(jnp.concatenate([us[gi][b], vg[gi][rows]], axis=0),
                   jnp.concatenate([b4[gi][rows], k4[gi][rows]], axis=0), _TN) for gi in gs]
        for gi in gs:
            s_new = (s_old[gi] + jnp.where(same_head, ds[gi], 0.0)) * gam[gi][(b + 1) * C - 1:(b + 1) * C, :]
            sbd_ref[b, gi] = s_new

            @pl.when(c == nchunk - 1)
            def _fin():
                sout_ref[b, sls[gi], :] = _dot_xr(s_new, rep_t)

    u = [x[0] if nb == 1 else jnp.concatenate(x, axis=0) for x in us]
    ysum = [x[0] if nb == 1 else jnp.concatenate(x, axis=0) for x in ys]
    yb = [fold(mm_s(x, stack(y_))) for x, y_ in zip(m_b, u)]
    y = [x + y_ + z for x, y_, z in zip(ysum, mkv4, yb)]

    ym = [_dot_xr(x, block_ones, passes=ps["gn"]) * (1.0 / HEAD_DIM) for x in y]
    yc = [x - y_ for x, y_ in zip(y, ym)]
    yv = [_dot_xr(x * x, block_ones, passes=ps["gn"]) * (1.0 / HEAD_DIM) for x in yc]
    bonus = [_dot_xr(rg[gi] * k2[gi] * rk_ref[:, sls[gi]], block_ones, passes=ps["gn"]) * vg[gi] for gi in gs]
    for gi in gs:
        sl = sls[gi]
        yn = yc[gi] * lax.rsqrt(yv[gi] + GN_EPS) * gng_ref[:, sl] + gnb_ref[:, sl]
        o_ref[:, sl] = (yn + bonus[gi]) * gate_all[:, sl]


def _rwkv(proj, nseq, nchunk, C, prev_shift, state0, prm, gpb=None):
    nb = ROWS // C
    if gpb is None:
        gpb = N_GROUPS if nb == 1 else N_GROUPS // 2
    assert nb * C == ROWS and nseq % nb == 0 and (nb == 1 or nchunk == 1)
    rows = nseq * nchunk * C
    w = gpb * GROUP_W
    sec = D_MODEL // w
    prev = jnp.zeros((nseq, C, SHIFT_W), F32).at[:, 0].set(prev_shift).reshape(nseq * C, SHIFT_W)
    prev_rkv = prev[:, :RKV_W]
    prev_lora = jnp.pad(prev[:, RKV_W:], ((0, 0), (0, LORA_PAD - LORA_W)))

    def tok(off):
        return pl.BlockSpec((ROWS, w), lambda s, g, c: (s * nchunk + c, off + g))

    def prv(off):
        return pl.BlockSpec((ROWS, w), lambda s, g, c: (s, off + g))

    def vec(off=0):
        return pl.BlockSpec((1, w), lambda s, g, c: (0, off + g))

    up = pl.BlockSpec((LORA_PAD, w), lambda s, g, c: (0, g))
    in_specs = [
        tok(0), tok(sec), tok(2 * sec),
        pl.BlockSpec((ROWS, LORA_PAD), lambda s, g, c: (s * nchunk + c, COL_LORA // LORA_PAD)),
        prv(0), prv(sec), prv(2 * sec),
        pl.BlockSpec((ROWS, LORA_PAD), lambda s, g, c: (s, 0)),
        pl.BlockSpec((nb, w, HEAD_DIM), lambda s, g, c: (s, g, 0)),
        vec(0), vec(sec), vec(2 * sec),
        pl.BlockSpec((1, LORA_PAD), lambda s, g, c: (0, 0)),
        vec(), vec(), vec(), vec(), vec(), vec(), vec(),
        up, up, up,
    ]
    out_specs = [pl.BlockSpec((ROWS, w), lambda s, g, c: (s * nchunk + c, g)),
                 pl.BlockSpec((nb, w, HEAD_DIM), lambda s, g, c: (s, g, 0))]
    return pl.pallas_call(
        functools.partial(_rwkv_body, C, nb, nchunk, gpb, RWKV_PASSES),
        grid=(nseq // nb, N_GROUPS // gpb, nchunk),
        in_specs=in_specs,
        out_specs=out_specs,
        out_shape=[jax.ShapeDtypeStruct((rows, D_MODEL), F32),
                   jax.ShapeDtypeStruct((nseq, D_MODEL, HEAD_DIM), F32)],
        scratch_shapes=[pltpu.VMEM((nb, gpb, GROUP_W, GROUP_W), F32),
                        pltpu.VMEM((ROWS + 8, 3 * w), F32),
                        pltpu.VMEM((ROWS + 8, LORA_PAD), F32)],
        compiler_params=_params(("arbitrary", "arbitrary", "arbitrary")),
        name="rwkv",
    )(proj, proj, proj, proj, prev_rkv, prev_rkv, prev_rkv, prev_lora, state0,
      prm["mu_rkv"], prm["mu_rkv"], prm["mu_rkv"], prm["mu_lora"],
      prm["w0"], prm["a0"], prm["k_k"], prm["k_a"], prm["r_k"], prm["gn_g"], prm["gn_b"],
      prm["w_up"], prm["a_up"], prm["g_up"])


def _attn_rows(q, kcat, vcat, sink_ref, first_key, store, npr):
    m = q.shape[0]
    kt_all = kcat.T
    qi = _iota((npr * m, 2 * WINDOW), 0) & (m - 1)
    ki = _iota((npr * m, 2 * WINDOW), 1)
    allowed = (ki >= qi + 1) & (ki <= qi + WINDOW) & (ki >= first_key)
    lane = _iota((2 * WINDOW, 128), 1)
    zk = jnp.zeros((HEAD_DIM, 2 * WINDOW), F32)
    nkv = KV_W // HEAD_DIM
    k2s, v2s = [], []
    for kv in range(nkv):
        kt = kt_all[kv * HEAD_DIM:(kv + 1) * HEAD_DIM, :]
        k2s.append(jnp.concatenate([jnp.concatenate([kt, zk], axis=1),
                                    jnp.concatenate([zk, kt], axis=1)], axis=0).astype(BF16))
        slab = vcat[:, (kv // 2) * 128:(kv // 2 + 1) * 128]
        rolled = pltpu.roll(slab, HEAD_DIM, axis=1)
        lo_src, hi_src = (slab, rolled) if kv % 2 == 0 else (rolled, slab)
        v2s.append(jnp.concatenate([jnp.where(lane < HEAD_DIM, lo_src, 0.0),
                                    jnp.where(lane >= HEAD_DIM, hi_src, 0.0)], axis=0).astype(BF16))
    items = [(kv, p0) for kv in range(nkv) for p0 in range(0, 4, npr)]
    cols = [[kv * 512 + (p0 + i) * 128 for i in range(npr)] for kv, p0 in items]
    scores = [_bdot((jnp.concatenate([q[:, col:col + 128] for col in cs], axis=0) * ATT_SCALE).astype(BF16),
                    k2s[kv]) for (kv, _), cs in zip(items, cols)]
    probs = []
    for (kv, p0), s in zip(items, scores):
        ps = []
        for hb in range(2):
            sink = jnp.concatenate(
                [jnp.broadcast_to(sink_ref[:, (kv * 8 + (p0 + i) * 2 + hb) * 128:
                                           (kv * 8 + (p0 + i) * 2 + hb) * 128 + 1], (m, 1))
                 for i in range(npr)], axis=0)
            sh = jnp.where(allowed, s[:, hb * 256:(hb + 1) * 256], -jnp.inf)
            mx = jnp.maximum(jnp.max(sh, axis=1, keepdims=True), sink)
            e = jnp.exp(sh - mx)
            den = jnp.sum(e, axis=1, keepdims=True) + jnp.exp(sink - mx)
            ps.append(e * (1.0 / den))
        probs.append(jnp.concatenate(ps, axis=1).astype(BF16))
    outs = [_bdot(p, v2s[kv]) for (kv, _), p in zip(items, probs)]
    for cs, o in zip(cols, outs):
        for i, col in enumerate(cs):
            store(col, o[i * m:(i + 1) * m])


def _attn_prompt_body(q_ref, kvc_ref, kvp_ref, sink_ref, o_ref):
    n = pl.program_id(0)
    kvc = kvc_ref[...]
    kvp = kvp_ref[...]
    kcat = jnp.concatenate([kvp[:, :KV_W], kvc[:, :KV_W]], axis=0)
    vcat = jnp.concatenate([kvp[:, KV_W:], kvc[:, KV_W:]], axis=0)

    def store(col, val):
        o_ref[:, col:col + 128] = val

    _attn_rows(q_ref[...], kcat, vcat, sink_ref, jnp.where(n > 0, 0, WINDOW), store, npr=1)


def _attn_prompt(proj, sinks_e):
    t = proj.shape[0]
    nb = t // WINDOW
    qb, kvb = COL_Q // D_MODEL, COL_KV // (2 * KV_W)
    return pl.pallas_call(
        _attn_prompt_body,
        grid=(nb,),
        in_specs=[pl.BlockSpec((WINDOW, D_MODEL), lambda n: (n, qb)),
                  pl.BlockSpec((WINDOW, 2 * KV_W), lambda n: (n, kvb)),
                  pl.BlockSpec((WINDOW, 2 * KV_W), lambda n: (jnp.maximum(n - 1, 0), kvb)),
                  pl.BlockSpec((1, 32 * 128), lambda n: (0, 0))],
        out_specs=pl.BlockSpec((WINDOW, D_MODEL), lambda n: (n, 0)),
        out_shape=jax.ShapeDtypeStruct((t, D_MODEL), F32),
        compiler_params=_params(("parallel",)),
        name="attn_prompt",
    )(proj, proj, proj, sinks_e)


def _attn_sample_body(L, bt, q_ref, kvn_ref, ck_ref, cv_ref, sink_ref, o_ref):
    pad = jnp.zeros((WINDOW - L, KV_W), F32)

    def one(b, carry):
        rows = pl.ds(pl.multiple_of(b * L, L), L)
        kvn = kvn_ref[rows, :]
        kcat = jnp.concatenate([ck_ref[b], kvn[:, :KV_W], pad], axis=0)
        vcat = jnp.concatenate([cv_ref[b], kvn[:, KV_W:], pad], axis=0)

        def store(col, val):
            o_ref[rows, col:col + 128] = val

        _attn_rows(q_ref[rows, :], kcat, vcat, sink_ref, 0, store, npr=4)
        return carry

    lax.fori_loop(0, bt, one, 0, unroll=2)


def _attn_sample(proj, ck, cv, sinks_e, L, bt=16):
    rows = proj.shape[0]
    nb = rows // L
    bt = min(bt, nb)
    qb, kvb = COL_Q // D_MODEL, COL_KV // (2 * KV_W)
    return pl.pallas_call(
        functools.partial(_attn_sample_body, L, bt),
        grid=(nb // bt,),
        in_specs=[pl.BlockSpec((bt * L, D_MODEL), lambda i: (i, qb)),
                  pl.BlockSpec((bt * L, 2 * KV_W), lambda i: (i, kvb)),
                  pl.BlockSpec((bt, WINDOW, KV_W), lambda i: (i, 0, 0)),
                  pl.BlockSpec((bt, WINDOW, KV_W), lambda i: (i, 0, 0)),
                  pl.BlockSpec((1, 32 * 128), lambda i: (0, 0))],
        out_specs=pl.BlockSpec((bt * L, D_MODEL), lambda i: (i, 0)),
        out_shape=jax.ShapeDtypeStruct((rows, D_MODEL), F32),
        compiler_params=_params(("parallel",)),
        name="attn_sample",
    )(proj, proj, ck, cv, sinks_e)


def _layer_norm(x, g, b):
    mu = jnp.mean(x, axis=-1, keepdims=True)
    xc = x - mu
    var = jnp.mean(xc * xc, axis=-1, keepdims=True)
    return xc * lax.rsqrt(var + LN_EPS) * g + b


def _merge_body(ga_ref, gb_ref, oa_ref, ob_ref, x_ref, wout_ref, g_ref, b_ref, h_ref, ht_ref):
    mixed = _sigmoid(ga_ref[...]) * oa_ref[...] + _sigmoid(gb_ref[...]) * ob_ref[...]
    y = _bdot(mixed.astype(BF16), wout_ref[...])
    h = _layer_norm(DN_ALPHA * x_ref[...] + y, g_ref[...], b_ref[...])
    h_ref[...] = h
    ht_ref[...] = h.T.astype(BF16)


def _merge(proj, o_a, o_b, x2d, wout_bf16, ln_g, ln_b, tm=256):
    m = x2d.shape[0]
    tm = min(tm, m)
    gab = COL_GATE // D_MODEL
    row = lambda i: (i, 0)
    return pl.pallas_call(
        _merge_body,
        grid=(m // tm,),
        in_specs=[pl.BlockSpec((tm, D_MODEL), lambda i: (i, gab)),
                  pl.BlockSpec((tm, D_MODEL), lambda i: (i, gab + 1)),
                  pl.BlockSpec((tm, D_MODEL), row),
                  pl.BlockSpec((tm, D_MODEL), row),
                  pl.BlockSpec((tm, D_MODEL), row),
                  pl.BlockSpec((D_MODEL, D_MODEL), lambda i: (0, 0)),
                  pl.BlockSpec((1, D_MODEL), lambda i: (0, 0)),
                  pl.BlockSpec((1, D_MODEL), lambda i: (0, 0))],
        out_specs=[pl.BlockSpec((tm, D_MODEL), row),
                   pl.BlockSpec((D_MODEL, tm), lambda i: (0, i))],
        out_shape=[jax.ShapeDtypeStruct((m, D_MODEL), F32),
                   jax.ShapeDtypeStruct((D_MODEL, m), BF16)],
        compiler_params=_params(("parallel",)),
        name="merge",
    )(proj, proj, o_a, o_b, x2d, wout_bf16, ln_g, ln_b)


def _top_rows(x, count):
    return _top_rows_multi([x], count)[0]


def _top_rows_multi(xs, count):
    rows = _iota(xs[0].shape, 0)
    big = xs[0].shape[0]
    outs = [[] for _ in xs]
    for _ in range(count):
        mxs = [jnp.max(x, axis=0, keepdims=True) for x in xs]
        for o, mx in zip(outs, mxs):
            o.append(mx)
        firsts = [jnp.min(jnp.where(x == mx, rows, big), axis=0, keepdims=True) for x, mx in zip(xs, mxs)]
        xs = [jnp.where(rows == f, -jnp.inf, x) for x, f in zip(xs, firsts)]
    return outs


def _route_body(tm, h_ref, wq_ref, keys_ref, s0_ref, s1_ref, aux_ref, q_s):
    q = _bdot(h_ref[...].astype(BF16), wq_ref[...])
    for hc in range(2 * PEER_HEADS):
        q_s[hc] = q[:, hc * N_KEYS:(hc + 1) * N_KEYS]
    pairs = [(i, j) for i in range(PEER_TOPK) for j in range(PEER_TOPK) if (i + 1) * (j + 1) <= PEER_TOPK]
    npad = -len(pairs) % 8
    neg = jnp.full((1, tm), -jnp.inf, F32)
    zero = jnp.zeros((1, tm), F32)

    def head(hh, carry):
        s0 = _bdot(keys_ref[hh, 0], q_s[2 * hh].astype(BF16), _NT)
        s1 = _bdot(keys_ref[hh, 1], q_s[2 * hh + 1].astype(BF16), _NT)
        s0_ref[hh] = s0
        for lb in range(tm // 128):
            s1_ref[hh, lb] = s1[:, lb * 128:(lb + 1) * 128]
        top0, top1 = _top_rows_multi([s0, s1], PEER_TOPK)
        cand = jnp.concatenate([top0[i] + top1[j] for i, j in pairs] + [neg] * npad, axis=0)
        best = _top_rows(cand, PEER_TOPK)
        mx = best[0]
        z = zero
        for bsum in best:
            z = z + jnp.exp(bsum - mx)
        aux_ref[hh] = jnp.concatenate([best[-1], top0[0], top1[0], 1.0 / z, zero, zero, zero, zero], axis=0)
        return carry

    lax.fori_loop(0, PEER_HEADS, head, 0)


def _route(h, wq_bf16, keys_bf16, tm=256):
    m = h.shape[0]
    tm = min(tm, m)
    tok3 = lambda i: (0, 0, i)
    return pl.pallas_call(
        functools.partial(_route_body, tm),
        grid=(m // tm,),
        in_specs=[pl.BlockSpec((tm, D_MODEL), lambda i: (i, 0)),
                  pl.BlockSpec((D_MODEL, D_MODEL), lambda i: (0, 0)),
                  pl.BlockSpec((PEER_HEADS, 2, N_KEYS, N_KEYS), lambda i: (0, 0, 0, 0))],
        out_specs=[pl.BlockSpec((PEER_HEADS, N_KEYS, tm), tok3),
                   pl.BlockSpec((PEER_HEADS, tm // 128, N_KEYS, 128), lambda i: (0, i, 0, 0)),
                   pl.BlockSpec((PEER_HEADS, 8, tm), tok3)],
        out_shape=[jax.ShapeDtypeStruct((PEER_HEADS, N_KEYS, m), F32),
                   jax.ShapeDtypeStruct((PEER_HEADS, m // 128, N_KEYS, 128), F32),
                   jax.ShapeDtypeStruct((PEER_HEADS, 8, m), F32)],
        scratch_shapes=[pltpu.VMEM((2 * PEER_HEADS, tm, N_KEYS), F32)],
        compiler_params=_params(("parallel",)),
        name="peer_route",
    )(h, wq_bf16, keys_bf16)


def _peer_body(te, ne, ht_ref, u_ref, vt_ref, s0_ref, s1_ref, aux_ref, h_ref, g_ref, b_ref, y_ref,
               acc_ref, e1_ref, cf_ref, st0_ref, st1_ref, w0_ref, w1_ref):
    j = pl.program_id(1)
    tm = acc_ref.shape[1]

    @pl.when(j == 0)
    def _():
        acc_ref[...] = jnp.zeros_like(acc_ref)
        for ref in (st0_ref, st1_ref, w0_ref, w1_ref):
            ref[...] = jnp.zeros_like(ref)
        for hh in range(PEER_HEADS):
            aux = aux_ref[hh]
            for lb in range(tm // 128):
                e1_ref[hh, lb] = jnp.exp(s1_ref[hh, lb] - aux[2:3, lb * 128:(lb + 1) * 128])
            cf_ref[hh] = jnp.exp(s0_ref[hh] - aux[1:2, :]) * aux[3:4, :]

    def stages(st_new, st_old, w_new, w_old):
        jb = j - 1
        live = (jb >= 0) & (jb < ne)
        jb_c = jnp.clip(jb, 0, ne - 1)
        taus = [jnp.where(live, aux_ref[hh, 0:1, :], jnp.inf) for hh in range(PEER_HEADS)]
        ncc = te // N_KEYS
        s0rows = [[s0_ref[hh, pl.ds(jb_c * ncc + cc, 1), :] for hh in range(PEER_HEADS)] for cc in range(ncc)]
        cfrows = [[cf_ref[hh, pl.ds(jb_c * ncc + cc, 1), :] for hh in range(PEER_HEADS)] for cc in range(ncc)]
        tw = min(256, tm)
        mxu_pieces, gate_tiles = [], []
        for t0 in range(0, tm, tw):
            ts = slice(t0, t0 + tw)
            lbs = range(t0 // 128, (t0 + tw) // 128)

            def mix(mr, ts=ts, lbs=lbs):
                acc_ref[mr, ts] += _bdot(vt_ref[mr, :], jnp.concatenate([w_old[lb] for lb in lbs], axis=1))

            def score(er, t0=t0, ts=ts, lbs=lbs):
                st = _bdot(u_ref[er, :], ht_ref[:, ts])
                for lb in lbs:
                    st_new[lb, er, :] = st[:, lb * 128 - t0:(lb + 1) * 128 - t0]

            for q in range(4):
                mxu_pieces.append(functools.partial(mix, slice(q * (D_MODEL // 4), (q + 1) * (D_MODEL // 4))))
            for q in range(2):
                mxu_pieces.append(functools.partial(score, slice(q * (te // 2), (q + 1) * (te // 2))))

            def gate(lb, cc):
                ls = slice(lb * 128, (lb + 1) * 128)
                rows = slice(cc * N_KEYS, (cc + 1) * N_KEYS)
                gsum = None
                for hh in range(PEER_HEADS):
                    sel = (s1_ref[hh, lb] + s0rows[cc][hh][:, ls]) >= taus[hh][:, ls]
                    term = jnp.where(sel, e1_ref[hh, lb] * cfrows[cc][hh][:, ls], 0.0)
                    gsum = term if gsum is None else gsum + term
                so = st_old[lb, rows, :]
                act = 0.5 * so * (1.0 + lax.erf(so * np.float32(np.sqrt(0.5))))
                w_new[lb, rows, :] = (gsum * act).astype(BF16)

            for lb in lbs:
                for cc in range(ncc):
                    gate_tiles.append(functools.partial(gate, lb, cc))
        for i in range(max(len(mxu_pieces), len(gate_tiles))):
            if i < len(mxu_pieces):
                mxu_pieces[i]()
            if i < len(gate_tiles):
                gate_tiles[i]()

    @pl.when(j % 2 == 0)
    def _():
        stages(st0_ref, st1_ref, w1_ref, w0_ref)

    @pl.when(j % 2 == 1)
    def _():
        stages(st1_ref, st0_ref, w0_ref, w1_ref)

    @pl.when(j == ne + 1)
    def _():
        out = acc_ref[...].T
        y_ref[...] = _layer_norm(DN_ALPHA * h_ref[...] + out, g_ref[...], b_ref[...])


def _peer(ht, u_bf16, vt_bf16, s0t, s1t, aux, h, ln_g, ln_b, tm=512, te=512):
    m = h.shape[0]
    tm = min(tm, m)
    ne = N_EXPERTS // te
    tok3 = lambda i, j: (0, 0, i)
    return pl.pallas_call(
        functools.partial(_peer_body, te, ne),
        grid=(m // tm, ne + 2),
        in_specs=[pl.BlockSpec((D_MODEL, tm), lambda i, j: (0, i)),
                  pl.BlockSpec((te, D_MODEL), lambda i, j: (jnp.minimum(j, ne - 1), 0)),
                  pl.BlockSpec((D_MODEL, te), lambda i, j: (0, jnp.clip(j - 2, 0, ne - 1))),
                  pl.BlockSpec((PEER_HEADS, N_KEYS, tm), tok3),
                  pl.BlockSpec((PEER_HEADS, tm // 128, N_KEYS, 128), lambda i, j: (0, i, 0, 0)),
                  pl.BlockSpec((PEER_HEADS, 8, tm), tok3),
                  pl.BlockSpec((tm, D_MODEL), lambda i, j: (i, 0)),
                  pl.BlockSpec((1, D_MODEL), lambda i, j: (0, 0)),
                  pl.BlockSpec((1, D_MODEL), lambda i, j: (0, 0))],
        out_specs=pl.BlockSpec((tm, D_MODEL), lambda i, j: (i, 0)),
        out_shape=jax.ShapeDtypeStruct((m, D_MODEL), F32),
        scratch_shapes=[pltpu.VMEM((D_MODEL, tm), F32),
                        pltpu.VMEM((PEER_HEADS, tm // 128, N_KEYS, 128), F32),
                        pltpu.VMEM((PEER_HEADS, N_KEYS, tm), F32),
                        pltpu.VMEM((tm // 128, te, 128), F32),
                        pltpu.VMEM((tm // 128, te, 128), F32),
                        pltpu.VMEM((tm // 128, te, 128), BF16),
                        pltpu.VMEM((tm // 128, te, 128), BF16)],
        compiler_params=_params(("parallel", "arbitrary")),
        name="peer_dense",
    )(ht, u_bf16, vt_bf16, s0t, s1t, aux, h, ln_g, ln_b)


def _pad_rows(w, lo, total):
    return jnp.zeros((total, w.shape[1]), w.dtype).at[lo:lo + w.shape[0]].set(w)


def _layer(x2d, nseq, seq_len, chunk, prev_shift, state0, cache, wts):
    proj = _inproj(x2d, wts["w_in"])
    o_a, s_last = _rwkv(proj, nseq, seq_len // chunk, chunk, prev_shift, state0, wts)
    if cache is None:
        o_b = _attn_prompt(proj, wts["sinks"])
    else:
        o_b = _attn_sample(proj, cache[0], cache[1], wts["sinks"], seq_len)
    h, ht = _merge(proj, o_a, o_b, x2d, wts["w_out"], wts["ln1_g"], wts["ln1_b"])
    s0t, s1t, aux = _route(h, wts["peer_wq"], wts["peer_keys"])
    y = _peer(ht, wts["peer_u"], wts["peer_vt"], s0t, s1t, aux, h, wts["ln2_g"], wts["ln2_b"])
    return y, proj, s_last


def kernel(x_prompt, x_sample, state_shift, state_wkv, cache_k, cache_v, w_in, rw_mu, rw_w0, rw_w_up, rw_a0,
           rw_a_up, rw_g_up, rw_k_k, rw_k_a, rw_r_k, rw_gn_g, rw_gn_b, att_sinks, w_out, ln1_g, ln1_b,
           peer_wq, peer_keys, peer_u, peer_v, ln2_g, ln2_b):
    depth = w_in.shape[0]
    assert depth == 1
    l = 0
    bp, tp, _ = x_prompt.shape
    bs, ts, _ = x_sample.shape
    assert bp == 1
    win = cache_k.shape[2]
    assert win == WINDOW and tp % WINDOW == 0 and ts <= 8

    w = w_in[l]
    q0 = SHIFT_W
    k0 = q0 + D_MODEL
    g0 = k0 + 2 * KV_W
    w_perm = jnp.concatenate(
        [w[:, :RKV_W], w[:, q0:q0 + D_MODEL], w[:, g0:g0 + 2 * D_MODEL], w[:, RKV_W:SHIFT_W],
         jnp.zeros((D_MODEL, LORA_PAD - LORA_W), w.dtype), w[:, k0:k0 + 2 * KV_W]], axis=1).astype(BF16)
    mu = rw_mu[l]
    row = lambda v: v.reshape(1, -1)
    wts = dict(
        w_in=w_perm,
        mu_rkv=row(mu[:RKV_W]),
        mu_lora=row(jnp.pad(mu[RKV_W:], (0, LORA_PAD - LORA_W))),
        w0=row(rw_w0[l]), a0=row(rw_a0[l]), k_k=row(rw_k_k[l]), k_a=row(rw_k_a[l]), r_k=row(rw_r_k[l]),
        gn_g=row(rw_gn_g[l]), gn_b=row(rw_gn_b[l]),
        w_up=_pad_rows(rw_w_up[l], 0, LORA_PAD).astype(BF16),
        a_up=_pad_rows(rw_a_up[l], 96, LORA_PAD).astype(BF16),
        g_up=_pad_rows(rw_g_up[l], 192, LORA_PAD).astype(BF16),
        sinks=jnp.repeat(att_sinks[l], 128).reshape(1, 32 * 128),
        w_out=w_out[l].astype(BF16),
        ln1_g=row(ln1_g[l]), ln1_b=row(ln1_b[l]), ln2_g=row(ln2_g[l]), ln2_b=row(ln2_b[l]),
        peer_wq=peer_wq[l].astype(BF16),
        peer_keys=peer_keys[l].astype(BF16),
        peer_u=peer_u[l].astype(BF16),
        peer_vt=peer_v[l].T.astype(BF16),
    )

    def unshift(p_last):
        return jnp.concatenate([p_last[..., :RKV_W], p_last[..., COL_LORA:COL_LORA + LORA_W]], axis=-1)

    chunk_p = 64
    y_p, proj_p, s_p = _layer(x_prompt[0], 1, tp, chunk_p,
                              jnp.zeros((1, SHIFT_W), F32), jnp.zeros((1, D_MODEL, HEAD_DIM), F32), None, wts)
    keep = min(WINDOW, tp)
    kv_p = proj_p[tp - keep:, COL_KV:]
    y_prompt = y_p[None]
    new_shift_prompt = unshift(proj_p[tp - 1])[None, None]
    new_wkv_prompt = s_p.reshape(1, 1, D_MODEL // HEAD_DIM, HEAD_DIM, HEAD_DIM)
    new_k_prompt = kv_p[:, :KV_W].reshape(1, 1, keep, KV_W // HEAD_DIM, HEAD_DIM)
    new_v_prompt = kv_p[:, KV_W:].reshape(1, 1, keep, KV_W // HEAD_DIM, HEAD_DIM)

    ck = cache_k[l].reshape(bs, win, KV_W)
    cv = cache_v[l].reshape(bs, win, KV_W)
    y_s, proj_s, s_s = _layer(x_sample.reshape(bs * ts, D_MODEL), bs, ts, ts,
                              state_shift[l], state_wkv[l].reshape(bs, D_MODEL, HEAD_DIM), (ck, cv), wts)
    proj_s3 = proj_s.reshape(bs, ts, PROJ_W)
    y_sample = y_s.reshape(bs, ts, D_MODEL)
    new_shift_sample = unshift(proj_s3[:, ts - 1])[None]
    new_wkv_sample = s_s.reshape(1, bs, D_MODEL // HEAD_DIM, HEAD_DIM, HEAD_DIM)
    kv_s = proj_s3[:, :, COL_KV:]
    new_k_sample = jnp.concatenate([ck, kv_s[:, :, :KV_W]], axis=1)[:, ts:].reshape(
        1, bs, win, KV_W // HEAD_DIM, HEAD_DIM)
    new_v_sample = jnp.concatenate([cv, kv_s[:, :, KV_W:]], axis=1)[:, ts:].reshape(
        1, bs, win, KV_W // HEAD_DIM, HEAD_DIM)

    return (y_prompt, y_sample, new_shift_prompt, new_wkv_prompt, new_k_prompt, new_v_prompt,
            new_shift_sample, new_wkv_sample, new_k_sample, new_v_sample)
```

```python
import functools

import numpy as np
import jax
import jax.numpy as jnp
from jax import lax
from jax.experimental import pallas as pl
from jax.experimental.pallas import tpu as pltpu

F32, BF16 = jnp.float32, jnp.bfloat16

D_MODEL = 2048
HEAD_DIM = 64
HEADS_PER_GROUP = 2
GROUP_W = HEADS_PER_GROUP * HEAD_DIM
N_GROUPS = D_MODEL // GROUP_W
RKV_W = 3 * D_MODEL
LORA_W = 96 + 96 + 256
LORA_PAD = 512
SHIFT_W = RKV_W + LORA_W
KV_W = 256
GN_EPS = 64e-5
LN_EPS = 1e-5
WINDOW = 128
ATT_SCALE = HEAD_DIM ** -0.5
N_KEYS = 128
N_EXPERTS = N_KEYS * N_KEYS
PEER_HEADS = 8
PEER_TOPK = 16
DN_ALPHA = 2.0 ** 0.25

COL_Q = RKV_W
COL_GATE = COL_Q + D_MODEL
COL_LORA = COL_GATE + 2 * D_MODEL
COL_KV = COL_LORA + LORA_PAD
PROJ_W = COL_KV + 2 * KV_W

VMEM_LIMIT = 56 * 1024 * 1024

_NN = (((1,), (0,)), ((), ()))
_NT = (((1,), (1,)), ((), ()))
_TN = (((0,), (0,)), ((), ()))


def _bdot(a, b, dn=_NN):
    return lax.dot_general(a, b, dn, preferred_element_type=F32)


def _dot(a, b, dn=_NN):
    return _bdot(a.astype(BF16), b.astype(BF16), dn)


def _split(x, n):
    parts, r = [], x
    for i in range(n):
        p = r.astype(BF16)
        parts.append(p)
        if i + 1 < n:
            r = r - p.astype(F32)
    return parts


def _dot3(a, b, dn=_NN):
    a1, a2 = _split(a, 2)
    b1, b2 = _split(b, 2)
    return (_bdot(a1, b2, dn) + _bdot(a2, b1, dn)) + _bdot(a1, b1, dn)


def _dot_xl(a_bf16, b, dn=_NN, passes=3):
    out = None
    for part in reversed(_split(b, passes)):
        t = _bdot(a_bf16, part, dn)
        out = t if out is None else out + t
    return out


def _dot_xr(a, b_bf16, dn=_NN, passes=3):
    out = None
    for part in reversed(_split(a, passes)):
        t = _bdot(part, b_bf16, dn)
        out = t if out is None else out + t
    return out


def _mm(passes):
    return _dot if passes == 1 else _dot3


def _sigmoid(x):
    return 1.0 / (1.0 + jnp.exp(-x))


def _iota(shape, dim):
    return lax.broadcasted_iota(jnp.int32, shape, dim)


def _head_of(idx):
    return jnp.right_shift(idx, 6)


def _ones_where(mask):
    return jnp.where(mask, 1.0, 0.0).astype(BF16)


def _params(sem):
    return pltpu.CompilerParams(dimension_semantics=sem, vmem_limit_bytes=VMEM_LIMIT)


def _inproj_body(x_ref, w_ref, o_ref, xb_ref):
    @pl.when(pl.program_id(1) == 0)
    def _():
        xb_ref[...] = x_ref[...].astype(BF16)

    o_ref[...] = _bdot(xb_ref[...], w_ref[...])


def _inproj(x2d, w_bf16, tn=1024):
    m = x2d.shape[0]
    tm = min(m, 1024)
    n = w_bf16.shape[1]
    return pl.pallas_call(
        _inproj_body,
        grid=(m // tm, n // tn),
        in_specs=[pl.BlockSpec((tm, D_MODEL), lambda i, j: (i, 0)),
                  pl.BlockSpec((D_MODEL, tn), lambda i, j: (0, j))],
        out_specs=pl.BlockSpec((tm, tn), lambda i, j: (i, j)),
        out_shape=jax.ShapeDtypeStruct((m, n), F32),
        scratch_shapes=[pltpu.VMEM((tm, D_MODEL), BF16)],
        compiler_params=_params(("parallel", "arbitrary")),
        name="inproj",
    )(x2d, w_bf16)


RWKV_PASSES = dict(seg=2, cum=3, lblk=1, inv=1, apply=1, state=1, gn=2)
ROWS = 64


def _rwkv_body(C, nb, ncs, nsteps, gpb, ps,
               r_ref, k_ref, v_ref, lo_ref, pr_ref, pk_ref, pv_ref, plo_ref, s0_ref,
               mur_ref, muk_ref, muv_ref, mulo_ref, w0_ref, a0_ref, kk_ref, ka_ref, rk_ref, gng_ref, gnb_ref,
               wup_ref, aup_ref, gup_ref,
               o_ref, sout_ref,
               sbd_ref, sh_ref, shl_ref):
    assert HEADS_PER_GROUP == 2 and HEADS_PER_GROUP * ROWS == GROUP_W and (ncs == 1 or nb == 1)
    c = pl.program_id(2)
    w = gpb * GROUP_W
    srows = ncs * ROWS
    log2c = C.bit_length() - 1
    st = HEADS_PER_GROUP * ROWS
    sq = (st, st)
    ri, ci = _iota(sq, 0), _iota(sq, 1)
    same_head = _head_of(ri) == _head_of(ci)
    tr, tc = ri & (ROWS - 1), ci & (ROWS - 1)
    blk = same_head & (jnp.right_shift(tr, log2c) == jnp.right_shift(tc, log2c))
    strict = blk & (tr > tc)
    incl = blk & (tr >= tc)
    eye = jnp.where(ri == ci, 1.0, 0.0)
    block_ones = _ones_where(same_head)
    r64, c64 = _iota((ROWS, ROWS), 0), _iota((ROWS, ROWS), 1)
    tri_seq = _ones_where((jnp.right_shift(r64, log2c) == jnp.right_shift(c64, log2c)) & (r64 >= c64))
    lane_head = _head_of(_iota((ROWS, GROUP_W), 1))
    rep = _ones_where((_iota((HEAD_DIM, GROUP_W), 1) & (HEAD_DIM - 1)) == _iota((HEAD_DIM, GROUP_W), 0))
    rep_t = _ones_where((_iota((GROUP_W, HEAD_DIM), 0) & (HEAD_DIM - 1)) == _iota((GROUP_W, HEAD_DIM), 1))
    mm_l, mm_i, mm_a, mm_s = _mm(ps["lblk"]), _mm(ps["inv"]), _mm(ps["apply"]), _mm(ps["state"])

    @pl.when(c == 0)
    def _init():
        if nb == 1:
            sh_ref[7:8, 0:w] = pr_ref[0:1, :]
            sh_ref[7:8, w:2 * w] = pk_ref[0:1, :]
            sh_ref[7:8, 2 * w:3 * w] = pv_ref[0:1, :]
            shl_ref[7:8, :] = plo_ref[0:1, :]
        for b in range(nb):
            for gi in range(gpb):
                s0 = s0_ref[b, gi * GROUP_W:(gi + 1) * GROUP_W, :]
                sbd_ref[b, gi] = jnp.where(same_head, _dot_xr(s0, rep), 0.0)

    def lerp(x, ref, lo, hi, prev_ref, mu):
        ref[pl.ds(8, srows), lo:hi] = x
        xs = ref[pl.ds(7, srows), lo:hi]
        ref[7:8, lo:hi] = x[srows - 1:srows, :]
        if nb > 1:
            first = (_iota(x.shape, 0) & (C - 1)) == 0
            xs = jnp.where(first, prev_ref[...], xs)
        return x + (xs - x) * mu

    r = lerp(r_ref[...], sh_ref, 0, w, pr_ref, mur_ref[...])
    k = lerp(k_ref[...], sh_ref, w, 2 * w, pk_ref, muk_ref[...])
    v = lerp(v_ref[...], sh_ref, 2 * w, 3 * w, pv_ref, muv_ref[...])
    lo = lerp(lo_ref[...], shl_ref, 0, LORA_PAD, plo_ref, mulo_ref[...])

    wl = w0_ref[...] + _dot(jnp.tanh(lo), wup_ref[...])
    w_log = -(jnp.maximum(-wl, 0.0) + jnp.log1p(jnp.exp(-jnp.abs(wl)))) - 0.5
    logw_all = -jnp.exp(w_log)
    a_all = _sigmoid(a0_ref[...] + _dot(lo, aup_ref[...]))
    gate_all = _dot(_sigmoid(lo), gup_ref[...])
    kk_all = k * kk_ref[...]
    k2_all = k * (1.0 + (a_all - 1.0) * ka_ref[...])

    def stack(x):
        return jnp.concatenate([jnp.where(lane_head == j, x, 0.0) for j in range(HEADS_PER_GROUP)], axis=0)

    def fold(x):
        return x[0:ROWS] + x[ROWS:2 * ROWS]

    gs = range(ncs * gpb)
    sls = [slice(gi * GROUP_W, (gi + 1) * GROUP_W) for _ in range(ncs) for gi in range(gpb)]
    rws = [slice(ck * ROWS, (ck + 1) * ROWS) for ck in range(ncs) for _ in range(gpb)]
    rg = [r[rw, sl] for rw, sl in zip(rws, sls)]
    vg = [v[rw, sl] for rw, sl in zip(rws, sls)]
    k2 = [k2_all[rw, sl] for rw, sl in zip(rws, sls)]
    kk = [kk_all[rw, sl] for rw, sl in zip(rws, sls)]
    ss = [_dot_xr(x * x, block_ones, passes=ps["seg"]) for x in kk]
    kkn = [x / jnp.maximum(jnp.sqrt(s2), 1e-12) for x, s2 in zip(kk, ss)]
    logw = [logw_all[rw, sl] for rw, sl in zip(rws, sls)]
    cum = [_dot_xl(tri_seq, x, passes=ps["cum"]) for x in logw]
    gam = [jnp.exp(x) for x in cum]
    ginv = [jnp.exp(-x) for x in cum]
    a4 = [-(jnp.exp(cm - lw) * kn) for cm, lw, kn in zip(cum, logw, kkn)]
    b4 = [kn * a_all[rw, sl] * gi_ for kn, rw, sl, gi_ in zip(kkn, rws, sls, ginv)]
    k4 = [x * gi_ for x, gi_ in zip(k2, ginv)]
    r4 = [gm * x for gm, x in zip(gam, rg)]

    single = all(ps[c_] == 1 for c_ in ("lblk", "inv", "apply", "state"))
    nar = (lambda x: x.astype(BF16)) if single else (lambda x: x)
    a_st = [nar(stack(x)) for x in a4]
    r_st = [nar(stack(x)) for x in r4]
    v_st = [nar(stack(x)) for x in vg]
    prod = [mm_l(jnp.concatenate([x, y_], axis=0), jnp.concatenate([z, q_], axis=0), _NT)
            for x, y_, z, q_ in zip(a_st, r_st, b4, k4)]
    low = _iota(sq, 1) < HEAD_DIM

    def diag_blocks(x, mask):
        xr = pltpu.roll(x, HEAD_DIM, axis=1)
        return jnp.where(mask, jnp.where(low, x, xr), 0.0), jnp.where(mask, jnp.where(low, xr, x), 0.0)

    l_bk = [diag_blocks(x[:st], strict) for x in prod]
    m_bk = [diag_blocks(x[st:], incl) for x in prod]
    l_b, l_k = [x[0] for x in l_bk], [nar(x[1]) for x in l_bk]
    m_b, m_k = [nar(x[0]) for x in m_bk], [nar(x[1]) for x in m_bk]
    t = [nar(eye + x) for x in l_b]
    l_bn = [nar(x) for x in l_b]
    pw = [nar(mm_i(x, x)) for x in l_bn]
    cov = 2
    while cov < C:
        if 2 * cov >= C:
            t = [nar(x + mm_i(p_, x)) for x, p_ in zip(t, pw)]
        else:
            both = [mm_i(p_, jnp.concatenate([p_, x], axis=1)) for x, p_ in zip(t, pw)]
            pw = [nar(x[:, :st]) for x in both]
            t = [nar(x + y_[:, st:]) for x, y_ in zip(t, both)]
        cov *= 2
    lm_v = [mm_a(jnp.concatenate([x, y_], axis=0), z) for x, y_, z in zip(l_k, m_k, v_st)]
    mkv4 = [fold(x[st:]) for x in lm_v]
    t_av = [mm_a(x, jnp.concatenate([y_, nar(z[:st])], axis=1)) for x, y_, z in zip(t, a_st, lm_v)]
    wm4 = [fold(x[:, :st]) for x in t_av]
    uk4 = [fold(x[:, st:]) for x in t_av]

    us = [[None] * nb for _ in gs]
    ys = [[None] * nb for _ in gs]
    for ck in range(ncs):
        cgs = range(ck * gpb, (ck + 1) * gpb)
        for b in range(nb):
            rows = slice(b * C, (b + 1) * C)
            s_old = {gi: sbd_ref[b, gi - ck * gpb] for gi in cgs}
            uy = {gi: mm_s(jnp.concatenate([wm4[gi][rows], r4[gi][rows]], axis=0), s_old[gi], _NT) for gi in cgs}
            for gi in cgs:
                us[gi][b] = uy[gi][:C] + uk4[gi][rows]
                ys[gi][b] = uy[gi][C:]
            ds = {gi: mm_s(jnp.concatenate([us[gi][b], vg[gi][rows]], axis=0),
                           jnp.concatenate([b4[gi][rows], k4[gi][rows]], axis=0), _TN) for gi in cgs}
            for gi in cgs:
                s_new = (s_old[gi] + jnp.where(same_head, ds[gi], 0.0)) * gam[gi][(b + 1) * C - 1:(b + 1) * C, :]
                sbd_ref[b, gi - ck * gpb] = s_new
                if ck == ncs - 1:
                    @pl.when(c == nsteps - 1)
                    def _fin():
                        sout_ref[b, sls[gi], :] = _dot_xr(s_new, rep_t)

    u = [x[0] if nb == 1 else jnp.concatenate(x, axis=0) for x in us]
    ysum = [x[0] if nb == 1 else jnp.concatenate(x, axis=0) for x in ys]
    yb = [fold(mm_s(x, stack(y_))) for x, y_ in zip(m_b, u)]
    y = [x + y_ + z for x, y_, z in zip(ysum, mkv4, yb)]

    ym = [_dot_xr(x, block_ones, passes=ps["gn"]) * (1.0 / HEAD_DIM) for x in y]
    yc = [x - y_ for x, y_ in zip(y, ym)]
    yv = [_dot_xr(x * x, block_ones, passes=ps["gn"]) * (1.0 / HEAD_DIM) for x in yc]
    bonus = [_dot_xr(rg[gi] * k2[gi] * rk_ref[:, sls[gi]], block_ones, passes=ps["gn"]) * vg[gi] for gi in gs]
    for gi in gs:
        sl, rw = sls[gi], rws[gi]
        yn = yc[gi] * lax.rsqrt(yv[gi] + GN_EPS) * gng_ref[:, sl] + gnb_ref[:, sl]
        o_ref[rw, sl] = (yn + bonus[gi]) * gate_all[rw, sl]


def _rwkv(proj, nseq, nchunk, C, prev_shift, state0, prm, gpb=None, ncs=None):
    nb = ROWS // C
    if gpb is None:
        gpb = N_GROUPS if nb == 1 else N_GROUPS // 2
    if ncs is None:
        ncs = 2 if nb == 1 and nchunk % 2 == 0 else 1
    assert nb * C == ROWS and nseq % nb == 0 and (nb == 1 or nchunk == 1) and nchunk % ncs == 0
    rows = nseq * nchunk * C
    nsteps = nchunk // ncs
    srows = ncs * ROWS
    w = gpb * GROUP_W
    sec = D_MODEL // w
    prows = srows if nb == 1 else C
    prev = jnp.zeros((nseq, prows, SHIFT_W), F32).at[:, 0].set(prev_shift).reshape(nseq * prows, SHIFT_W)
    prev_rkv = prev[:, :RKV_W]
    prev_lora = jnp.pad(prev[:, RKV_W:], ((0, 0), (0, LORA_PAD - LORA_W)))

    def tok(off):
        return pl.BlockSpec((srows, w), lambda s, g, c: (s * nsteps + c, off + g))

    def prv(off):
        return pl.BlockSpec((srows, w), lambda s, g, c: (s, off + g))

    def vec(off=0):
        return pl.BlockSpec((1, w), lambda s, g, c: (0, off + g))

    up = pl.BlockSpec((LORA_PAD, w), lambda s, g, c: (0, g))
    in_specs = [
        tok(0), tok(sec), tok(2 * sec),
        pl.BlockSpec((srows, LORA_PAD), lambda s, g, c: (s * nsteps + c, COL_LORA // LORA_PAD)),
        prv(0), prv(sec), prv(2 * sec),
        pl.BlockSpec((srows, LORA_PAD), lambda s, g, c: (s, 0)),
        pl.BlockSpec((nb, w, HEAD_DIM), lambda s, g, c: (s, g, 0)),
        vec(0), vec(sec), vec(2 * sec),
        pl.BlockSpec((1, LORA_PAD), lambda s, g, c: (0, 0)),
        vec(), vec(), vec(), vec(), vec(), vec(), vec(),
        up, up, up,
    ]
    out_specs = [pl.BlockSpec((srows, w), lambda s, g, c: (s * nsteps + c, g)),
                 pl.BlockSpec((nb, w, HEAD_DIM), lambda s, g, c: (s, g, 0))]
    return pl.pallas_call(
        functools.partial(_rwkv_body, C, nb, ncs, nsteps, gpb, RWKV_PASSES),
        grid=(nseq // nb, N_GROUPS // gpb, nsteps),
        in_specs=in_specs,
        out_specs=out_specs,
        out_shape=[jax.ShapeDtypeStruct((rows, D_MODEL), F32),
                   jax.ShapeDtypeStruct((nseq, D_MODEL, HEAD_DIM), F32)],
        scratch_shapes=[pltpu.VMEM((nb, gpb, GROUP_W, GROUP_W), F32),
                        pltpu.VMEM((srows + 8, 3 * w), F32),
                        pltpu.VMEM((srows + 8, LORA_PAD), F32)],
        compiler_params=_params(("arbitrary", "arbitrary", "arbitrary")),
        name="rwkv",
    )(proj, proj, proj, proj, prev_rkv, prev_rkv, prev_rkv, prev_lora, state0,
      prm["mu_rkv"], prm["mu_rkv"], prm["mu_rkv"], prm["mu_lora"],
      prm["w0"], prm["a0"], prm["k_k"], prm["k_a"], prm["r_k"], prm["gn_g"], prm["gn_b"],
      prm["w_up"], prm["a_up"], prm["g_up"])


def _attn_rows(q, kcat, vcat, sink_ref, first_key, store, npr):
    m = q.shape[0]
    kt_all = kcat.T
    qi = _iota((npr * m, 2 * WINDOW), 0) & (m - 1)
    ki = _iota((npr * m, 2 * WINDOW), 1)
    allowed = (ki >= qi + 1) & (ki <= qi + WINDOW) & (ki >= first_key)
    lane = _iota((2 * WINDOW, 128), 1)
    zk = jnp.zeros((HEAD_DIM, 2 * WINDOW), F32)
    nkv = KV_W // HEAD_DIM
    k2s, v2s = [], []
    for kv in range(nkv):
        kt = kt_all[kv * HEAD_DIM:(kv + 1) * HEAD_DIM, :]
        k2s.append(jnp.concatenate([jnp.concatenate([kt, zk], axis=1),
                                    jnp.concatenate([zk, kt], axis=1)], axis=0).astype(BF16))
        slab = vcat[:, (kv // 2) * 128:(kv // 2 + 1) * 128]
        rolled = pltpu.roll(slab, HEAD_DIM, axis=1)
        lo_src, hi_src = (slab, rolled) if kv % 2 == 0 else (rolled, slab)
        v2s.append(jnp.concatenate([jnp.where(lane < HEAD_DIM, lo_src, 0.0),
                                    jnp.where(lane >= HEAD_DIM, hi_src, 0.0)], axis=0).astype(BF16))
    items = [(kv, p0) for kv in range(nkv) for p0 in range(0, 4, npr)]
    cols = [[kv * 512 + (p0 + i) * 128 for i in range(npr)] for kv, p0 in items]
    scores = [_bdot((jnp.concatenate([q[:, col:col + 128] for col in cs], axis=0) * ATT_SCALE).astype(BF16),
                    k2s[kv]) for (kv, _), cs in zip(items, cols)]
    probs = []
    for (kv, p0), s in zip(items, scores):
        ps = []
        for hb in range(2):
            sink = jnp.concatenate(
                [jnp.broadcast_to(sink_ref[:, (kv * 8 + (p0 + i) * 2 + hb) * 128:
                                           (kv * 8 + (p0 + i) * 2 + hb) * 128 + 1], (m, 1))
                 for i in range(npr)], axis=0)
            sh = jnp.where(allowed, s[:, hb * 256:(hb + 1) * 256], -jnp.inf)
            mx = jnp.maximum(jnp.max(sh, axis=1, keepdims=True), sink)
            e = jnp.exp(sh - mx)
            den = jnp.sum(e, axis=1, keepdims=True) + jnp.exp(sink - mx)
            ps.append(e * (1.0 / den))
        probs.append(jnp.concatenate(ps, axis=1).astype(BF16))
    outs = [_bdot(p, v2s[kv]) for (kv, _), p in zip(items, probs)]
    for cs, o in zip(cols, outs):
        for i, col in enumerate(cs):
            store(col, o[i * m:(i + 1) * m])


def _attn_prompt_body(q_ref, kvc_ref, kvp_ref, sink_ref, o_ref):
    n = pl.program_id(0)
    kvc = kvc_ref[...]
    kvp = kvp_ref[...]
    kcat = jnp.concatenate([kvp[:, :KV_W], kvc[:, :KV_W]], axis=0)
    vcat = jnp.concatenate([kvp[:, KV_W:], kvc[:, KV_W:]], axis=0)

    def store(col, val):
        o_ref[:, col:col + 128] = val

    _attn_rows(q_ref[...], kcat, vcat, sink_ref, jnp.where(n > 0, 0, WINDOW), store, npr=1)


def _attn_prompt(proj, sinks_e):
    t = proj.shape[0]
    nb = t // WINDOW
    qb, kvb = COL_Q // D_MODEL, COL_KV // (2 * KV_W)
    return pl.pallas_call(
        _attn_prompt_body,
        grid=(nb,),
        in_specs=[pl.BlockSpec((WINDOW, D_MODEL), lambda n: (n, qb)),
                  pl.BlockSpec((WINDOW, 2 * KV_W), lambda n: (n, kvb)),
                  pl.BlockSpec((WINDOW, 2 * KV_W), lambda n: (jnp.maximum(n - 1, 0), kvb)),
                  pl.BlockSpec((1, 32 * 128), lambda n: (0, 0))],
        out_specs=pl.BlockSpec((WINDOW, D_MODEL), lambda n: (n, 0)),
        out_shape=jax.ShapeDtypeStruct((t, D_MODEL), F32),
        compiler_params=_params(("parallel",)),
        name="attn_prompt",
    )(proj, proj, proj, sinks_e)


def _attn_sample_body(L, bt, q_ref, kvn_ref, ck_ref, cv_ref, sink_ref, o_ref):
    pad = jnp.zeros((WINDOW - L, KV_W), F32)

    def one(b, carry):
        rows = pl.ds(pl.multiple_of(b * L, L), L)
        kvn = kvn_ref[rows, :]
        kcat = jnp.concatenate([ck_ref[b], kvn[:, :KV_W], pad], axis=0)
        vcat = jnp.concatenate([cv_ref[b], kvn[:, KV_W:], pad], axis=0)

        def store(col, val):
            o_ref[rows, col:col + 128] = val

        _attn_rows(q_ref[rows, :], kcat, vcat, sink_ref, 0, store, npr=4)
        return carry

    lax.fori_loop(0, bt, one, 0, unroll=2)


def _attn_sample(proj, ck, cv, sinks_e, L, bt=16):
    rows = proj.shape[0]
    nb = rows // L
    bt = min(bt, nb)
    qb, kvb = COL_Q // D_MODEL, COL_KV // (2 * KV_W)
    return pl.pallas_call(
        functools.partial(_attn_sample_body, L, bt),
        grid=(nb // bt,),
        in_specs=[pl.BlockSpec((bt * L, D_MODEL), lambda i: (i, qb)),
                  pl.BlockSpec((bt * L, 2 * KV_W), lambda i: (i, kvb)),
                  pl.BlockSpec((bt, WINDOW, KV_W), lambda i: (i, 0, 0)),
                  pl.BlockSpec((bt, WINDOW, KV_W), lambda i: (i, 0, 0)),
                  pl.BlockSpec((1, 32 * 128), lambda i: (0, 0))],
        out_specs=pl.BlockSpec((bt * L, D_MODEL), lambda i: (i, 0)),
        out_shape=jax.ShapeDtypeStruct((rows, D_MODEL), F32),
        compiler_params=_params(("parallel",)),
        name="attn_sample",
    )(proj, proj, ck, cv, sinks_e)


def _layer_norm(x, g, b):
    mu = jnp.mean(x, axis=-1, keepdims=True)
    xc = x - mu
    var = jnp.mean(xc * xc, axis=-1, keepdims=True)
    return xc * lax.rsqrt(var + LN_EPS) * g + b


def _merge_body(ga_ref, gb_ref, oa_ref, ob_ref, x_ref, wout_ref, g_ref, b_ref, h_ref, ht_ref):
    mixed = _sigmoid(ga_ref[...]) * oa_ref[...] + _sigmoid(gb_ref[...]) * ob_ref[...]
    y = _bdot(mixed.astype(BF16), wout_ref[...])
    h = _layer_norm(DN_ALPHA * x_ref[...] + y, g_ref[...], b_ref[...])
    h_ref[...] = h
    ht_ref[...] = h.T.astype(BF16)


def _merge(proj, o_a, o_b, x2d, wout_bf16, ln_g, ln_b, tm=256):
    m = x2d.shape[0]
    tm = min(tm, m)
    gab = COL_GATE // D_MODEL
    row = lambda i: (i, 0)
    return pl.pallas_call(
        _merge_body,
        grid=(m // tm,),
        in_specs=[pl.BlockSpec((tm, D_MODEL), lambda i: (i, gab)),
                  pl.BlockSpec((tm, D_MODEL), lambda i: (i, gab + 1)),
                  pl.BlockSpec((tm, D_MODEL), row),
                  pl.BlockSpec((tm, D_MODEL), row),
                  pl.BlockSpec((tm, D_MODEL), row),
                  pl.BlockSpec((D_MODEL, D_MODEL), lambda i: (0, 0)),
                  pl.BlockSpec((1, D_MODEL), lambda i: (0, 0)),
                  pl.BlockSpec((1, D_MODEL), lambda i: (0, 0))],
        out_specs=[pl.BlockSpec((tm, D_MODEL), row),
                   pl.BlockSpec((D_MODEL, tm), lambda i: (0, i))],
        out_shape=[jax.ShapeDtypeStruct((m, D_MODEL), F32),
                   jax.ShapeDtypeStruct((D_MODEL, m), BF16)],
        compiler_params=_params(("parallel",)),
        name="merge",
    )(proj, proj, o_a, o_b, x2d, wout_bf16, ln_g, ln_b)


def _top_rows(x, count):
    return _top_rows_multi([x], count)[0]


def _top_rows_multi(xs, count):
    rows = _iota(xs[0].shape, 0)
    big = xs[0].shape[0]
    outs = [[] for _ in xs]
    for _ in range(count):
        mxs = [jnp.max(x, axis=0, keepdims=True) for x in xs]
        for o, mx in zip(outs, mxs):
            o.append(mx)
        firsts = [jnp.min(jnp.where(x == mx, rows, big), axis=0, keepdims=True) for x, mx in zip(xs, mxs)]
        xs = [jnp.where(rows == f, -jnp.inf, x) for x, f in zip(xs, firsts)]
    return outs


def _route_body(tm, h_ref, wq_ref, keys_ref, s0_ref, s1_ref, aux_ref, q_s):
    q = _bdot(h_ref[...].astype(BF16), wq_ref[...])
    for hc in range(2 * PEER_HEADS):
        q_s[hc] = q[:, hc * N_KEYS:(hc + 1) * N_KEYS]
    pairs = [(i, j) for i in range(PEER_TOPK) for j in range(PEER_TOPK) if (i + 1) * (j + 1) <= PEER_TOPK]
    npad = -len(pairs) % 8
    neg = jnp.full((1, tm), -jnp.inf, F32)
    zero = jnp.zeros((1, tm), F32)

    def head(hh, carry):
        s0 = _bdot(keys_ref[hh, 0], q_s[2 * hh].astype(BF16), _NT)
        s1 = _bdot(keys_ref[hh, 1], q_s[2 * hh + 1].astype(BF16), _NT)
        s0_ref[hh] = s0
        for lb in range(tm // 128):
            s1_ref[hh, lb] = s1[:, lb * 128:(lb + 1) * 128]
        top0, top1 = _top_rows_multi([s0, s1], PEER_TOPK)
        cand = jnp.concatenate([top0[i] + top1[j] for i, j in pairs] + [neg] * npad, axis=0)
        best = _top_rows(cand, PEER_TOPK)
        mx = best[0]
        z = zero
        for bsum in best:
            z = z + jnp.exp(bsum - mx)
        aux_ref[hh] = jnp.concatenate([best[-1], top0[0], top1[0], 1.0 / z, zero, zero, zero, zero], axis=0)
        return carry

    lax.fori_loop(0, PEER_HEADS, head, 0)


def _route(h, wq_bf16, keys_bf16, tm=256):
    m = h.shape[0]
    tm = min(tm, m)
    tok3 = lambda i: (0, 0, i)
    return pl.pallas_call(
        functools.partial(_route_body, tm),
        grid=(m // tm,),
        in_specs=[pl.BlockSpec((tm, D_MODEL), lambda i: (i, 0)),
                  pl.BlockSpec((D_MODEL, D_MODEL), lambda i: (0, 0)),
                  pl.BlockSpec((PEER_HEADS, 2, N_KEYS, N_KEYS), lambda i: (0, 0, 0, 0))],
        out_specs=[pl.BlockSpec((PEER_HEADS, N_KEYS, tm), tok3),
                   pl.BlockSpec((PEER_HEADS, tm // 128, N_KEYS, 128), lambda i: (0, i, 0, 0)),
                   pl.BlockSpec((PEER_HEADS, 8, tm), tok3)],
        out_shape=[jax.ShapeDtypeStruct((PEER_HEADS, N_KEYS, m), F32),
                   jax.ShapeDtypeStruct((PEER_HEADS, m // 128, N_KEYS, 128), F32),
                   jax.ShapeDtypeStruct((PEER_HEADS, 8, m), F32)],
        scratch_shapes=[pltpu.VMEM((2 * PEER_HEADS, tm, N_KEYS), F32)],
        compiler_params=_params(("parallel",)),
        name="peer_route",
    )(h, wq_bf16, keys_bf16)


def _peer_body(te, ne, ht_ref, u_ref, vt_ref, s0_ref, s1_ref, aux_ref, h_ref, g_ref, b_ref, y_ref,
               acc_ref, e1_ref, cf_ref, st0_ref, st1_ref, w0_ref, w1_ref):
    j = pl.program_id(1)
    tm = acc_ref.shape[1]

    @pl.when(j == 0)
    def _():
        acc_ref[...] = jnp.zeros_like(acc_ref)
        for ref in (st0_ref, st1_ref, w0_ref, w1_ref):
            ref[...] = jnp.zeros_like(ref)
        for hh in range(PEER_HEADS):
            aux = aux_ref[hh]
            for lb in range(tm // 128):
                e1_ref[hh, lb] = jnp.exp(s1_ref[hh, lb] - aux[2:3, lb * 128:(lb + 1) * 128])
            cf_ref[hh] = jnp.exp(s0_ref[hh] - aux[1:2, :]) * aux[3:4, :]

    def stages(st_new, st_old, w_new, w_old):
        jb = j - 1
        live = (jb >= 0) & (jb < ne)
        jb_c = jnp.clip(jb, 0, ne - 1)
        taus = [jnp.where(live, aux_ref[hh, 0:1, :], jnp.inf) for hh in range(PEER_HEADS)]
        ncc = te // N_KEYS
        s0rows = [[s0_ref[hh, pl.ds(jb_c * ncc + cc, 1), :] for hh in range(PEER_HEADS)] for cc in range(ncc)]
        cfrows = [[cf_ref[hh, pl.ds(jb_c * ncc + cc, 1), :] for hh in range(PEER_HEADS)] for cc in range(ncc)]
        tw = min(256, tm)
        mxu_pieces, gate_tiles = [], []
        for t0 in range(0, tm, tw):
            ts = slice(t0, t0 + tw)
            lbs = range(t0 // 128, (t0 + tw) // 128)

            def mix(mr, ts=ts, lbs=lbs):
                acc_ref[mr, ts] += _bdot(vt_ref[mr, :], jnp.concatenate([w_old[lb] for lb in lbs], axis=1))

            def score(er, t0=t0, ts=ts, lbs=lbs):
                st = _bdot(u_ref[er, :], ht_ref[:, ts])
                for lb in lbs:
                    st_new[lb, er, :] = st[:, lb * 128 - t0:(lb + 1) * 128 - t0]

            for q in range(4):
                mxu_pieces.append(functools.partial(mix, slice(q * (D_MODEL // 4), (q + 1) * (D_MODEL // 4))))
            for q in range(2):
                mxu_pieces.append(functools.partial(score, slice(q * (te // 2), (q + 1) * (te // 2))))

            def gate(lb, cc):
                ls = slice(lb * 128, (lb + 1) * 128)
                rows = slice(cc * N_KEYS, (cc + 1) * N_KEYS)
                gsum = None
                for hh in range(PEER_HEADS):
                    sel = (s1_ref[hh, lb] + s0rows[cc][hh][:, ls]) >= taus[hh][:, ls]
                    term = jnp.where(sel, e1_ref[hh, lb] * cfrows[cc][hh][:, ls], 0.0)
                    gsum = term if gsum is None else gsum + term
                so = st_old[lb, rows, :]
                act = 0.5 * so * (1.0 + lax.erf(so * np.float32(np.sqrt(0.5))))
                w_new[lb, rows, :] = (gsum * act).astype(BF16)

            for lb in lbs:
                for cc in range(ncc):
                    gate_tiles.append(functools.partial(gate, lb, cc))
        for i in range(max(len(mxu_pieces), len(gate_tiles))):
            if i < len(gate_tiles):
                gate_tiles[i]()
            if i < len(mxu_pieces):
                mxu_pieces[i]()

    @pl.when(j % 2 == 0)
    def _():
        stages(st0_ref, st1_ref, w1_ref, w0_ref)

    @pl.when(j % 2 == 1)
    def _():
        stages(st1_ref, st0_ref, w0_ref, w1_ref)

    @pl.when(j == ne + 1)
    def _():
        out = acc_ref[...].T
        y_ref[...] = _layer_norm(DN_ALPHA * h_ref[...] + out, g_ref[...], b_ref[...])


def _peer(ht, u_bf16, vt_bf16, s0t, s1t, aux, h, ln_g, ln_b, tm=512, te=512):
    m = h.shape[0]
    tm = min(tm, m)
    ne = N_EXPERTS // te
    tok3 = lambda i, j: (0, 0, i)
    return pl.pallas_call(
        functools.partial(_peer_body, te, ne),
        grid=(m // tm, ne + 2),
        in_specs=[pl.BlockSpec((D_MODEL, tm), lambda i, j: (0, i)),
                  pl.BlockSpec((te, D_MODEL), lambda i, j: (jnp.minimum(j, ne - 1), 0)),
                  pl.BlockSpec((D_MODEL, te), lambda i, j: (0, jnp.clip(j - 2, 0, ne - 1))),
                  pl.BlockSpec((PEER_HEADS, N_KEYS, tm), tok3),
                  pl.BlockSpec((PEER_HEADS, tm // 128, N_KEYS, 128), lambda i, j: (0, i, 0, 0)),
                  pl.BlockSpec((PEER_HEADS, 8, tm), tok3),
                  pl.BlockSpec((tm, D_MODEL), lambda i, j: (i, 0)),
                  pl.BlockSpec((1, D_MODEL), lambda i, j: (0, 0)),
                  pl.BlockSpec((1, D_MODEL), lambda i, j: (0, 0))],
        out_specs=pl.BlockSpec((tm, D_MODEL), lambda i, j: (i, 0)),
        out_shape=jax.ShapeDtypeStruct((m, D_MODEL), F32),
        scratch_shapes=[pltpu.VMEM((D_MODEL, tm), F32),
                        pltpu.VMEM((PEER_HEADS, tm // 128, N_KEYS, 128), F32),
                        pltpu.VMEM((PEER_HEADS, N_KEYS, tm), F32),
                        pltpu.VMEM((tm // 128, te, 128), F32),
                        pltpu.VMEM((tm // 128, te, 128), F32),
                        pltpu.VMEM((tm // 128, te, 128), BF16),
                        pltpu.VMEM((tm // 128, te, 128), BF16)],
        compiler_params=_params(("parallel", "arbitrary")),
        name="peer_dense",
    )(ht, u_bf16, vt_bf16, s0t, s1t, aux, h, ln_g, ln_b)


def _pad_rows(w, lo, total):
    return jnp.zeros((total, w.shape[1]), w.dtype).at[lo:lo + w.shape[0]].set(w)


def _layer(x2d, nseq, seq_len, chunk, prev_shift, state0, cache, wts):
    proj = _inproj(x2d, wts["w_in"])
    o_a, s_last = _rwkv(proj, nseq, seq_len // chunk, chunk, prev_shift, state0, wts)
    if cache is None:
        o_b = _attn_prompt(proj, wts["sinks"])
    else:
        o_b = _attn_sample(proj, cache[0], cache[1], wts["sinks"], seq_len)
    h, ht = _merge(proj, o_a, o_b, x2d, wts["w_out"], wts["ln1_g"], wts["ln1_b"])
    s0t, s1t, aux = _route(h, wts["peer_wq"], wts["peer_keys"])
    y = _peer(ht, wts["peer_u"], wts["peer_vt"], s0t, s1t, aux, h, wts["ln2_g"], wts["ln2_b"])
    return y, proj, s_last


def kernel(x_prompt, x_sample, state_shift, state_wkv, cache_k, cache_v, w_in, rw_mu, rw_w0, rw_w_up, rw_a0,
           rw_a_up, rw_g_up, rw_k_k, rw_k_a, rw_r_k, rw_gn_g, rw_gn_b, att_sinks, w_out, ln1_g, ln1_b,
           peer_wq, peer_keys, peer_u, peer_v, ln2_g, ln2_b):
    depth = w_in.shape[0]
    assert depth == 1
    l = 0
    bp, tp, _ = x_prompt.shape
    bs, ts, _ = x_sample.shape
    assert bp == 1
    win = cache_k.shape[2]
    assert win == WINDOW and tp % WINDOW == 0 and ts <= 8

    w = w_in[l]
    q0 = SHIFT_W
    k0 = q0 + D_MODEL
    g0 = k0 + 2 * KV_W
    w_perm = jnp.concatenate(
        [w[:, :RKV_W], w[:, q0:q0 + D_MODEL], w[:, g0:g0 + 2 * D_MODEL], w[:, RKV_W:SHIFT_W],
         jnp.zeros((D_MODEL, LORA_PAD - LORA_W), w.dtype), w[:, k0:k0 + 2 * KV_W]], axis=1).astype(BF16)
    mu = rw_mu[l]
    row = lambda v: v.reshape(1, -1)
    wts = dict(
        w_in=w_perm,
        mu_rkv=row(mu[:RKV_W]),
        mu_lora=row(jnp.pad(mu[RKV_W:], (0, LORA_PAD - LORA_W))),
        w0=row(rw_w0[l]), a0=row(rw_a0[l]), k_k=row(rw_k_k[l]), k_a=row(rw_k_a[l]), r_k=row(rw_r_k[l]),
        gn_g=row(rw_gn_g[l]), gn_b=row(rw_gn_b[l]),
        w_up=_pad_rows(rw_w_up[l], 0, LORA_PAD).astype(BF16),
        a_up=_pad_rows(rw_a_up[l], 96, LORA_PAD).astype(BF16),
        g_up=_pad_rows(rw_g_up[l], 192, LORA_PAD).astype(BF16),
        sinks=jnp.repeat(att_sinks[l], 128).reshape(1, 32 * 128),
        w_out=w_out[l].astype(BF16),
        ln1_g=row(ln1_g[l]), ln1_b=row(ln1_b[l]), ln2_g=row(ln2_g[l]), ln2_b=row(ln2_b[l]),
        peer_wq=peer_wq[l].astype(BF16),
        peer_keys=peer_keys[l].astype(BF16),
        peer_u=peer_u[l].astype(BF16),
        peer_vt=peer_v[l].T.astype(BF16),
    )

    def unshift(p_last):
        return jnp.concatenate([p_last[..., :RKV_W], p_last[..., COL_LORA:COL_LORA + LORA_W]], axis=-1)

    chunk_p = 64
    y_p, proj_p, s_p = _layer(x_prompt[0], 1, tp, chunk_p,
                              jnp.zeros((1, SHIFT_W), F32), jnp.zeros((1, D_MODEL, HEAD_DIM), F32), None, wts)
    keep = min(WINDOW, tp)
    kv_p = proj_p[tp - keep:, COL_KV:]
    y_prompt = y_p[None]
    new_shift_prompt = unshift(proj_p[tp - 1])[None, None]
    new_wkv_prompt = s_p.reshape(1, 1, D_MODEL // HEAD_DIM, HEAD_DIM, HEAD_DIM)
    new_k_prompt = kv_p[:, :KV_W].reshape(1, 1, keep, KV_W // HEAD_DIM, HEAD_DIM)
    new_v_prompt = kv_p[:, KV_W:].reshape(1, 1, keep, KV_W // HEAD_DIM, HEAD_DIM)

    ck = cache_k[l].reshape(bs, win, KV_W)
    cv = cache_v[l].reshape(bs, win, KV_W)
    y_s, proj_s, s_s = _layer(x_sample.reshape(bs * ts, D_MODEL), bs, ts, ts,
                              state_shift[l], state_wkv[l].reshape(bs, D_MODEL, HEAD_DIM), (ck, cv), wts)
    proj_s3 = proj_s.reshape(bs, ts, PROJ_W)
    y_sample = y_s.reshape(bs, ts, D_MODEL)
    new_shift_sample = unshift(proj_s3[:, ts - 1])[None]
    new_wkv_sample = s_s.reshape(1, bs, D_MODEL // HEAD_DIM, HEAD_DIM, HEAD_DIM)
    kv_s = proj_s3[:, :, COL_KV:]
    new_k_sample = jnp.concatenate([ck, kv_s[:, :, :KV_W]], axis=1)[:, ts:].reshape(
        1, bs, win, KV_W // HEAD_DIM, HEAD_DIM)
    new_v_sample = jnp.concatenate([cv, kv_s[:, :, KV_W:]], axis=1)[:, ts:].reshape(
        1, bs, win, KV_W // HEAD_DIM, HEAD_DIM)

    return (y_prompt, y_sample, new_shift_prompt, new_wkv_prompt, new_k_prompt, new_v_prompt,
            new_shift_sample, new_wkv_sample, new_k_sample, new_v_sample)
```

```python
import functools

import numpy as np
import jax
import jax.numpy as jnp
from jax import lax
from jax.experimental import pallas as pl
from jax.experimental.pallas import tpu as pltpu

F32, BF16 = jnp.float32, jnp.bfloat16

D_MODEL = 2048
HEAD_DIM = 64
HEADS_PER_GROUP = 2
GROUP_W = HEADS_PER_GROUP * HEAD_DIM
N_GROUPS = D_MODEL // GROUP_W
RKV_W = 3 * D_MODEL
LORA_W = 96 + 96 + 256
LORA_PAD = 512
SHIFT_W = RKV_W + LORA_W
KV_W = 256
GN_EPS = 64e-5
LN_EPS = 1e-5
WINDOW = 128
ATT_SCALE = HEAD_DIM ** -0.5
N_KEYS = 128
N_EXPERTS = N_KEYS * N_KEYS
PEER_HEADS = 8
PEER_TOPK = 16
DN_ALPHA = 2.0 ** 0.25

COL_Q = RKV_W
COL_GATE = COL_Q + D_MODEL
COL_LORA = COL_GATE + 2 * D_MODEL
COL_KV = COL_LORA + LORA_PAD
PROJ_W = COL_KV + 2 * KV_W

VMEM_LIMIT = 56 * 1024 * 1024

_NN = (((1,), (0,)), ((), ()))
_NT = (((1,), (1,)), ((), ()))
_TN = (((0,), (0,)), ((), ()))


def _bdot(a, b, dn=_NN):
    return lax.dot_general(a, b, dn, preferred_element_type=F32)


def _dot(a, b, dn=_NN):
    return _bdot(a.astype(BF16), b.astype(BF16), dn)


def _split(x, n):
    parts, r = [], x
    for i in range(n):
        p = r.astype(BF16)
        parts.append(p)
        if i + 1 < n:
            r = r - p.astype(F32)
    return parts


def _dot3(a, b, dn=_NN):
    a1, a2 = _split(a, 2)
    b1, b2 = _split(b, 2)
    return (_bdot(a1, b2, dn) + _bdot(a2, b1, dn)) + _bdot(a1, b1, dn)


def _dot_xl(a_bf16, b, dn=_NN, passes=3):
    out = None
    for part in reversed(_split(b, passes)):
        t = _bdot(a_bf16, part, dn)
        out = t if out is None else out + t
    return out


def _dot_xr(a, b_bf16, dn=_NN, passes=3):
    out = None
    for part in reversed(_split(a, passes)):
        t = _bdot(part, b_bf16, dn)
        out = t if out is None else out + t
    return out


def _mm(passes):
    return _dot if passes == 1 else _dot3


def _sigmoid(x):
    return 1.0 / (1.0 + jnp.exp(-x))


def _iota(shape, dim):
    return lax.broadcasted_iota(jnp.int32, shape, dim)


def _head_of(idx):
    return jnp.right_shift(idx, 6)


def _ones_where(mask):
    return jnp.where(mask, 1.0, 0.0).astype(BF16)


def _params(sem):
    return pltpu.CompilerParams(dimension_semantics=sem, vmem_limit_bytes=VMEM_LIMIT)


def _inproj_body(x_ref, w_ref, o_ref, xb_ref):
    @pl.when(pl.program_id(1) == 0)
    def _():
        xb_ref[...] = x_ref[...].astype(BF16)

    o_ref[...] = _bdot(xb_ref[...], w_ref[...])


def _inproj(x2d, w_bf16, tn=1024):
    m = x2d.shape[0]
    tm = min(m, 1024)
    n = w_bf16.shape[1]
    return pl.pallas_call(
        _inproj_body,
        grid=(m // tm, n // tn),
        in_specs=[pl.BlockSpec((tm, D_MODEL), lambda i, j: (i, 0)),
                  pl.BlockSpec((D_MODEL, tn), lambda i, j: (0, j))],
        out_specs=pl.BlockSpec((tm, tn), lambda i, j: (i, j)),
        out_shape=jax.ShapeDtypeStruct((m, n), F32),
        scratch_shapes=[pltpu.VMEM((tm, D_MODEL), BF16)],
        compiler_params=_params(("parallel", "arbitrary")),
        name="inproj",
    )(x2d, w_bf16)


RWKV_PASSES = dict(seg=2, cum=3, lblk=1, inv=1, apply=1, state=1, gn=2)
ROWS = 64


def _rwkv_body(C, nb, ncs, nsteps, gpb, ps,
               r_ref, k_ref, v_ref, lo_ref, pr_ref, pk_ref, pv_ref, plo_ref, s0_ref,
               mur_ref, muk_ref, muv_ref, mulo_ref, w0_ref, a0_ref, kk_ref, ka_ref, rk_ref, gng_ref, gnb_ref,
               wup_ref, aup_ref, gup_ref,
               o_ref, sout_ref,
               sbd_ref, sh_ref, shl_ref):
    assert HEADS_PER_GROUP == 2 and HEADS_PER_GROUP * ROWS == GROUP_W and (ncs == 1 or nb == 1)
    c = pl.program_id(2)
    w = gpb * GROUP_W
    srows = ncs * ROWS
    log2c = C.bit_length() - 1
    st = HEADS_PER_GROUP * ROWS
    sq = (st, st)
    ri, ci = _iota(sq, 0), _iota(sq, 1)
    same_head = _head_of(ri) == _head_of(ci)
    tr, tc = ri & (ROWS - 1), ci & (ROWS - 1)
    blk = same_head & (jnp.right_shift(tr, log2c) == jnp.right_shift(tc, log2c))
    strict = blk & (tr > tc)
    incl = blk & (tr >= tc)
    eye = jnp.where(ri == ci, 1.0, 0.0)
    block_ones = _ones_where(same_head)
    r64, c64 = _iota((ROWS, ROWS), 0), _iota((ROWS, ROWS), 1)
    tri_seq = _ones_where((jnp.right_shift(r64, log2c) == jnp.right_shift(c64, log2c)) & (r64 >= c64))
    lane_head = _head_of(_iota((ROWS, GROUP_W), 1))
    rep = _ones_where((_iota((HEAD_DIM, GROUP_W), 1) & (HEAD_DIM - 1)) == _iota((HEAD_DIM, GROUP_W), 0))
    rep_t = _ones_where((_iota((GROUP_W, HEAD_DIM), 0) & (HEAD_DIM - 1)) == _iota((GROUP_W, HEAD_DIM), 1))
    mm_l, mm_i, mm_a, mm_s = _mm(ps["lblk"]), _mm(ps["inv"]), _mm(ps["apply"]), _mm(ps["state"])

    @pl.when(c == 0)
    def _init():
        if nb == 1:
            sh_ref[7:8, 0:w] = pr_ref[0:1, :]
            sh_ref[7:8, w:2 * w] = pk_ref[0:1, :]
            sh_ref[7:8, 2 * w:3 * w] = pv_ref[0:1, :]
            shl_ref[7:8, :] = plo_ref[0:1, :]
        for b in range(nb):
            for gi in range(gpb):
                s0 = s0_ref[b, gi * HEADS_PER_GROUP:(gi + 1) * HEADS_PER_GROUP].reshape(GROUP_W, HEAD_DIM)
                sbd_ref[b, gi] = jnp.where(same_head, _dot_xr(s0, rep), 0.0)

    def lerp(x, ref, lo, hi, prev_ref, mu):
        ref[pl.ds(8, srows), lo:hi] = x
        xs = ref[pl.ds(7, srows), lo:hi]
        ref[7:8, lo:hi] = x[srows - 1:srows, :]
        if nb > 1:
            first = (_iota(x.shape, 0) & (C - 1)) == 0
            xs = jnp.where(first, prev_ref[...], xs)
        return x + (xs - x) * mu

    r = lerp(r_ref[...], sh_ref, 0, w, pr_ref, mur_ref[...])
    k = lerp(k_ref[...], sh_ref, w, 2 * w, pk_ref, muk_ref[...])
    v = lerp(v_ref[...], sh_ref, 2 * w, 3 * w, pv_ref, muv_ref[...])
    lo = lerp(lo_ref[...], shl_ref, 0, LORA_PAD, plo_ref, mulo_ref[...])

    wl = w0_ref[...] + _dot(jnp.tanh(lo), wup_ref[...])
    w_log = -(jnp.maximum(-wl, 0.0) + jnp.log1p(jnp.exp(-jnp.abs(wl)))) - 0.5
    logw_all = -jnp.exp(w_log)
    a_all = _sigmoid(a0_ref[...] + _dot(lo, aup_ref[...]))
    gate_all = _dot(_sigmoid(lo), gup_ref[...])
    kk_all = k * kk_ref[...]
    k2_all = k * (1.0 + (a_all - 1.0) * ka_ref[...])

    def stack(x):
        return jnp.concatenate([jnp.where(lane_head == j, x, 0.0) for j in range(HEADS_PER_GROUP)], axis=0)

    def fold(x):
        return x[0:ROWS] + x[ROWS:2 * ROWS]

    gs = range(ncs * gpb)
    sls = [slice(gi * GROUP_W, (gi + 1) * GROUP_W) for _ in range(ncs) for gi in range(gpb)]
    rws = [slice(ck * ROWS, (ck + 1) * ROWS) for ck in range(ncs) for _ in range(gpb)]
    rg = [r[rw, sl] for rw, sl in zip(rws, sls)]
    vg = [v[rw, sl] for rw, sl in zip(rws, sls)]
    k2 = [k2_all[rw, sl] for rw, sl in zip(rws, sls)]
    kk = [kk_all[rw, sl] for rw, sl in zip(rws, sls)]
    ss = [_dot_xr(x * x, block_ones, passes=ps["seg"]) for x in kk]
    kkn = [x / jnp.maximum(jnp.sqrt(s2), 1e-12) for x, s2 in zip(kk, ss)]
    logw = [logw_all[rw, sl] for rw, sl in zip(rws, sls)]
    cum = [_dot_xl(tri_seq, x, passes=ps["cum"]) for x in logw]
    gam = [jnp.exp(x) for x in cum]
    ginv = [jnp.exp(-x) for x in cum]
    a4 = [-(jnp.exp(cm - lw) * kn) for cm, lw, kn in zip(cum, logw, kkn)]
    b4 = [kn * a_all[rw, sl] * gi_ for kn, rw, sl, gi_ in zip(kkn, rws, sls, ginv)]
    k4 = [x * gi_ for x, gi_ in zip(k2, ginv)]
    r4 = [gm * x for gm, x in zip(gam, rg)]

    single = all(ps[c_] == 1 for c_ in ("lblk", "inv", "apply", "state"))
    nar = (lambda x: x.astype(BF16)) if single else (lambda x: x)
    a_st = [nar(stack(x)) for x in a4]
    r_st = [nar(stack(x)) for x in r4]
    v_st = [nar(stack(x)) for x in vg]
    prod = [mm_l(jnp.concatenate([x, y_], axis=0), jnp.concatenate([z, q_], axis=0), _NT)
            for x, y_, z, q_ in zip(a_st, r_st, b4, k4)]
    low = _iota(sq, 1) < HEAD_DIM

    def diag_blocks(x, mask):
        xr = pltpu.roll(x, HEAD_DIM, axis=1)
        return jnp.where(mask, jnp.where(low, x, xr), 0.0), jnp.where(mask, jnp.where(low, xr, x), 0.0)

    l_bk = [diag_blocks(x[:st], strict) for x in prod]
    m_bk = [diag_blocks(x[st:], incl) for x in prod]
    l_b, l_k = [x[0] for x in l_bk], [nar(x[1]) for x in l_bk]
    m_b, m_k = [nar(x[0]) for x in m_bk], [nar(x[1]) for x in m_bk]
    t = [nar(eye + x) for x in l_b]
    l_bn = [nar(x) for x in l_b]
    pw = [nar(mm_i(x, x)) for x in l_bn]
    cov = 2
    while cov < C:
        if 2 * cov >= C:
            t = [nar(x + mm_i(p_, x)) for x, p_ in zip(t, pw)]
        else:
            both = [mm_i(p_, jnp.concatenate([p_, x], axis=1)) for x, p_ in zip(t, pw)]
            pw = [nar(x[:, :st]) for x in both]
            t = [nar(x + y_[:, st:]) for x, y_ in zip(t, both)]
        cov *= 2
    lm_v = [mm_a(jnp.concatenate([x, y_], axis=0), z) for x, y_, z in zip(l_k, m_k, v_st)]
    mkv4 = [fold(x[st:]) for x in lm_v]
    t_av = [mm_a(x, jnp.concatenate([y_, nar(z[:st])], axis=1)) for x, y_, z in zip(t, a_st, lm_v)]
    wm4 = [fold(x[:, :st]) for x in t_av]
    uk4 = [fold(x[:, st:]) for x in t_av]

    us = [[None] * nb for _ in gs]
    ys = [[None] * nb for _ in gs]
    for ck in range(ncs):
        cgs = range(ck * gpb, (ck + 1) * gpb)
        for b in range(nb):
            rows = slice(b * C, (b + 1) * C)
            s_old = {gi: sbd_ref[b, gi - ck * gpb] for gi in cgs}
            uy = {gi: mm_s(jnp.concatenate([wm4[gi][rows], r4[gi][rows]], axis=0), s_old[gi], _NT) for gi in cgs}
            for gi in cgs:
                us[gi][b] = uy[gi][:C] + uk4[gi][rows]
                ys[gi][b] = uy[gi][C:]
            ds = {gi: mm_s(jnp.concatenate([us[gi][b], vg[gi][rows]], axis=0),
                           jnp.concatenate([b4[gi][rows], k4[gi][rows]], axis=0), _TN) for gi in cgs}
            for gi in cgs:
                s_new = (s_old[gi] + jnp.where(same_head, ds[gi], 0.0)) * gam[gi][(b + 1) * C - 1:(b + 1) * C, :]
                sbd_ref[b, gi - ck * gpb] = s_new
                if ck == ncs - 1:
                    @pl.when(c == nsteps - 1)
                    def _fin():
                        hs = slice((gi - ck * gpb) * HEADS_PER_GROUP, (gi - ck * gpb + 1) * HEADS_PER_GROUP)
                        sout_ref[b, hs] = _dot_xr(s_new, rep_t).reshape(HEADS_PER_GROUP, HEAD_DIM, HEAD_DIM)

    u = [x[0] if nb == 1 else jnp.concatenate(x, axis=0) for x in us]
    ysum = [x[0] if nb == 1 else jnp.concatenate(x, axis=0) for x in ys]
    yb = [fold(mm_s(x, stack(y_))) for x, y_ in zip(m_b, u)]
    y = [x + y_ + z for x, y_, z in zip(ysum, mkv4, yb)]

    ym = [_dot_xr(x, block_ones, passes=ps["gn"]) * (1.0 / HEAD_DIM) for x in y]
    yc = [x - y_ for x, y_ in zip(y, ym)]
    yv = [_dot_xr(x * x, block_ones, passes=ps["gn"]) * (1.0 / HEAD_DIM) for x in yc]
    bonus = [_dot_xr(rg[gi] * k2[gi] * rk_ref[:, sls[gi]], block_ones, passes=ps["gn"]) * vg[gi] for gi in gs]
    for gi in gs:
        sl, rw = sls[gi], rws[gi]
        yn = yc[gi] * lax.rsqrt(yv[gi] + GN_EPS) * gng_ref[:, sl] + gnb_ref[:, sl]
        o_ref[rw, sl] = (yn + bonus[gi]) * gate_all[rw, sl]


def _rwkv(proj, nseq, nchunk, C, prev_shift, state0, prm, gpb=None, ncs=None):
    nb = ROWS // C
    if gpb is None:
        gpb = N_GROUPS if nb == 1 else N_GROUPS // 2
    if ncs is None:
        ncs = 2 if nb == 1 and nchunk % 2 == 0 else 1
    assert nb * C == ROWS and nseq % nb == 0 and (nb == 1 or nchunk == 1) and nchunk % ncs == 0
    rows = nseq * nchunk * C
    nsteps = nchunk // ncs
    srows = ncs * ROWS
    w = gpb * GROUP_W
    sec = D_MODEL // w
    prows = srows if nb == 1 else C
    prev = jnp.zeros((nseq, prows, SHIFT_W), F32).at[:, 0].set(prev_shift).reshape(nseq * prows, SHIFT_W)
    prev_rkv = prev[:, :RKV_W]
    prev_lora = jnp.pad(prev[:, RKV_W:], ((0, 0), (0, LORA_PAD - LORA_W)))

    def tok(off):
        return pl.BlockSpec((srows, w), lambda s, g, c: (s * nsteps + c, off + g))

    def prv(off):
        return pl.BlockSpec((srows, w), lambda s, g, c: (s, off + g))

    def vec(off=0):
        return pl.BlockSpec((1, w), lambda s, g, c: (0, off + g))

    up = pl.BlockSpec((LORA_PAD, w), lambda s, g, c: (0, g))
    in_specs = [
        tok(0), tok(sec), tok(2 * sec),
        pl.BlockSpec((srows, LORA_PAD), lambda s, g, c: (s * nsteps + c, COL_LORA // LORA_PAD)),
        prv(0), prv(sec), prv(2 * sec),
        pl.BlockSpec((srows, LORA_PAD), lambda s, g, c: (s, 0)),
        pl.BlockSpec((nb, gpb * HEADS_PER_GROUP, HEAD_DIM, HEAD_DIM), lambda s, g, c: (s, g, 0, 0)),
        vec(0), vec(sec), vec(2 * sec),
        pl.BlockSpec((1, LORA_PAD), lambda s, g, c: (0, 0)),
        vec(), vec(), vec(), vec(), vec(), vec(), vec(),
        up, up, up,
    ]
    out_specs = [pl.BlockSpec((srows, w), lambda s, g, c: (s * nsteps + c, g)),
                 pl.BlockSpec((nb, gpb * HEADS_PER_GROUP, HEAD_DIM, HEAD_DIM), lambda s, g, c: (s, g, 0, 0))]
    return pl.pallas_call(
        functools.partial(_rwkv_body, C, nb, ncs, nsteps, gpb, RWKV_PASSES),
        grid=(nseq // nb, N_GROUPS // gpb, nsteps),
        in_specs=in_specs,
        out_specs=out_specs,
        out_shape=[jax.ShapeDtypeStruct((rows, D_MODEL), F32),
                   jax.ShapeDtypeStruct((nseq, D_MODEL // HEAD_DIM, HEAD_DIM, HEAD_DIM), F32)],
        scratch_shapes=[pltpu.VMEM((nb, gpb, GROUP_W, GROUP_W), F32),
                        pltpu.VMEM((srows + 8, 3 * w), F32),
                        pltpu.VMEM((srows + 8, LORA_PAD), F32)],
        compiler_params=_params(("arbitrary", "arbitrary", "arbitrary")),
        name="rwkv",
    )(proj, proj, proj, proj, prev_rkv, prev_rkv, prev_rkv, prev_lora, state0,
      prm["mu_rkv"], prm["mu_rkv"], prm["mu_rkv"], prm["mu_lora"],
      prm["w0"], prm["a0"], prm["k_k"], prm["k_a"], prm["r_k"], prm["gn_g"], prm["gn_b"],
      prm["w_up"], prm["a_up"], prm["g_up"])


def _attn_rows(q, kcat, vcat, sink_ref, first_key, store, npr):
    m = q.shape[0]
    kt_all = kcat.T
    qi = _iota((npr * m, 2 * WINDOW), 0) & (m - 1)
    ki = _iota((npr * m, 2 * WINDOW), 1)
    allowed = (ki >= qi + 1) & (ki <= qi + WINDOW) & (ki >= first_key)
    lane = _iota((2 * WINDOW, 128), 1)
    ones_k = jnp.ones((2 * WINDOW, 128), BF16)
    zk = jnp.zeros((HEAD_DIM, 2 * WINDOW), F32)
    nkv = KV_W // HEAD_DIM
    k2s, v2s = [], []
    for kv in range(nkv):
        kt = kt_all[kv * HEAD_DIM:(kv + 1) * HEAD_DIM, :]
        k2s.append(jnp.concatenate([jnp.concatenate([kt, zk], axis=1),
                                    jnp.concatenate([zk, kt], axis=1)], axis=0).astype(BF16))
        slab = vcat[:, (kv // 2) * 128:(kv // 2 + 1) * 128]
        rolled = pltpu.roll(slab, HEAD_DIM, axis=1)
        lo_src, hi_src = (slab, rolled) if kv % 2 == 0 else (rolled, slab)
        v2s.append(jnp.concatenate([jnp.where(lane < HEAD_DIM, lo_src, 0.0),
                                    jnp.where(lane >= HEAD_DIM, hi_src, 0.0)], axis=0).astype(BF16))
    items = [(kv, p0) for kv in range(nkv) for p0 in range(0, 4, npr)]
    cols = [[kv * 512 + (p0 + i) * 128 for i in range(npr)] for kv, p0 in items]
    scores = [_bdot((jnp.concatenate([q[:, col:col + 128] for col in cs], axis=0) * ATT_SCALE).astype(BF16),
                    k2s[kv]) for (kv, _), cs in zip(items, cols)]
    probs = []
    for (kv, p0), s in zip(items, scores):
        ps = []
        for hb in range(2):
            sink = jnp.concatenate(
                [jnp.broadcast_to(sink_ref[:, (kv * 8 + (p0 + i) * 2 + hb) * 128:
                                           (kv * 8 + (p0 + i) * 2 + hb) * 128 + 1], (m, 1))
                 for i in range(npr)], axis=0)
            sh = jnp.where(allowed, s[:, hb * 256:(hb + 1) * 256], -jnp.inf)
            mx = jnp.maximum(jnp.max(sh, axis=1, keepdims=True), sink)
            e = jnp.exp(sh - mx)
            inv = 1.0 / (_dot_xr(e, ones_k, passes=2) + jnp.exp(sink - mx))
            ps += [e[:, :128] * inv, e[:, 128:] * inv]
        probs.append(jnp.concatenate(ps, axis=1).astype(BF16))
    outs = [_bdot(p, v2s[kv]) for (kv, _), p in zip(items, probs)]
    for cs, o in zip(cols, outs):
        for i, col in enumerate(cs):
            store(col, o[i * m:(i + 1) * m])


def _attn_prompt_body(q_ref, kvc_ref, kvp_ref, sink_ref, o_ref):
    n = pl.program_id(0)
    kvc = kvc_ref[...]
    kvp = kvp_ref[...]
    kcat = jnp.concatenate([kvp[:, :KV_W], kvc[:, :KV_W]], axis=0)
    vcat = jnp.concatenate([kvp[:, KV_W:], kvc[:, KV_W:]], axis=0)

    def store(col, val):
        o_ref[:, col:col + 128] = val

    _attn_rows(q_ref[...], kcat, vcat, sink_ref, jnp.where(n > 0, 0, WINDOW), store, npr=1)


def _attn_prompt(proj, sinks_e):
    t = proj.shape[0]
    nb = t // WINDOW
    qb, kvb = COL_Q // D_MODEL, COL_KV // (2 * KV_W)
    return pl.pallas_call(
        _attn_prompt_body,
        grid=(nb,),
        in_specs=[pl.BlockSpec((WINDOW, D_MODEL), lambda n: (n, qb)),
                  pl.BlockSpec((WINDOW, 2 * KV_W), lambda n: (n, kvb)),
                  pl.BlockSpec((WINDOW, 2 * KV_W), lambda n: (jnp.maximum(n - 1, 0), kvb)),
                  pl.BlockSpec((1, 32 * 128), lambda n: (0, 0))],
        out_specs=pl.BlockSpec((WINDOW, D_MODEL), lambda n: (n, 0)),
        out_shape=jax.ShapeDtypeStruct((t, D_MODEL), F32),
        compiler_params=_params(("parallel",)),
        name="attn_prompt",
    )(proj, proj, proj, sinks_e)


def _attn_sample_body(L, bt, q_ref, kvn_ref, ck_ref, cv_ref, sink_ref, o_ref):
    pad = jnp.zeros((WINDOW - L, KV_W), F32)

    def one(b, carry):
        rows = pl.ds(pl.multiple_of(b * L, L), L)
        kvn = kvn_ref[rows, :]
        kcat = jnp.concatenate([ck_ref[b], kvn[:, :KV_W], pad], axis=0)
        vcat = jnp.concatenate([cv_ref[b], kvn[:, KV_W:], pad], axis=0)

        def store(col, val):
            o_ref[rows, col:col + 128] = val

        _attn_rows(q_ref[rows, :], kcat, vcat, sink_ref, 0, store, npr=4)
        return carry

    lax.fori_loop(0, bt, one, 0, unroll=2)


def _attn_sample(proj, ck, cv, sinks_e, L, bt=16):
    rows = proj.shape[0]
    nb = rows // L
    bt = min(bt, nb)
    qb, kvb = COL_Q // D_MODEL, COL_KV // (2 * KV_W)
    return pl.pallas_call(
        functools.partial(_attn_sample_body, L, bt),
        grid=(nb // bt,),
        in_specs=[pl.BlockSpec((bt * L, D_MODEL), lambda i: (i, qb)),
                  pl.BlockSpec((bt * L, 2 * KV_W), lambda i: (i, kvb)),
                  pl.BlockSpec((bt, WINDOW, KV_W), lambda i: (i, 0, 0)),
                  pl.BlockSpec((bt, WINDOW, KV_W), lambda i: (i, 0, 0)),
                  pl.BlockSpec((1, 32 * 128), lambda i: (0, 0))],
        out_specs=pl.BlockSpec((bt * L, D_MODEL), lambda i: (i, 0)),
        out_shape=jax.ShapeDtypeStruct((rows, D_MODEL), F32),
        compiler_params=_params(("parallel",)),
        name="attn_sample",
    )(proj, proj, ck, cv, sinks_e)


def _layer_norm(x, g, b):
    mu = jnp.mean(x, axis=-1, keepdims=True)
    xc = x - mu
    var = jnp.mean(xc * xc, axis=-1, keepdims=True)
    return xc * lax.rsqrt(var + LN_EPS) * g + b


def _merge_body(ga_ref, gb_ref, oa_ref, ob_ref, x_ref, wout_ref, g_ref, b_ref, h_ref, ht_ref):
    mixed = _sigmoid(ga_ref[...]) * oa_ref[...] + _sigmoid(gb_ref[...]) * ob_ref[...]
    y = _bdot(mixed.astype(BF16), wout_ref[...])
    h = _layer_norm(DN_ALPHA * x_ref[...] + y, g_ref[...], b_ref[...])
    h_ref[...] = h
    ht_ref[...] = h.T.astype(BF16)


def _merge(proj, o_a, o_b, x2d, wout_bf16, ln_g, ln_b, tm=256):
    m = x2d.shape[0]
    tm = min(tm, m)
    gab = COL_GATE // D_MODEL
    row = lambda i: (i, 0)
    return pl.pallas_call(
        _merge_body,
        grid=(m // tm,),
        in_specs=[pl.BlockSpec((tm, D_MODEL), lambda i: (i, gab)),
                  pl.BlockSpec((tm, D_MODEL), lambda i: (i, gab + 1)),
                  pl.BlockSpec((tm, D_MODEL), row),
                  pl.BlockSpec((tm, D_MODEL), row),
                  pl.BlockSpec((tm, D_MODEL), row),
                  pl.BlockSpec((D_MODEL, D_MODEL), lambda i: (0, 0)),
                  pl.BlockSpec((1, D_MODEL), lambda i: (0, 0)),
                  pl.BlockSpec((1, D_MODEL), lambda i: (0, 0))],
        out_specs=[pl.BlockSpec((tm, D_MODEL), row),
                   pl.BlockSpec((D_MODEL, tm), lambda i: (0, i))],
        out_shape=[jax.ShapeDtypeStruct((m, D_MODEL), F32),
                   jax.ShapeDtypeStruct((D_MODEL, m), BF16)],
        compiler_params=_params(("parallel",)),
        name="merge",
    )(proj, proj, o_a, o_b, x2d, wout_bf16, ln_g, ln_b)


def _top_rows(x, count):
    return _top_rows_multi([x], count)[0]


def _top_rows_multi(xs, count):
    rows = _iota(xs[0].shape, 0)
    big = xs[0].shape[0]
    outs = [[] for _ in xs]
    for _ in range(count):
        mxs = [jnp.max(x, axis=0, keepdims=True) for x in xs]
        for o, mx in zip(outs, mxs):
            o.append(mx)
        firsts = [jnp.min(jnp.where(x == mx, rows, big), axis=0, keepdims=True) for x, mx in zip(xs, mxs)]
        xs = [jnp.where(rows == f, -jnp.inf, x) for x, f in zip(xs, firsts)]
    return outs


def _bitonic_network(n):
    comps = []
    k = 2
    while k <= n:
        j = k // 2
        while j >= 1:
            comps += [(i, i ^ j, (i & k) == 0) for i in range(n) if (i ^ j) > i]
            j //= 2
        k *= 2
    return comps


def _top16_of_128(xs):
    nt = N_KEYS // 8
    tiles = [[x[8 * j:8 * (j + 1)] for j in range(nt)] for x in xs]
    for i, l, desc in _bitonic_network(nt):
        for t in tiles:
            hi, lo = jnp.maximum(t[i], t[l]), jnp.minimum(t[i], t[l])
            t[i], t[l] = (hi, lo) if desc else (lo, hi)
    sub = _iota(tiles[0][0].shape, 0)
    outs = [[] for _ in xs]
    for r in range(PEER_TOPK):
        for t, o in zip(tiles, outs):
            mx = jnp.max(t[0], axis=0, keepdims=True)
            o.append(mx)
            first = jnp.min(jnp.where(t[0] == mx, sub, 8), axis=0, keepdims=True)
            pop = sub == first
            for j in range(PEER_TOPK - r - 1):
                t[j] = jnp.where(pop, t[j + 1], t[j])
    return outs


def _route_body(tm, h_ref, wq_ref, keys_ref, s0_ref, s1_ref, aux_ref, q_s):
    q = _bdot(h_ref[...].astype(BF16), wq_ref[...])
    for hc in range(2 * PEER_HEADS):
        q_s[hc] = q[:, hc * N_KEYS:(hc + 1) * N_KEYS]
    pairs = [(i, j) for i in range(PEER_TOPK) for j in range(PEER_TOPK) if (i + 1) * (j + 1) <= PEER_TOPK]
    npad = -len(pairs) % 8
    neg = jnp.full((1, tm), -jnp.inf, F32)
    zero = jnp.zeros((1, tm), F32)

    def head(hh, carry):
        s0 = _bdot(keys_ref[hh, 0], q_s[2 * hh].astype(BF16), _NT)
        s1 = _bdot(keys_ref[hh, 1], q_s[2 * hh + 1].astype(BF16), _NT)
        s0_ref[hh] = s0
        for lb in range(tm // 128):
            s1_ref[hh, lb] = s1[:, lb * 128:(lb + 1) * 128]
        top0, top1 = _top16_of_128([s0, s1])
        cand = jnp.concatenate([top0[i] + top1[j] for i, j in pairs] + [neg] * npad, axis=0)
        best = _top_rows(cand, PEER_TOPK)
        mx = best[0]
        z = zero
        for bsum in best:
            z = z + jnp.exp(bsum - mx)
        aux_ref[hh] = jnp.concatenate([best[-1], top0[0], top1[0], 1.0 / z, zero, zero, zero, zero], axis=0)
        return carry

    lax.fori_loop(0, PEER_HEADS, head, 0)


def _route(h, wq_bf16, keys_bf16, tm=256):
    m = h.shape[0]
    tm = min(tm, m)
    tok3 = lambda i: (0, 0, i)
    return pl.pallas_call(
        functools.partial(_route_body, tm),
        grid=(m // tm,),
        in_specs=[pl.BlockSpec((tm, D_MODEL), lambda i: (i, 0)),
                  pl.BlockSpec((D_MODEL, D_MODEL), lambda i: (0, 0)),
                  pl.BlockSpec((PEER_HEADS, 2, N_KEYS, N_KEYS), lambda i: (0, 0, 0, 0))],
        out_specs=[pl.BlockSpec((PEER_HEADS, N_KEYS, tm), tok3),
                   pl.BlockSpec((PEER_HEADS, tm // 128, N_KEYS, 128), lambda i: (0, i, 0, 0)),
                   pl.BlockSpec((PEER_HEADS, 8, tm), tok3)],
        out_shape=[jax.ShapeDtypeStruct((PEER_HEADS, N_KEYS, m), F32),
                   jax.ShapeDtypeStruct((PEER_HEADS, m // 128, N_KEYS, 128), F32),
                   jax.ShapeDtypeStruct((PEER_HEADS, 8, m), F32)],
        scratch_shapes=[pltpu.VMEM((2 * PEER_HEADS, tm, N_KEYS), F32)],
        compiler_params=_params(("parallel",)),
        name="peer_route",
    )(h, wq_bf16, keys_bf16)


def _peer_body(te, ne, ht_ref, u_ref, vt_ref, s0_ref, s1_ref, aux_ref, h_ref, g_ref, b_ref, y_ref,
               acc_ref, e1_ref, cf_ref, st0_ref, st1_ref, w0_ref, w1_ref):
    j = pl.program_id(1)
    tm = acc_ref.shape[1]

    @pl.when(j == 0)
    def _():
        acc_ref[...] = jnp.zeros_like(acc_ref)
        for ref in (st0_ref, st1_ref, w0_ref, w1_ref):
            ref[...] = jnp.zeros_like(ref)
        for hh in range(PEER_HEADS):
            aux = aux_ref[hh]
            for lb in range(tm // 128):
                e1_ref[hh, lb] = jnp.exp(s1_ref[hh, lb] - aux[2:3, lb * 128:(lb + 1) * 128])
            cf_ref[hh] = jnp.exp(s0_ref[hh] - aux[1:2, :]) * aux[3:4, :]

    def stages(st_new, st_old, w_new, w_old):
        jb = j - 1
        live = (jb >= 0) & (jb < ne)
        jb_c = jnp.clip(jb, 0, ne - 1)
        taus = [jnp.where(live, aux_ref[hh, 0:1, :], jnp.inf) for hh in range(PEER_HEADS)]
        ncc = te // N_KEYS
        s0rows = [[s0_ref[hh, pl.ds(jb_c * ncc + cc, 1), :] for hh in range(PEER_HEADS)] for cc in range(ncc)]
        cfrows = [[cf_ref[hh, pl.ds(jb_c * ncc + cc, 1), :] for hh in range(PEER_HEADS)] for cc in range(ncc)]
        tw = min(256, tm)
        mxu_pieces, gate_tiles = [], []
        for t0 in range(0, tm, tw):
            ts = slice(t0, t0 + tw)
            lbs = range(t0 // 128, (t0 + tw) // 128)

            def mix(mr, ts=ts, lbs=lbs):
                acc_ref[mr, ts] += _bdot(vt_ref[mr, :], jnp.concatenate([w_old[lb] for lb in lbs], axis=1))

            def score(er, t0=t0, ts=ts, lbs=lbs):
                st = _bdot(u_ref[er, :], ht_ref[:, ts])
                for lb in lbs:
                    st_new[lb, er, :] = st[:, lb * 128 - t0:(lb + 1) * 128 - t0]

            for q in range(4):
                mxu_pieces.append(functools.partial(mix, slice(q * (D_MODEL // 4), (q + 1) * (D_MODEL // 4))))
            for q in range(2):
                mxu_pieces.append(functools.partial(score, slice(q * (te // 2), (q + 1) * (te // 2))))

            def gate(lb, cc):
                ls = slice(lb * 128, (lb + 1) * 128)
                rows = slice(cc * N_KEYS, (cc + 1) * N_KEYS)
                gsum = None
                for hh in range(PEER_HEADS):
                    sel = (s1_ref[hh, lb] + s0rows[cc][hh][:, ls]) >= taus[hh][:, ls]
                    term = jnp.where(sel, e1_ref[hh, lb] * cfrows[cc][hh][:, ls], 0.0)
                    gsum = term if gsum is None else gsum + term
                so = st_old[lb, rows, :]
                act = 0.5 * so * (1.0 + lax.erf(so * np.float32(np.sqrt(0.5))))
                w_new[lb, rows, :] = (gsum * act).astype(BF16)

            for lb in lbs:
                for cc in range(ncc):
                    gate_tiles.append(functools.partial(gate, lb, cc))
        for i in range(max(len(mxu_pieces), len(gate_tiles))):
            if i < len(gate_tiles):
                gate_tiles[i]()
            if i < len(mxu_pieces):
                mxu_pieces[i]()

    @pl.when(j % 2 == 0)
    def _():
        stages(st0_ref, st1_ref, w1_ref, w0_ref)

    @pl.when(j % 2 == 1)
    def _():
        stages(st1_ref, st0_ref, w0_ref, w1_ref)

    @pl.when(j == ne + 1)
    def _():
        out = acc_ref[...].T
        y_ref[...] = _layer_norm(DN_ALPHA * h_ref[...] + out, g_ref[...], b_ref[...])


def _peer(ht, u_bf16, vt_bf16, s0t, s1t, aux, h, ln_g, ln_b, tm=512, te=512):
    m = h.shape[0]
    tm = min(tm, m)
    ne = N_EXPERTS // te
    tok3 = lambda i, j: (0, 0, i)
    return pl.pallas_call(
        functools.partial(_peer_body, te, ne),
        grid=(m // tm, ne + 2),
        in_specs=[pl.BlockSpec((D_MODEL, tm), lambda i, j: (0, i)),
                  pl.BlockSpec((te, D_MODEL), lambda i, j: (jnp.minimum(j, ne - 1), 0)),
                  pl.BlockSpec((D_MODEL, te), lambda i, j: (0, jnp.clip(j - 2, 0, ne - 1))),
                  pl.BlockSpec((PEER_HEADS, N_KEYS, tm), tok3),
                  pl.BlockSpec((PEER_HEADS, tm // 128, N_KEYS, 128), lambda i, j: (0, i, 0, 0)),
                  pl.BlockSpec((PEER_HEADS, 8, tm), tok3),
                  pl.BlockSpec((tm, D_MODEL), lambda i, j: (i, 0)),
                  pl.BlockSpec((1, D_MODEL), lambda i, j: (0, 0)),
                  pl.BlockSpec((1, D_MODEL), lambda i, j: (0, 0))],
        out_specs=pl.BlockSpec((tm, D_MODEL), lambda i, j: (i, 0)),
        out_shape=jax.ShapeDtypeStruct((m, D_MODEL), F32),
        scratch_shapes=[pltpu.VMEM((D_MODEL, tm), F32),
                        pltpu.VMEM((PEER_HEADS, tm // 128, N_KEYS, 128), F32),
                        pltpu.VMEM((PEER_HEADS, N_KEYS, tm), F32),
                        pltpu.VMEM((tm // 128, te, 128), F32),
                        pltpu.VMEM((tm // 128, te, 128), F32),
                        pltpu.VMEM((tm // 128, te, 128), BF16),
                        pltpu.VMEM((tm // 128, te, 128), BF16)],
        compiler_params=_params(("parallel", "arbitrary")),
        name="peer_dense",
    )(ht, u_bf16, vt_bf16, s0t, s1t, aux, h, ln_g, ln_b)


def _pad_rows(w, lo, total):
    return jnp.zeros((total, w.shape[1]), w.dtype).at[lo:lo + w.shape[0]].set(w)


def _layer(x2d, nseq, seq_len, chunk, prev_shift, state0, cache, wts):
    proj = _inproj(x2d, wts["w_in"])
    o_a, s_last = _rwkv(proj, nseq, seq_len // chunk, chunk, prev_shift, state0, wts)
    if cache is None:
        o_b = _attn_prompt(proj, wts["sinks"])
    else:
        o_b = _attn_sample(proj, cache[0], cache[1], wts["sinks"], seq_len)
    h, ht = _merge(proj, o_a, o_b, x2d, wts["w_out"], wts["ln1_g"], wts["ln1_b"])
    s0t, s1t, aux = _route(h, wts["peer_wq"], wts["peer_keys"])
    y = _peer(ht, wts["peer_u"], wts["peer_vt"], s0t, s1t, aux, h, wts["ln2_g"], wts["ln2_b"])
    return y, proj, s_last


def kernel(x_prompt, x_sample, state_shift, state_wkv, cache_k, cache_v, w_in, rw_mu, rw_w0, rw_w_up, rw_a0,
           rw_a_up, rw_g_up, rw_k_k, rw_k_a, rw_r_k, rw_gn_g, rw_gn_b, att_sinks, w_out, ln1_g, ln1_b,
           peer_wq, peer_keys, peer_u, peer_v, ln2_g, ln2_b):
    depth = w_in.shape[0]
    assert depth == 1
    l = 0
    bp, tp, _ = x_prompt.shape
    bs, ts, _ = x_sample.shape
    assert bp == 1
    win = cache_k.shape[2]
    assert win == WINDOW and tp % WINDOW == 0 and ts <= 8

    w = w_in[l]
    q0 = SHIFT_W
    k0 = q0 + D_MODEL
    g0 = k0 + 2 * KV_W
    w_perm = jnp.concatenate(
        [w[:, :RKV_W], w[:, q0:q0 + D_MODEL], w[:, g0:g0 + 2 * D_MODEL], w[:, RKV_W:SHIFT_W],
         jnp.zeros((D_MODEL, LORA_PAD - LORA_W), w.dtype), w[:, k0:k0 + 2 * KV_W]], axis=1).astype(BF16)
    mu = rw_mu[l]
    row = lambda v: v.reshape(1, -1)
    wts = dict(
        w_in=w_perm,
        mu_rkv=row(mu[:RKV_W]),
        mu_lora=row(jnp.pad(mu[RKV_W:], (0, LORA_PAD - LORA_W))),
        w0=row(rw_w0[l]), a0=row(rw_a0[l]), k_k=row(rw_k_k[l]), k_a=row(rw_k_a[l]), r_k=row(rw_r_k[l]),
        gn_g=row(rw_gn_g[l]), gn_b=row(rw_gn_b[l]),
        w_up=_pad_rows(rw_w_up[l], 0, LORA_PAD).astype(BF16),
        a_up=_pad_rows(rw_a_up[l], 96, LORA_PAD).astype(BF16),
        g_up=_pad_rows(rw_g_up[l], 192, LORA_PAD).astype(BF16),
        sinks=jnp.repeat(att_sinks[l], 128).reshape(1, 32 * 128),
        w_out=w_out[l].astype(BF16),
        ln1_g=row(ln1_g[l]), ln1_b=row(ln1_b[l]), ln2_g=row(ln2_g[l]), ln2_b=row(ln2_b[l]),
        peer_wq=peer_wq[l].astype(BF16),
        peer_keys=peer_keys[l].astype(BF16),
        peer_u=peer_u[l].astype(BF16),
        peer_vt=peer_v[l].T.astype(BF16),
    )

    def unshift(p_last):
        return jnp.concatenate([p_last[..., :RKV_W], p_last[..., COL_LORA:COL_LORA + LORA_W]], axis=-1)

    chunk_p = 64
    y_p, proj_p, s_p = _layer(x_prompt[0], 1, tp, chunk_p,
                              jnp.zeros((1, SHIFT_W), F32),
                              jnp.zeros((1, D_MODEL // HEAD_DIM, HEAD_DIM, HEAD_DIM), F32), None, wts)
    keep = min(WINDOW, tp)
    kv_p = proj_p[tp - keep:, COL_KV:]
    y_prompt = y_p[None]
    new_shift_prompt = unshift(proj_p[tp - 1])[None, None]
    new_wkv_prompt = s_p[None]
    new_k_prompt = kv_p[:, :KV_W].reshape(1, 1, keep, KV_W // HEAD_DIM, HEAD_DIM)
    new_v_prompt = kv_p[:, KV_W:].reshape(1, 1, keep, KV_W // HEAD_DIM, HEAD_DIM)

    ck = cache_k[l].reshape(bs, win, KV_W)
    cv = cache_v[l].reshape(bs, win, KV_W)
    y_s, proj_s, s_s = _layer(x_sample.reshape(bs * ts, D_MODEL), bs, ts, ts,
                              state_shift[l], state_wkv[l], (ck, cv), wts)
    proj_s3 = proj_s.reshape(bs, ts, PROJ_W)
    y_sample = y_s.reshape(bs, ts, D_MODEL)
    new_shift_sample = unshift(proj_s3[:, ts - 1])[None]
    new_wkv_sample = s_s[None]
    kv_s = proj_s3[:, :, COL_KV:]
    new_k_sample = jnp.concatenate([ck, kv_s[:, :, :KV_W]], axis=1)[:, ts:].reshape(
        1, bs, win, KV_W // HEAD_DIM, HEAD_DIM)
    new_v_sample = jnp.concatenate([cv, kv_s[:, :, KV_W:]], axis=1)[:, ts:].reshape(
        1, bs, win, KV_W // HEAD_DIM, HEAD_DIM)

    return (y_prompt, y_sample, new_shift_prompt, new_wkv_prompt, new_k_prompt, new_v_prompt,
            new_shift_sample, new_wkv_sample, new_k_sample, new_v_sample)
```

```python
import functools

import numpy as np
import jax
import jax.numpy as jnp
from jax import lax
from jax.experimental import pallas as pl
from jax.experimental.pallas import tpu as pltpu

F32, BF16 = jnp.float32, jnp.bfloat16

D_MODEL = 2048
HEAD_DIM = 64
HEADS_PER_GROUP = 2
GROUP_W = HEADS_PER_GROUP * HEAD_DIM
N_GROUPS = D_MODEL // GROUP_W
RKV_W = 3 * D_MODEL
LORA_W = 96 + 96 + 256
LORA_PAD = 512
SHIFT_W = RKV_W + LORA_W
KV_W = 256
GN_EPS = 64e-5
LN_EPS = 1e-5
WINDOW = 128
ATT_SCALE = HEAD_DIM ** -0.5
N_KEYS = 128
N_EXPERTS = N_KEYS * N_KEYS
PEER_HEADS = 8
PEER_TOPK = 16
DN_ALPHA = 2.0 ** 0.25

COL_LORA = RKV_W
PROJ_A_W = RKV_W + LORA_PAD
COL_Q = 0
COL_GATE = D_MODEL
COL_KV = 3 * D_MODEL
PROJ_B_W = COL_KV + 2 * KV_W

VMEM_LIMIT = 56 * 1024 * 1024

_NN = (((1,), (0,)), ((), ()))
_NT = (((1,), (1,)), ((), ()))
_TN = (((0,), (0,)), ((), ()))


def _bdot(a, b, dn=_NN):
    return lax.dot_general(a, b, dn, preferred_element_type=F32)


def _dot(a, b, dn=_NN):
    return _bdot(a.astype(BF16), b.astype(BF16), dn)


def _split(x, n):
    parts, r = [], x
    for i in range(n):
        p = r.astype(BF16)
        parts.append(p)
        if i + 1 < n:
            r = r - p.astype(F32)
    return parts


def _dot3(a, b, dn=_NN):
    a1, a2 = _split(a, 2)
    b1, b2 = _split(b, 2)
    return (_bdot(a1, b2, dn) + _bdot(a2, b1, dn)) + _bdot(a1, b1, dn)


def _dot_xl(a_bf16, b, dn=_NN, passes=3):
    out = None
    for part in reversed(_split(b, passes)):
        t = _bdot(a_bf16, part, dn)
        out = t if out is None else out + t
    return out


def _dot_xr(a, b_bf16, dn=_NN, passes=3):
    out = None
    for part in reversed(_split(a, passes)):
        t = _bdot(part, b_bf16, dn)
        out = t if out is None else out + t
    return out


def _mm(passes):
    return _dot if passes == 1 else _dot3


def _sigmoid(x):
    return 1.0 / (1.0 + jnp.exp(-x))


def _iota(shape, dim):
    return lax.broadcasted_iota(jnp.int32, shape, dim)


def _head_of(idx):
    return jnp.right_shift(idx, 6)


def _ones_where(mask):
    return jnp.where(mask, 1.0, 0.0).astype(BF16)


def _params(sem):
    return pltpu.CompilerParams(dimension_semantics=sem, vmem_limit_bytes=VMEM_LIMIT)


def _inproj_body(x_ref, w_ref, o_ref, xb_ref):
    @pl.when(pl.program_id(1) == 0)
    def _():
        xb_ref[...] = x_ref[...].astype(BF16)

    o_ref[...] = _bdot(xb_ref[...], w_ref[...].astype(BF16), _NT)


def _inproj(x2d, w_t, n, tn=512):
    m = x2d.shape[0]
    tm = min(m, 1024)
    return pl.pallas_call(
        _inproj_body,
        grid=(m // tm, n // tn),
        in_specs=[pl.BlockSpec((tm, D_MODEL), lambda i, j: (i, 0)),
                  pl.BlockSpec((tn, D_MODEL), lambda i, j: (j, 0))],
        out_specs=pl.BlockSpec((tm, tn), lambda i, j: (i, j)),
        out_shape=jax.ShapeDtypeStruct((m, n), F32),
        scratch_shapes=[pltpu.VMEM((tm, D_MODEL), BF16)],
        compiler_params=_params(("parallel", "arbitrary")),
        name="inproj",
    )(x2d, w_t)


RWKV_PASSES = dict(seg=2, cum=3, lblk=1, inv=1, apply=1, state=1, gn=2)
ROWS = 64


def _rwkv_body(C, nb, ncs, nsteps, gpb, ps,
               r_ref, k_ref, v_ref, lo_ref, pr_ref, pk_ref, pv_ref, plo_ref, s0_ref,
               mur_ref, muk_ref, muv_ref, mulo_ref, w0_ref, a0_ref, kk_ref, ka_ref, rk_ref, gng_ref, gnb_ref,
               wup_ref, aup_ref, gup_ref,
               o_ref, sout_ref,
               sbd_ref, sh_ref, shl_ref):
    assert HEADS_PER_GROUP == 2 and HEADS_PER_GROUP * ROWS == GROUP_W and (ncs == 1 or nb == 1)
    c = pl.program_id(2)
    w = gpb * GROUP_W
    srows = ncs * ROWS
    log2c = C.bit_length() - 1
    st = HEADS_PER_GROUP * ROWS
    sq = (st, st)
    ri, ci = _iota(sq, 0), _iota(sq, 1)
    same_head = _head_of(ri) == _head_of(ci)
    tr, tc = ri & (ROWS - 1), ci & (ROWS - 1)
    blk = same_head & (jnp.right_shift(tr, log2c) == jnp.right_shift(tc, log2c))
    strict = blk & (tr > tc)
    incl = blk & (tr >= tc)
    eye = jnp.where(ri == ci, 1.0, 0.0)
    block_ones = _ones_where(same_head)
    r64, c64 = _iota((ROWS, ROWS), 0), _iota((ROWS, ROWS), 1)
    tri_seq = _ones_where((jnp.right_shift(r64, log2c) == jnp.right_shift(c64, log2c)) & (r64 >= c64))
    lane_head = _head_of(_iota((ROWS, GROUP_W), 1))
    rep = _ones_where((_iota((HEAD_DIM, GROUP_W), 1) & (HEAD_DIM - 1)) == _iota((HEAD_DIM, GROUP_W), 0))
    rep_t = _ones_where((_iota((GROUP_W, HEAD_DIM), 0) & (HEAD_DIM - 1)) == _iota((GROUP_W, HEAD_DIM), 1))
    mm_l, mm_i, mm_a, mm_s = _mm(ps["lblk"]), _mm(ps["inv"]), _mm(ps["apply"]), _mm(ps["state"])

    @pl.when(c == 0)
    def _init():
        if nb == 1:
            sh_ref[7:8, 0:w] = pr_ref[0:1, :]
            sh_ref[7:8, w:2 * w] = pk_ref[0:1, :]
            sh_ref[7:8, 2 * w:3 * w] = pv_ref[0:1, :]
            shl_ref[7:8, :] = plo_ref[0:1, :]
        for b in range(nb):
            for gi in range(gpb):
                s0 = s0_ref[b, gi * HEADS_PER_GROUP:(gi + 1) * HEADS_PER_GROUP].reshape(GROUP_W, HEAD_DIM)
                sbd_ref[b, gi] = jnp.where(same_head, _dot_xr(s0, rep), 0.0)

    def lerp(x, ref, lo, hi, prev_ref, mu):
        ref[pl.ds(8, srows), lo:hi] = x
        xs = ref[pl.ds(7, srows), lo:hi]
        ref[7:8, lo:hi] = x[srows - 1:srows, :]
        if nb > 1:
            first = (_iota(x.shape, 0) & (C - 1)) == 0
            xs = jnp.where(first, prev_ref[...], xs)
        return x + (xs - x) * mu

    r = lerp(r_ref[...], sh_ref, 0, w, pr_ref, mur_ref[...])
    k = lerp(k_ref[...], sh_ref, w, 2 * w, pk_ref, muk_ref[...])
    v = lerp(v_ref[...], sh_ref, 2 * w, 3 * w, pv_ref, muv_ref[...])
    lo = lerp(lo_ref[...], shl_ref, 0, LORA_PAD, plo_ref, mulo_ref[...])

    wl = w0_ref[...] + _dot(jnp.tanh(lo), wup_ref[...])
    w_log = -(jnp.maximum(-wl, 0.0) + jnp.log1p(jnp.exp(-jnp.abs(wl)))) - 0.5
    logw_all = -jnp.exp(w_log)
    a_all = _sigmoid(a0_ref[...] + _dot(lo, aup_ref[...]))
    gate_all = _dot(_sigmoid(lo), gup_ref[...])
    kk_all = k * kk_ref[...]
    k2_all = k * (1.0 + (a_all - 1.0) * ka_ref[...])

    def stack(x):
        return jnp.concatenate([jnp.where(lane_head == j, x, 0.0) for j in range(HEADS_PER_GROUP)], axis=0)

    def fold(x):
        return x[0:ROWS] + x[ROWS:2 * ROWS]

    gs = range(ncs * gpb)
    sls = [slice(gi * GROUP_W, (gi + 1) * GROUP_W) for _ in range(ncs) for gi in range(gpb)]
    rws = [slice(ck * ROWS, (ck + 1) * ROWS) for ck in range(ncs) for _ in range(gpb)]
    rg = [r[rw, sl] for rw, sl in zip(rws, sls)]
    vg = [v[rw, sl] for rw, sl in zip(rws, sls)]
    k2 = [k2_all[rw, sl] for rw, sl in zip(rws, sls)]
    kk = [kk_all[rw, sl] for rw, sl in zip(rws, sls)]
    ss = [_dot_xr(x * x, block_ones, passes=ps["seg"]) for x in kk]
    kkn = [x / jnp.maximum(jnp.sqrt(s2), 1e-12) for x, s2 in zip(kk, ss)]
    logw = [logw_all[rw, sl] for rw, sl in zip(rws, sls)]
    cum = [_dot_xl(tri_seq, x, passes=ps["cum"]) for x in logw]
    gam = [jnp.exp(x) for x in cum]
    ginv = [jnp.exp(-x) for x in cum]
    a4 = [-(jnp.exp(cm - lw) * kn) for cm, lw, kn in zip(cum, logw, kkn)]
    b4 = [kn * a_all[rw, sl] * gi_ for kn, rw, sl, gi_ in zip(kkn, rws, sls, ginv)]
    k4 = [x * gi_ for x, gi_ in zip(k2, ginv)]
    r4 = [gm * x for gm, x in zip(gam, rg)]

    single = all(ps[c_] == 1 for c_ in ("lblk", "inv", "apply", "state"))
    nar = (lambda x: x.astype(BF16)) if single else (lambda x: x)
    a_st = [nar(stack(x)) for x in a4]
    r_st = [nar(stack(x)) for x in r4]
    v_st = [nar(stack(x)) for x in vg]
    prod = [mm_l(jnp.concatenate([x, y_], axis=0), jnp.concatenate([z, q_], axis=0), _NT)
            for x, y_, z, q_ in zip(a_st, r_st, b4, k4)]
    low = _iota(sq, 1) < HEAD_DIM

    def diag_blocks(x, mask):
        xr = pltpu.roll(x, HEAD_DIM, axis=1)
        return jnp.where(mask, jnp.where(low, x, xr), 0.0), jnp.where(mask, jnp.where(low, xr, x), 0.0)

    l_bk = [diag_blocks(x[:st], strict) for x in prod]
    m_bk = [diag_blocks(x[st:], incl) for x in prod]
    l_b, l_k = [x[0] for x in l_bk], [nar(x[1]) for x in l_bk]
    m_b, m_k = [nar(x[0]) for x in m_bk], [nar(x[1]) for x in m_bk]
    t = [nar(eye + x) for x in l_b]
    l_bn = [nar(x) for x in l_b]
    pw = [nar(mm_i(x, x)) for x in l_bn]
    cov = 2
    while cov < C:
        if 2 * cov >= C:
            t = [nar(x + mm_i(p_, x)) for x, p_ in zip(t, pw)]
        else:
            both = [mm_i(p_, jnp.concatenate([p_, x], axis=1)) for x, p_ in zip(t, pw)]
            pw = [nar(x[:, :st]) for x in both]
            t = [nar(x + y_[:, st:]) for x, y_ in zip(t, both)]
        cov *= 2
    lm_v = [mm_a(jnp.concatenate([x, y_], axis=0), z) for x, y_, z in zip(l_k, m_k, v_st)]
    mkv4 = [fold(x[st:]) for x in lm_v]
    t_av = [mm_a(x, jnp.concatenate([y_, nar(z[:st])], axis=1)) for x, y_, z in zip(t, a_st, lm_v)]
    wm4 = [fold(x[:, :st]) for x in t_av]
    uk4 = [fold(x[:, st:]) for x in t_av]

    us = [[None] * nb for _ in gs]
    ys = [[None] * nb for _ in gs]
    for ck in range(ncs):
        cgs = range(ck * gpb, (ck + 1) * gpb)
        for b in range(nb):
            rows = slice(b * C, (b + 1) * C)
            s_old = {gi: sbd_ref[b, gi - ck * gpb] for gi in cgs}
            uy = {gi: mm_s(jnp.concatenate([wm4[gi][rows], r4[gi][rows]], axis=0), s_old[gi], _NT) for gi in cgs}
            for gi in cgs:
                us[gi][b] = uy[gi][:C] + uk4[gi][rows]
                ys[gi][b] = uy[gi][C:]
            ds = {gi: mm_s(jnp.concatenate([us[gi][b], vg[gi][rows]], axis=0),
                           jnp.concatenate([b4[gi][rows], k4[gi][rows]], axis=0), _TN) for gi in cgs}
            for gi in cgs:
                s_new = (s_old[gi] + jnp.where(same_head, ds[gi], 0.0)) * gam[gi][(b + 1) * C - 1:(b + 1) * C, :]
                sbd_ref[b, gi - ck * gpb] = s_new
                if ck == ncs - 1:
                    @pl.when(c == nsteps - 1)
                    def _fin():
                        hs = slice((gi - ck * gpb) * HEADS_PER_GROUP, (gi - ck * gpb + 1) * HEADS_PER_GROUP)
                        sout_ref[b, hs] = _dot_xr(s_new, rep_t).reshape(HEADS_PER_GROUP, HEAD_DIM, HEAD_DIM)

    u = [x[0] if nb == 1 else jnp.concatenate(x, axis=0) for x in us]
    ysum = [x[0] if nb == 1 else jnp.concatenate(x, axis=0) for x in ys]
    yb = [fold(mm_s(x, stack(y_))) for x, y_ in zip(m_b, u)]
    y = [x + y_ + z for x, y_, z in zip(ysum, mkv4, yb)]

    ym = [_dot_xr(x, block_ones, passes=ps["gn"]) * (1.0 / HEAD_DIM) for x in y]
    yc = [x - y_ for x, y_ in zip(y, ym)]
    yv = [_dot_xr(x * x, block_ones, passes=ps["gn"]) * (1.0 / HEAD_DIM) for x in yc]
    bonus = [_dot_xr(rg[gi] * k2[gi] * rk_ref[:, sls[gi]], block_ones, passes=ps["gn"]) * vg[gi] for gi in gs]
    for gi in gs:
        sl, rw = sls[gi], rws[gi]
        yn = yc[gi] * lax.rsqrt(yv[gi] + GN_EPS) * gng_ref[:, sl] + gnb_ref[:, sl]
        o_ref[rw, sl] = (yn + bonus[gi]) * gate_all[rw, sl]


def _rwkv(proj, nseq, nchunk, C, prev_shift, state0, prm, gpb=None, ncs=None):
    nb = ROWS // C
    if gpb is None:
        gpb = N_GROUPS if nb == 1 else N_GROUPS // 2
    if ncs is None:
        ncs = 2 if nb == 1 and nchunk % 2 == 0 else 1
    assert nb * C == ROWS and nseq % nb == 0 and (nb == 1 or nchunk == 1) and nchunk % ncs == 0
    rows = nseq * nchunk * C
    nsteps = nchunk // ncs
    srows = ncs * ROWS
    w = gpb * GROUP_W
    sec = D_MODEL // w
    prows = 8 if nb == 1 else C
    pblk = 8 if nb == 1 else srows
    prev = jnp.zeros((nseq, prows, SHIFT_W), F32).at[:, 0].set(prev_shift).reshape(nseq * prows, SHIFT_W)
    prev_rkv = prev[:, :RKV_W]
    prev_lora = jnp.pad(prev[:, RKV_W:], ((0, 0), (0, LORA_PAD - LORA_W)))

    def tok(off):
        return pl.BlockSpec((srows, w), lambda s, g, c: (s * nsteps + c, off + g))

    def prv(off):
        return pl.BlockSpec((pblk, w), lambda s, g, c: (s, off + g))

    def vec(off=0):
        return pl.BlockSpec((1, w), lambda s, g, c: (0, off + g))

    up = pl.BlockSpec((LORA_PAD, w), lambda s, g, c: (0, g))
    in_specs = [
        tok(0), tok(sec), tok(2 * sec),
        pl.BlockSpec((srows, LORA_PAD), lambda s, g, c: (s * nsteps + c, COL_LORA // LORA_PAD)),
        prv(0), prv(sec), prv(2 * sec),
        pl.BlockSpec((pblk, LORA_PAD), lambda s, g, c: (s, 0)),
        pl.BlockSpec((nb, gpb * HEADS_PER_GROUP, HEAD_DIM, HEAD_DIM), lambda s, g, c: (s, g, 0, 0)),
        vec(0), vec(sec), vec(2 * sec),
        pl.BlockSpec((1, LORA_PAD), lambda s, g, c: (0, 0)),
        vec(), vec(), vec(), vec(), vec(), vec(), vec(),
        up, up, up,
    ]
    out_specs = [pl.BlockSpec((srows, w), lambda s, g, c: (s * nsteps + c, g)),
                 pl.BlockSpec((nb, gpb * HEADS_PER_GROUP, HEAD_DIM, HEAD_DIM), lambda s, g, c: (s, g, 0, 0))]
    return pl.pallas_call(
        functools.partial(_rwkv_body, C, nb, ncs, nsteps, gpb, RWKV_PASSES),
        grid=(nseq // nb, N_GROUPS // gpb, nsteps),
        in_specs=in_specs,
        out_specs=out_specs,
        out_shape=[jax.ShapeDtypeStruct((rows, D_MODEL), F32),
                   jax.ShapeDtypeStruct((nseq, D_MODEL // HEAD_DIM, HEAD_DIM, HEAD_DIM), F32)],
        scratch_shapes=[pltpu.VMEM((nb, gpb, GROUP_W, GROUP_W), F32),
                        pltpu.VMEM((srows + 8, 3 * w), F32),
                        pltpu.VMEM((srows + 8, LORA_PAD), F32)],
        compiler_params=_params(("arbitrary", "arbitrary", "arbitrary")),
        name="rwkv",
    )(proj, proj, proj, proj, prev_rkv, prev_rkv, prev_rkv, prev_lora, state0,
      prm["mu_rkv"], prm["mu_rkv"], prm["mu_rkv"], prm["mu_lora"],
      prm["w0"], prm["a0"], prm["k_k"], prm["k_a"], prm["r_k"], prm["gn_g"], prm["gn_b"],
      prm["w_up"], prm["a_up"], prm["g_up"])


def _attn_rows(q, kcat, vcat, sink_ref, first_key, store, npr):
    m = q.shape[0]
    kt_all = kcat.T
    qi = _iota((npr * m, 2 * WINDOW), 0) & (m - 1)
    ki = _iota((npr * m, 2 * WINDOW), 1)
    allowed = (ki >= qi + 1) & (ki <= qi + WINDOW) & (ki >= first_key)
    lane = _iota((2 * WINDOW, 128), 1)
    ones_k = jnp.ones((2 * WINDOW, 128), BF16)
    zk = jnp.zeros((HEAD_DIM, 2 * WINDOW), F32)
    nkv = KV_W // HEAD_DIM
    k2s, v2s = [], []
    for kv in range(nkv):
        kt = kt_all[kv * HEAD_DIM:(kv + 1) * HEAD_DIM, :]
        k2s.append(jnp.concatenate([jnp.concatenate([kt, zk], axis=1),
                                    jnp.concatenate([zk, kt], axis=1)], axis=0).astype(BF16))
        slab = vcat[:, (kv // 2) * 128:(kv // 2 + 1) * 128]
        rolled = pltpu.roll(slab, HEAD_DIM, axis=1)
        lo_src, hi_src = (slab, rolled) if kv % 2 == 0 else (rolled, slab)
        v2s.append(jnp.concatenate([jnp.where(lane < HEAD_DIM, lo_src, 0.0),
                                    jnp.where(lane >= HEAD_DIM, hi_src, 0.0)], axis=0).astype(BF16))
    items = [(kv, p0) for kv in range(nkv) for p0 in range(0, 4, npr)]
    cols = [[kv * 512 + (p0 + i) * 128 for i in range(npr)] for kv, p0 in items]
    scores = [_bdot((jnp.concatenate([q[:, col:col + 128] for col in cs], axis=0) * ATT_SCALE).astype(BF16),
                    k2s[kv]) for (kv, _), cs in zip(items, cols)]
    probs = []
    for (kv, p0), s in zip(items, scores):
        ps = []
        for hb in range(2):
            sink = jnp.concatenate(
                [jnp.broadcast_to(sink_ref[:, (kv * 8 + (p0 + i) * 2 + hb) * 128:
                                           (kv * 8 + (p0 + i) * 2 + hb) * 128 + 1], (m, 1))
                 for i in range(npr)], axis=0)
            sh = jnp.where(allowed, s[:, hb * 256:(hb + 1) * 256], -jnp.inf)
            mx = jnp.maximum(jnp.max(sh, axis=1, keepdims=True), sink)
            e = jnp.exp(sh - mx)
            inv = 1.0 / (_dot_xr(e, ones_k, passes=2) + jnp.exp(sink - mx))
            ps += [e[:, :128] * inv, e[:, 128:] * inv]
        probs.append(jnp.concatenate(ps, axis=1).astype(BF16))
    outs = [_bdot(p, v2s[kv]) for (kv, _), p in zip(items, probs)]
    for cs, o in zip(cols, outs):
        for i, col in enumerate(cs):
            store(col, o[i * m:(i + 1) * m])


def _attn_prompt_body(q_ref, kvc_ref, kvp_ref, sink_ref, o_ref):
    n = pl.program_id(0)
    kvc = kvc_ref[...]
    kvp = kvp_ref[...]
    kcat = jnp.concatenate([kvp[:, :KV_W], kvc[:, :KV_W]], axis=0)
    vcat = jnp.concatenate([kvp[:, KV_W:], kvc[:, KV_W:]], axis=0)

    def store(col, val):
        o_ref[:, col:col + 128] = val

    _attn_rows(q_ref[...], kcat, vcat, sink_ref, jnp.where(n > 0, 0, WINDOW), store, npr=1)


def _attn_prompt(proj, sinks_e):
    t = proj.shape[0]
    nb = t // WINDOW
    qb, kvb = COL_Q // D_MODEL, COL_KV // (2 * KV_W)
    return pl.pallas_call(
        _attn_prompt_body,
        grid=(nb,),
        in_specs=[pl.BlockSpec((WINDOW, D_MODEL), lambda n: (n, qb)),
                  pl.BlockSpec((WINDOW, 2 * KV_W), lambda n: (n, kvb)),
                  pl.BlockSpec((WINDOW, 2 * KV_W), lambda n: (jnp.maximum(n - 1, 0), kvb)),
                  pl.BlockSpec((1, 32 * 128), lambda n: (0, 0))],
        out_specs=pl.BlockSpec((WINDOW, D_MODEL), lambda n: (n, 0)),
        out_shape=jax.ShapeDtypeStruct((t, D_MODEL), F32),
        compiler_params=_params(("parallel",)),
        name="attn_prompt",
    )(proj, proj, proj, sinks_e)


def _attn_sample_body(L, bt, q_ref, kvn_ref, ck_ref, cv_ref, sink_ref, o_ref):
    pad = jnp.zeros((WINDOW - L, KV_W), F32)

    def one(b, carry):
        rows = pl.ds(pl.multiple_of(b * L, L), L)
        kvn = kvn_ref[rows, :]
        kcat = jnp.concatenate([ck_ref[b], kvn[:, :KV_W], pad], axis=0)
        vcat = jnp.concatenate([cv_ref[b], kvn[:, KV_W:], pad], axis=0)

        def store(col, val):
            o_ref[rows, col:col + 128] = val

        _attn_rows(q_ref[rows, :], kcat, vcat, sink_ref, 0, store, npr=4)
        return carry

    lax.fori_loop(0, bt, one, 0, unroll=2)


def _attn_sample(proj, ck, cv, sinks_e, L, bt=16):
    rows = proj.shape[0]
    nb = rows // L
    bt = min(bt, nb)
    qb, kvb = COL_Q // D_MODEL, COL_KV // (2 * KV_W)
    return pl.pallas_call(
        functools.partial(_attn_sample_body, L, bt),
        grid=(nb // bt,),
        in_specs=[pl.BlockSpec((bt * L, D_MODEL), lambda i: (i, qb)),
                  pl.BlockSpec((bt * L, 2 * KV_W), lambda i: (i, kvb)),
                  pl.BlockSpec((bt, WINDOW, KV_W), lambda i: (i, 0, 0)),
                  pl.BlockSpec((bt, WINDOW, KV_W), lambda i: (i, 0, 0)),
                  pl.BlockSpec((1, 32 * 128), lambda i: (0, 0))],
        out_specs=pl.BlockSpec((bt * L, D_MODEL), lambda i: (i, 0)),
        out_shape=jax.ShapeDtypeStruct((rows, D_MODEL), F32),
        compiler_params=_params(("parallel",)),
        name="attn_sample",
    )(proj, proj, ck, cv, sinks_e)


def _layer_norm(x, g, b):
    mu = jnp.mean(x, axis=-1, keepdims=True)
    xc = x - mu
    var = jnp.mean(xc * xc, axis=-1, keepdims=True)
    return xc * lax.rsqrt(var + LN_EPS) * g + b


def _merge_body(ga_ref, gb_ref, oa_ref, ob_ref, x_ref, wout_ref, g_ref, b_ref, h_ref, ht_ref):
    mixed = _sigmoid(ga_ref[...]) * oa_ref[...] + _sigmoid(gb_ref[...]) * ob_ref[...]
    y = _bdot(mixed.astype(BF16), wout_ref[...])
    h = _layer_norm(DN_ALPHA * x_ref[...] + y, g_ref[...], b_ref[...])
    h_ref[...] = h
    ht_ref[...] = h.T.astype(BF16)


def _merge(proj, o_a, o_b, x2d, wout_bf16, ln_g, ln_b, tm=256):
    m = x2d.shape[0]
    tm = min(tm, m)
    gab = COL_GATE // D_MODEL
    row = lambda i: (i, 0)
    return pl.pallas_call(
        _merge_body,
        grid=(m // tm,),
        in_specs=[pl.BlockSpec((tm, D_MODEL), lambda i: (i, gab)),
                  pl.BlockSpec((tm, D_MODEL), lambda i: (i, gab + 1)),
                  pl.BlockSpec((tm, D_MODEL), row),
                  pl.BlockSpec((tm, D_MODEL), row),
                  pl.BlockSpec((tm, D_MODEL), row),
                  pl.BlockSpec((D_MODEL, D_MODEL), lambda i: (0, 0)),
                  pl.BlockSpec((1, D_MODEL), lambda i: (0, 0)),
                  pl.BlockSpec((1, D_MODEL), lambda i: (0, 0))],
        out_specs=[pl.BlockSpec((tm, D_MODEL), row),
                   pl.BlockSpec((D_MODEL, tm), lambda i: (0, i))],
        out_shape=[jax.ShapeDtypeStruct((m, D_MODEL), F32),
                   jax.ShapeDtypeStruct((D_MODEL, m), BF16)],
        compiler_params=_params(("parallel",)),
        name="merge",
    )(proj, proj, o_a, o_b, x2d, wout_bf16, ln_g, ln_b)


def _top_rows(x, count):
    return _top_rows_multi([x], count)[0]


def _top_rows_multi(xs, count):
    rows = _iota(xs[0].shape, 0)
    big = xs[0].shape[0]
    outs = [[] for _ in xs]
    for _ in range(count):
        mxs = [jnp.max(x, axis=0, keepdims=True) for x in xs]
        for o, mx in zip(outs, mxs):
            o.append(mx)
        firsts = [jnp.min(jnp.where(x == mx, rows, big), axis=0, keepdims=True) for x, mx in zip(xs, mxs)]
        xs = [jnp.where(rows == f, -jnp.inf, x) for x, f in zip(xs, firsts)]
    return outs


def _bitonic_network(n):
    comps = []
    k = 2
    while k <= n:
        j = k // 2
        while j >= 1:
            comps += [(i, i ^ j, (i & k) == 0) for i in range(n) if (i ^ j) > i]
            j //= 2
        k *= 2
    return comps


def _top16_of_128(xs):
    nt = N_KEYS // 8
    tiles = [[x[8 * j:8 * (j + 1)] for j in range(nt)] for x in xs]
    for i, l, desc in _bitonic_network(nt):
        for t in tiles:
            hi, lo = jnp.maximum(t[i], t[l]), jnp.minimum(t[i], t[l])
            t[i], t[l] = (hi, lo) if desc else (lo, hi)
    sub = _iota(tiles[0][0].shape, 0)
    outs = [[] for _ in xs]
    for r in range(PEER_TOPK):
        for t, o in zip(tiles, outs):
            mx = jnp.max(t[0], axis=0, keepdims=True)
            o.append(mx)
            first = jnp.min(jnp.where(t[0] == mx, sub, 8), axis=0, keepdims=True)
            pop = sub == first
            for j in range(PEER_TOPK - r - 1):
                t[j] = jnp.where(pop, t[j + 1], t[j])
    return outs


def _route_body(tm, h_ref, wq_ref, keys_ref, s0_ref, s1_ref, aux_ref, q_s):
    q = _bdot(h_ref[...].astype(BF16), wq_ref[...])
    for hc in range(2 * PEER_HEADS):
        q_s[hc] = q[:, hc * N_KEYS:(hc + 1) * N_KEYS]
    pairs = [(i, j) for i in range(PEER_TOPK) for j in range(PEER_TOPK) if (i + 1) * (j + 1) <= PEER_TOPK]
    npad = -len(pairs) % 8
    neg = jnp.full((1, tm), -jnp.inf, F32)
    zero = jnp.zeros((1, tm), F32)

    def head(hh, carry):
        s0 = _bdot(keys_ref[hh, 0], q_s[2 * hh].astype(BF16), _NT)
        s1 = _bdot(keys_ref[hh, 1], q_s[2 * hh + 1].astype(BF16), _NT)
        s0_ref[hh] = s0
        for lb in range(tm // 128):
            s1_ref[hh, lb] = s1[:, lb * 128:(lb + 1) * 128]
        top0, top1 = _top16_of_128([s0, s1])
        cand = jnp.concatenate([top0[i] + top1[j] for i, j in pairs] + [neg] * npad, axis=0)
        best = _top_rows(cand, PEER_TOPK)
        mx = best[0]
        z = zero
        for bsum in best:
            z = z + jnp.exp(bsum - mx)
        aux_ref[hh] = jnp.concatenate([best[-1], top0[0], top1[0], 1.0 / z, zero, zero, zero, zero], axis=0)
        return carry

    lax.fori_loop(0, PEER_HEADS, head, 0)


def _route(h, wq_bf16, keys_bf16, tm=256):
    m = h.shape[0]
    tm = min(tm, m)
    tok3 = lambda i: (0, 0, i)
    return pl.pallas_call(
        functools.partial(_route_body, tm),
        grid=(m // tm,),
        in_specs=[pl.BlockSpec((tm, D_MODEL), lambda i: (i, 0)),
                  pl.BlockSpec((D_MODEL, D_MODEL), lambda i: (0, 0)),
                  pl.BlockSpec((PEER_HEADS, 2, N_KEYS, N_KEYS), lambda i: (0, 0, 0, 0))],
        out_specs=[pl.BlockSpec((PEER_HEADS, N_KEYS, tm), tok3),
                   pl.BlockSpec((PEER_HEADS, tm // 128, N_KEYS, 128), lambda i: (0, i, 0, 0)),
                   pl.BlockSpec((PEER_HEADS, 8, tm), tok3)],
        out_shape=[jax.ShapeDtypeStruct((PEER_HEADS, N_KEYS, m), F32),
                   jax.ShapeDtypeStruct((PEER_HEADS, m // 128, N_KEYS, 128), F32),
                   jax.ShapeDtypeStruct((PEER_HEADS, 8, m), F32)],
        scratch_shapes=[pltpu.VMEM((2 * PEER_HEADS, tm, N_KEYS), F32)],
        compiler_params=_params(("parallel",)),
        name="peer_route",
    )(h, wq_bf16, keys_bf16)


def _peer_body(te, ne, ht_ref, u_ref, vt_ref, s0_ref, s1_ref, aux_ref, h_ref, g_ref, b_ref, y_ref,
               acc_ref, e1_ref, cf_ref, st0_ref, st1_ref, w0_ref, w1_ref):
    j = pl.program_id(1)
    tm = acc_ref.shape[1]

    @pl.when(j == 0)
    def _():
        acc_ref[...] = jnp.zeros_like(acc_ref)
        for ref in (st0_ref, st1_ref, w0_ref, w1_ref):
            ref[...] = jnp.zeros_like(ref)
        for hh in range(PEER_HEADS):
            aux = aux_ref[hh]
            for lb in range(tm // 128):
                e1_ref[hh, lb] = jnp.exp(s1_ref[hh, lb] - aux[2:3, lb * 128:(lb + 1) * 128])
            cf_ref[hh] = jnp.exp(s0_ref[hh] - aux[1:2, :]) * aux[3:4, :]

    def stages(st_new, st_old, w_new, w_old):
        jb = j - 1
        live = (jb >= 0) & (jb < ne)
        jb_c = jnp.clip(jb, 0, ne - 1)
        taus = [jnp.where(live, aux_ref[hh, 0:1, :], jnp.inf) for hh in range(PEER_HEADS)]
        ncc = te // N_KEYS
        s0rows = [[s0_ref[hh, pl.ds(jb_c * ncc + cc, 1), :] for hh in range(PEER_HEADS)] for cc in range(ncc)]
        cfrows = [[cf_ref[hh, pl.ds(jb_c * ncc + cc, 1), :] for hh in range(PEER_HEADS)] for cc in range(ncc)]
        tw = min(256, tm)
        mxu_pieces, gate_tiles = [], []
        for t0 in range(0, tm, tw):
            ts = slice(t0, t0 + tw)
            lbs = range(t0 // 128, (t0 + tw) // 128)

            def mix(mr, ts=ts, lbs=lbs):
                acc_ref[mr, ts] += _bdot(vt_ref[mr, :], jnp.concatenate([w_old[lb] for lb in lbs], axis=1))

            def score(er, t0=t0, ts=ts, lbs=lbs):
                st = _bdot(u_ref[er, :], ht_ref[:, ts])
                for lb in lbs:
                    st_new[lb, er, :] = st[:, lb * 128 - t0:(lb + 1) * 128 - t0]

            for q in range(4):
                mxu_pieces.append(functools.partial(mix, slice(q * (D_MODEL // 4), (q + 1) * (D_MODEL // 4))))
            for q in range(2):
                mxu_pieces.append(functools.partial(score, slice(q * (te // 2), (q + 1) * (te // 2))))

            def gate(lb, cc):
                ls = slice(lb * 128, (lb + 1) * 128)
                rows = slice(cc * N_KEYS, (cc + 1) * N_KEYS)
                gsum = None
                for hh in range(PEER_HEADS):
                    sel = (s1_ref[hh, lb] + s0rows[cc][hh][:, ls]) >= taus[hh][:, ls]
                    term = jnp.where(sel, e1_ref[hh, lb] * cfrows[cc][hh][:, ls], 0.0)
                    gsum = term if gsum is None else gsum + term
                so = st_old[lb, rows, :]
                act = 0.5 * so * (1.0 + lax.erf(so * np.float32(np.sqrt(0.5))))
                w_new[lb, rows, :] = (gsum * act).astype(BF16)

            for lb in lbs:
                for cc in range(ncc):
                    gate_tiles.append(functools.partial(gate, lb, cc))
        for i in range(max(len(mxu_pieces), len(gate_tiles))):
            if i < len(gate_tiles):
                gate_tiles[i]()
            if i < len(mxu_pieces):
                mxu_pieces[i]()

    @pl.when(j % 2 == 0)
    def _():
        stages(st0_ref, st1_ref, w1_ref, w0_ref)

    @pl.when(j % 2 == 1)
    def _():
        stages(st1_ref, st0_ref, w0_ref, w1_ref)

    @pl.when(j == ne + 1)
    def _():
        out = acc_ref[...].T
        y_ref[...] = _layer_norm(DN_ALPHA * h_ref[...] + out, g_ref[...], b_ref[...])


def _peer(ht, u_bf16, vt_bf16, s0t, s1t, aux, h, ln_g, ln_b, tm=512, te=512):
    m = h.shape[0]
    tm = min(tm, m)
    ne = N_EXPERTS // te
    tok3 = lambda i, j: (0, 0, i)
    return pl.pallas_call(
        functools.partial(_peer_body, te, ne),
        grid=(m // tm, ne + 2),
        in_specs=[pl.BlockSpec((D_MODEL, tm), lambda i, j: (0, i)),
                  pl.BlockSpec((te, D_MODEL), lambda i, j: (jnp.minimum(j, ne - 1), 0)),
                  pl.BlockSpec((D_MODEL, te), lambda i, j: (0, jnp.clip(j - 2, 0, ne - 1))),
                  pl.BlockSpec((PEER_HEADS, N_KEYS, tm), tok3),
                  pl.BlockSpec((PEER_HEADS, tm // 128, N_KEYS, 128), lambda i, j: (0, i, 0, 0)),
                  pl.BlockSpec((PEER_HEADS, 8, tm), tok3),
                  pl.BlockSpec((tm, D_MODEL), lambda i, j: (i, 0)),
                  pl.BlockSpec((1, D_MODEL), lambda i, j: (0, 0)),
                  pl.BlockSpec((1, D_MODEL), lambda i, j: (0, 0))],
        out_specs=pl.BlockSpec((tm, D_MODEL), lambda i, j: (i, 0)),
        out_shape=jax.ShapeDtypeStruct((m, D_MODEL), F32),
        scratch_shapes=[pltpu.VMEM((D_MODEL, tm), F32),
                        pltpu.VMEM((PEER_HEADS, tm // 128, N_KEYS, 128), F32),
                        pltpu.VMEM((PEER_HEADS, N_KEYS, tm), F32),
                        pltpu.VMEM((tm // 128, te, 128), F32),
                        pltpu.VMEM((tm // 128, te, 128), F32),
                        pltpu.VMEM((tm // 128, te, 128), BF16),
                        pltpu.VMEM((tm // 128, te, 128), BF16)],
        compiler_params=_params(("parallel", "arbitrary")),
        name="peer_dense",
    )(ht, u_bf16, vt_bf16, s0t, s1t, aux, h, ln_g, ln_b)


def _pad_rows(w, lo, total):
    return jnp.zeros((total, w.shape[1]), w.dtype).at[lo:lo + w.shape[0]].set(w)


def _layer(x2d, nseq, seq_len, chunk, prev_shift, state0, cache, wts):
    proj_a = _inproj(x2d, wts["w_in_t"], PROJ_A_W)
    proj_b = _inproj(x2d, wts["w_in_b"], PROJ_B_W)
    o_a, s_last = _rwkv(proj_a, nseq, seq_len // chunk, chunk, prev_shift, state0, wts)
    if cache is None:
        o_b = _attn_prompt(proj_b, wts["sinks"])
    else:
        o_b = _attn_sample(proj_b, cache[0], cache[1], wts["sinks"], seq_len)
    h, ht = _merge(proj_b, o_a, o_b, x2d, wts["w_out"], wts["ln1_g"], wts["ln1_b"])
    s0t, s1t, aux = _route(h, wts["peer_wq"], wts["peer_keys"])
    y = _peer(ht, wts["peer_u"], wts["peer_vt"], s0t, s1t, aux, h, wts["ln2_g"], wts["ln2_b"])
    return y, proj_a, proj_b, s_last


def kernel(x_prompt, x_sample, state_shift, state_wkv, cache_k, cache_v, w_in, rw_mu, rw_w0, rw_w_up, rw_a0,
           rw_a_up, rw_g_up, rw_k_k, rw_k_a, rw_r_k, rw_gn_g, rw_gn_b, att_sinks, w_out, ln1_g, ln1_b,
           peer_wq, peer_keys, peer_u, peer_v, ln2_g, ln2_b):
    depth = w_in.shape[0]
    assert depth == 1
    l = 0
    bp, tp, _ = x_prompt.shape
    bs, ts, _ = x_sample.shape
    assert bp == 1
    win = cache_k.shape[2]
    assert win == WINDOW and tp % WINDOW == 0 and ts <= 8

    w_t = jnp.transpose(w_in[l])
    q0 = SHIFT_W
    k0 = q0 + D_MODEL
    g0 = k0 + 2 * KV_W
    w_b = jnp.concatenate([w_t[q0:q0 + D_MODEL], w_t[g0:g0 + 2 * D_MODEL], w_t[k0:k0 + 2 * KV_W]],
                          axis=0).astype(BF16)
    mu = rw_mu[l]
    row = lambda v: v.reshape(1, -1)
    wts = dict(
        w_in_t=w_t, w_in_b=w_b,
        mu_rkv=row(mu[:RKV_W]),
        mu_lora=row(jnp.pad(mu[RKV_W:], (0, LORA_PAD - LORA_W))),
        w0=row(rw_w0[l]), a0=row(rw_a0[l]), k_k=row(rw_k_k[l]), k_a=row(rw_k_a[l]), r_k=row(rw_r_k[l]),
        gn_g=row(rw_gn_g[l]), gn_b=row(rw_gn_b[l]),
        w_up=_pad_rows(rw_w_up[l], 0, LORA_PAD).astype(BF16),
        a_up=_pad_rows(rw_a_up[l], 96, LORA_PAD).astype(BF16),
        g_up=_pad_rows(rw_g_up[l], 192, LORA_PAD).astype(BF16),
        sinks=jnp.repeat(att_sinks[l], 128).reshape(1, 32 * 128),
        w_out=w_out[l].astype(BF16),
        ln1_g=row(ln1_g[l]), ln1_b=row(ln1_b[l]), ln2_g=row(ln2_g[l]), ln2_b=row(ln2_b[l]),
        peer_wq=peer_wq[l].astype(BF16),
        peer_keys=peer_keys[l].astype(BF16),
        peer_u=peer_u[l].astype(BF16),
        peer_vt=peer_v[l].T.astype(BF16),
    )

    chunk_p = 64
    y_p, pa_p, pb_p, s_p = _layer(x_prompt[0], 1, tp, chunk_p,
                                  jnp.zeros((1, SHIFT_W), F32),
                                  jnp.zeros((1, D_MODEL // HEAD_DIM, HEAD_DIM, HEAD_DIM), F32), None, wts)
    keep = min(WINDOW, tp)
    kv_p = pb_p[tp - keep:, COL_KV:]
    y_prompt = y_p[None]
    new_shift_prompt = pa_p[tp - 1, :SHIFT_W][None, None]
    new_wkv_prompt = s_p[None]
    new_k_prompt = kv_p[:, :KV_W].reshape(1, 1, keep, KV_W // HEAD_DIM, HEAD_DIM)
    new_v_prompt = kv_p[:, KV_W:].reshape(1, 1, keep, KV_W // HEAD_DIM, HEAD_DIM)

    ck = cache_k[l].reshape(bs, win, KV_W)
    cv = cache_v[l].reshape(bs, win, KV_W)
    y_s, pa_s, pb_s, s_s = _layer(x_sample.reshape(bs * ts, D_MODEL), bs, ts, ts,
                                  state_shift[l], state_wkv[l], (ck, cv), wts)
    y_sample = y_s.reshape(bs, ts, D_MODEL)
    new_shift_sample = pa_s.reshape(bs, ts, PROJ_A_W)[:, ts - 1, :SHIFT_W][None]
    new_wkv_sample = s_s[None]
    kv_s = pb_s.reshape(bs, ts, PROJ_B_W)[:, :, COL_KV:]
    new_k_sample = jnp.concatenate([ck, kv_s[:, :, :KV_W]], axis=1)[:, ts:].reshape(
        1, bs, win, KV_W // HEAD_DIM, HEAD_DIM)
    new_v_sample = jnp.concatenate([cv, kv_s[:, :, KV_W:]], axis=1)[:, ts:].reshape(
        1, bs, win, KV_W // HEAD_DIM, HEAD_DIM)

    return (y_prompt, y_sample, new_shift_prompt, new_wkv_prompt, new_k_prompt, new_v_prompt,
            new_shift_sample, new_wkv_sample, new_k_sample, new_v_sample)
```

```python
import functools

import numpy as np
import jax
import jax.numpy as jnp
from jax import lax
from jax.experimental import pallas as pl
from jax.experimental.pallas import tpu as pltpu

F32, BF16 = jnp.float32, jnp.bfloat16

D_MODEL = 2048
HEAD_DIM = 64
HEADS_PER_GROUP = 2
GROUP_W = HEADS_PER_GROUP * HEAD_DIM
N_GROUPS = D_MODEL // GROUP_W
RKV_W = 3 * D_MODEL
LORA_W = 96 + 96 + 256
LORA_PAD = 512
SHIFT_W = RKV_W + LORA_W
KV_W = 256
GN_EPS = 64e-5
LN_EPS = 1e-5
WINDOW = 128
ATT_SCALE = HEAD_DIM ** -0.5
N_KEYS = 128
N_EXPERTS = N_KEYS * N_KEYS
PEER_HEADS = 8
PEER_TOPK = 16
DN_ALPHA = 2.0 ** 0.25

COL_LORA = RKV_W
PROJ_A_W = RKV_W + LORA_PAD
COL_Q = 0
COL_GATE = D_MODEL
COL_KV = 3 * D_MODEL
PROJ_B_W = COL_KV + 2 * KV_W

VMEM_LIMIT = 56 * 1024 * 1024

_NN = (((1,), (0,)), ((), ()))
_NT = (((1,), (1,)), ((), ()))
_TN = (((0,), (0,)), ((), ()))


def _bdot(a, b, dn=_NN):
    return lax.dot_general(a, b, dn, preferred_element_type=F32)


def _dot(a, b, dn=_NN):
    return _bdot(a.astype(BF16), b.astype(BF16), dn)


def _split(x, n):
    parts, r = [], x
    for i in range(n):
        p = r.astype(BF16)
        parts.append(p)
        if i + 1 < n:
            r = r - p.astype(F32)
    return parts


def _dot3(a, b, dn=_NN):
    a1, a2 = _split(a, 2)
    b1, b2 = _split(b, 2)
    return (_bdot(a1, b2, dn) + _bdot(a2, b1, dn)) + _bdot(a1, b1, dn)


def _dot_xl(a_bf16, b, dn=_NN, passes=3):
    out = None
    for part in reversed(_split(b, passes)):
        t = _bdot(a_bf16, part, dn)
        out = t if out is None else out + t
    return out


def _dot_xr(a, b_bf16, dn=_NN, passes=3):
    out = None
    for part in reversed(_split(a, passes)):
        t = _bdot(part, b_bf16, dn)
        out = t if out is None else out + t
    return out


def _mm(passes):
    return _dot if passes == 1 else _dot3


def _sigmoid(x):
    return 1.0 / (1.0 + jnp.exp(-x))


def _iota(shape, dim):
    return lax.broadcasted_iota(jnp.int32, shape, dim)


def _head_of(idx):
    return jnp.right_shift(idx, 6)


def _ones_where(mask):
    return jnp.where(mask, 1.0, 0.0).astype(BF16)


def _params(sem):
    return pltpu.CompilerParams(dimension_semantics=sem, vmem_limit_bytes=VMEM_LIMIT)


def _inproj_body(x_ref, w_ref, o_ref, xb_ref):
    @pl.when(pl.program_id(1) == 0)
    def _():
        xb_ref[...] = x_ref[...].astype(BF16)

    o_ref[...] = _bdot(xb_ref[...], w_ref[...].astype(BF16), _NT)


def _inproj(x2d, w_t, n, tn=1664):
    assert n % tn == 0 and tn % 128 == 0
    m = x2d.shape[0]
    tm = min(m, 1024)
    return pl.pallas_call(
        _inproj_body,
        grid=(m // tm, n // tn),
        in_specs=[pl.BlockSpec((tm, D_MODEL), lambda i, j: (i, 0)),
                  pl.BlockSpec((tn, D_MODEL), lambda i, j: (j, 0))],
        out_specs=pl.BlockSpec((tm, tn), lambda i, j: (i, j)),
        out_shape=jax.ShapeDtypeStruct((m, n), F32),
        scratch_shapes=[pltpu.VMEM((tm, D_MODEL), BF16)],
        compiler_params=_params(("parallel", "arbitrary")),
        name="inproj",
    )(x2d, w_t)


RWKV_PASSES = dict(seg=2, cum=3, lblk=1, inv=1, apply=1, state=1, gn=2)
ROWS = 64


def _rwkv_body(C, nb, ncs, nsteps, gpb, ps,
               r_ref, k_ref, v_ref, lo_ref, pr_ref, pk_ref, pv_ref, plo_ref, s0_ref,
               mur_ref, muk_ref, muv_ref, mulo_ref, w0_ref, a0_ref, kk_ref, ka_ref, rk_ref, gng_ref, gnb_ref,
               wup_ref, aup_ref, gup_ref,
               o_ref, sout_ref,
               sbd_ref, sh_ref, shl_ref):
    assert HEADS_PER_GROUP == 2 and HEADS_PER_GROUP * ROWS == GROUP_W and (ncs == 1 or nb == 1)
    c = pl.program_id(2)
    w = gpb * GROUP_W
    srows = ncs * ROWS
    log2c = C.bit_length() - 1
    st = HEADS_PER_GROUP * ROWS
    sq = (st, st)
    ri, ci = _iota(sq, 0), _iota(sq, 1)
    same_head = _head_of(ri) == _head_of(ci)
    tr, tc = ri & (ROWS - 1), ci & (ROWS - 1)
    blk = same_head & (jnp.right_shift(tr, log2c) == jnp.right_shift(tc, log2c))
    strict = blk & (tr > tc)
    incl = blk & (tr >= tc)
    eye = jnp.where(ri == ci, 1.0, 0.0)
    block_ones = _ones_where(same_head)
    r64, c64 = _iota((ROWS, ROWS), 0), _iota((ROWS, ROWS), 1)
    tri_seq = _ones_where((jnp.right_shift(r64, log2c) == jnp.right_shift(c64, log2c)) & (r64 >= c64))
    lane_head = _head_of(_iota((ROWS, GROUP_W), 1))
    rep = _ones_where((_iota((HEAD_DIM, GROUP_W), 1) & (HEAD_DIM - 1)) == _iota((HEAD_DIM, GROUP_W), 0))
    rep_t = _ones_where((_iota((GROUP_W, HEAD_DIM), 0) & (HEAD_DIM - 1)) == _iota((GROUP_W, HEAD_DIM), 1))
    mm_l, mm_i, mm_a, mm_s = _mm(ps["lblk"]), _mm(ps["inv"]), _mm(ps["apply"]), _mm(ps["state"])

    @pl.when(c == 0)
    def _init():
        if nb == 1:
            sh_ref[7:8, 0:w] = pr_ref[0:1, :]
            sh_ref[7:8, w:2 * w] = pk_ref[0:1, :]
            sh_ref[7:8, 2 * w:3 * w] = pv_ref[0:1, :]
            shl_ref[7:8, :] = plo_ref[0:1, :]
        for b in range(nb):
            for gi in range(gpb):
                s0 = s0_ref[b, gi * HEADS_PER_GROUP:(gi + 1) * HEADS_PER_GROUP].reshape(GROUP_W, HEAD_DIM)
                sbd_ref[b, gi] = jnp.where(same_head, _dot_xr(s0, rep), 0.0)

    def lerp(x, ref, lo, hi, prev_ref, mu):
        ref[pl.ds(8, srows), lo:hi] = x
        xs = ref[pl.ds(7, srows), lo:hi]
        ref[7:8, lo:hi] = x[srows - 1:srows, :]
        if nb > 1:
            first = (_iota(x.shape, 0) & (C - 1)) == 0
            xs = jnp.where(first, prev_ref[...], xs)
        return x + (xs - x) * mu

    r = lerp(r_ref[...], sh_ref, 0, w, pr_ref, mur_ref[...])
    k = lerp(k_ref[...], sh_ref, w, 2 * w, pk_ref, muk_ref[...])
    v = lerp(v_ref[...], sh_ref, 2 * w, 3 * w, pv_ref, muv_ref[...])
    lo = lerp(lo_ref[...], shl_ref, 0, LORA_PAD, plo_ref, mulo_ref[...])

    wl = w0_ref[...] + _dot(jnp.tanh(lo), wup_ref[...])
    w_log = -(jnp.maximum(-wl, 0.0) + jnp.log1p(jnp.exp(-jnp.abs(wl)))) - 0.5
    logw_all = -jnp.exp(w_log)
    a_all = _sigmoid(a0_ref[...] + _dot(lo, aup_ref[...]))
    gate_all = _dot(_sigmoid(lo), gup_ref[...])
    kk_all = k * kk_ref[...]
    k2_all = k * (1.0 + (a_all - 1.0) * ka_ref[...])

    def stack(x):
        return jnp.concatenate([jnp.where(lane_head == j, x, 0.0) for j in range(HEADS_PER_GROUP)], axis=0)

    def fold(x):
        return x[0:ROWS] + x[ROWS:2 * ROWS]

    gs = range(ncs * gpb)
    sls = [slice(gi * GROUP_W, (gi + 1) * GROUP_W) for _ in range(ncs) for gi in range(gpb)]
    rws = [slice(ck * ROWS, (ck + 1) * ROWS) for ck in range(ncs) for _ in range(gpb)]
    rg = [r[rw, sl] for rw, sl in zip(rws, sls)]
    vg = [v[rw, sl] for rw, sl in zip(rws, sls)]
    k2 = [k2_all[rw, sl] for rw, sl in zip(rws, sls)]
    kk = [kk_all[rw, sl] for rw, sl in zip(rws, sls)]
    ss = [_dot_xr(x * x, block_ones, passes=ps["seg"]) for x in kk]
    kkn = [x / jnp.maximum(jnp.sqrt(s2), 1e-12) for x, s2 in zip(kk, ss)]
    logw = [logw_all[rw, sl] for rw, sl in zip(rws, sls)]
    cum = [_dot_xl(tri_seq, x, passes=ps["cum"]) for x in logw]
    gam = [jnp.exp(x) for x in cum]
    ginv = [jnp.exp(-x) for x in cum]
    a4 = [-(jnp.exp(cm - lw) * kn) for cm, lw, kn in zip(cum, logw, kkn)]
    b4 = [kn * a_all[rw, sl] * gi_ for kn, rw, sl, gi_ in zip(kkn, rws, sls, ginv)]
    k4 = [x * gi_ for x, gi_ in zip(k2, ginv)]
    r4 = [gm * x for gm, x in zip(gam, rg)]

    single = all(ps[c_] == 1 for c_ in ("lblk", "inv", "apply", "state"))
    nar = (lambda x: x.astype(BF16)) if single else (lambda x: x)
    a_st = [nar(stack(x)) for x in a4]
    r_st = [nar(stack(x)) for x in r4]
    v_st = [nar(stack(x)) for x in vg]
    prod = [mm_l(jnp.concatenate([x, y_], axis=0), jnp.concatenate([z, q_], axis=0), _NT)
            for x, y_, z, q_ in zip(a_st, r_st, b4, k4)]
    low = _iota(sq, 1) < HEAD_DIM

    def diag_blocks(x, mask):
        xr = pltpu.roll(x, HEAD_DIM, axis=1)
        return jnp.where(mask, jnp.where(low, x, xr), 0.0), jnp.where(mask, jnp.where(low, xr, x), 0.0)

    l_bk = [diag_blocks(x[:st], strict) for x in prod]
    m_bk = [diag_blocks(x[st:], incl) for x in prod]
    l_b, l_k = [x[0] for x in l_bk], [nar(x[1]) for x in l_bk]
    m_b, m_k = [nar(x[0]) for x in m_bk], [nar(x[1]) for x in m_bk]
    t = [nar(eye + x) for x in l_b]
    l_bn = [nar(x) for x in l_b]
    pw = [nar(mm_i(x, x)) for x in l_bn]
    cov = 2
    while cov < C:
        if 2 * cov >= C:
            t = [nar(x + mm_i(p_, x)) for x, p_ in zip(t, pw)]
        else:
            both = [mm_i(p_, jnp.concatenate([p_, x], axis=1)) for x, p_ in zip(t, pw)]
            pw = [nar(x[:, :st]) for x in both]
            t = [nar(x + y_[:, st:]) for x, y_ in zip(t, both)]
        cov *= 2
    lm_v = [mm_a(jnp.concatenate([x, y_], axis=0), z) for x, y_, z in zip(l_k, m_k, v_st)]
    mkv4 = [fold(x[st:]) for x in lm_v]
    t_av = [mm_a(x, jnp.concatenate([y_, nar(z[:st])], axis=1)) for x, y_, z in zip(t, a_st, lm_v)]
    wm4 = [fold(x[:, :st]) for x in t_av]
    uk4 = [fold(x[:, st:]) for x in t_av]

    us = [[None] * nb for _ in gs]
    ys = [[None] * nb for _ in gs]
    for ck in range(ncs):
        cgs = range(ck * gpb, (ck + 1) * gpb)
        for b in range(nb):
            rows = slice(b * C, (b + 1) * C)
            s_old = {gi: sbd_ref[b, gi - ck * gpb] for gi in cgs}
            uy = {gi: mm_s(jnp.concatenate([wm4[gi][rows], r4[gi][rows]], axis=0), s_old[gi], _NT) for gi in cgs}
            for gi in cgs:
                us[gi][b] = uy[gi][:C] + uk4[gi][rows]
                ys[gi][b] = uy[gi][C:]
            ds = {gi: mm_s(jnp.concatenate([us[gi][b], vg[gi][rows]], axis=0),
                           jnp.concatenate([b4[gi][rows], k4[gi][rows]], axis=0), _TN) for gi in cgs}
            for gi in cgs:
                s_new = (s_old[gi] + jnp.where(same_head, ds[gi], 0.0)) * gam[gi][(b + 1) * C - 1:(b + 1) * C, :]
                sbd_ref[b, gi - ck * gpb] = s_new
                if ck == ncs - 1:
                    @pl.when(c == nsteps - 1)
                    def _fin():
                        hs = slice((gi - ck * gpb) * HEADS_PER_GROUP, (gi - ck * gpb + 1) * HEADS_PER_GROUP)
                        sout_ref[b, hs] = _dot_xr(s_new, rep_t).reshape(HEADS_PER_GROUP, HEAD_DIM, HEAD_DIM)

    u = [x[0] if nb == 1 else jnp.concatenate(x, axis=0) for x in us]
    ysum = [x[0] if nb == 1 else jnp.concatenate(x, axis=0) for x in ys]
    yb = [fold(mm_s(x, stack(y_))) for x, y_ in zip(m_b, u)]
    y = [x + y_ + z for x, y_, z in zip(ysum, mkv4, yb)]

    ym = [_dot_xr(x, block_ones, passes=ps["gn"]) * (1.0 / HEAD_DIM) for x in y]
    yc = [x - y_ for x, y_ in zip(y, ym)]
    yv = [_dot_xr(x * x, block_ones, passes=ps["gn"]) * (1.0 / HEAD_DIM) for x in yc]
    bonus = [_dot_xr(rg[gi] * k2[gi] * rk_ref[:, sls[gi]], block_ones, passes=ps["gn"]) * vg[gi] for gi in gs]
    for gi in gs:
        sl, rw = sls[gi], rws[gi]
        yn = yc[gi] * lax.rsqrt(yv[gi] + GN_EPS) * gng_ref[:, sl] + gnb_ref[:, sl]
        o_ref[rw, sl] = (yn + bonus[gi]) * gate_all[rw, sl]


def _rwkv(proj, nseq, nchunk, C, prev_shift, state0, prm, gpb=None, ncs=None):
    nb = ROWS // C
    if gpb is None:
        gpb = N_GROUPS if nb == 1 else N_GROUPS // 2
    if ncs is None:
        ncs = 2 if nb == 1 and nchunk % 2 == 0 else 1
    assert nb * C == ROWS and nseq % nb == 0 and (nb == 1 or nchunk == 1) and nchunk % ncs == 0
    rows = nseq * nchunk * C
    nsteps = nchunk // ncs
    srows = ncs * ROWS
    w = gpb * GROUP_W
    sec = D_MODEL // w
    prows = 8 if nb == 1 else C
    pblk = 8 if nb == 1 else srows
    prev = jnp.zeros((nseq, prows, SHIFT_W), F32).at[:, 0].set(prev_shift).reshape(nseq * prows, SHIFT_W)
    prev_rkv = prev[:, :RKV_W]
    prev_lora = jnp.pad(prev[:, RKV_W:], ((0, 0), (0, LORA_PAD - LORA_W)))

    def tok(off):
        return pl.BlockSpec((srows, w), lambda s, g, c: (s * nsteps + c, off + g))

    def prv(off):
        return pl.BlockSpec((pblk, w), lambda s, g, c: (s, off + g))

    def vec(off=0):
        return pl.BlockSpec((1, w), lambda s, g, c: (0, off + g))

    up = pl.BlockSpec((LORA_PAD, w), lambda s, g, c: (0, g))
    in_specs = [
        tok(0), tok(sec), tok(2 * sec),
        pl.BlockSpec((srows, LORA_PAD), lambda s, g, c: (s * nsteps + c, COL_LORA // LORA_PAD)),
        prv(0), prv(sec), prv(2 * sec),
        pl.BlockSpec((pblk, LORA_PAD), lambda s, g, c: (s, 0)),
        pl.BlockSpec((nb, gpb * HEADS_PER_GROUP, HEAD_DIM, HEAD_DIM), lambda s, g, c: (s, g, 0, 0)),
        vec(0), vec(sec), vec(2 * sec),
        pl.BlockSpec((1, LORA_PAD), lambda s, g, c: (0, 0)),
        vec(), vec(), vec(), vec(), vec(), vec(), vec(),
        up, up, up,
    ]
    out_specs = [pl.BlockSpec((srows, w), lambda s, g, c: (s * nsteps + c, g)),
                 pl.BlockSpec((nb, gpb * HEADS_PER_GROUP, HEAD_DIM, HEAD_DIM), lambda s, g, c: (s, g, 0, 0))]
    return pl.pallas_call(
        functools.partial(_rwkv_body, C, nb, ncs, nsteps, gpb, RWKV_PASSES),
        grid=(nseq // nb, N_GROUPS // gpb, nsteps),
        in_specs=in_specs,
        out_specs=out_specs,
        out_shape=[jax.ShapeDtypeStruct((rows, D_MODEL), F32),
                   jax.ShapeDtypeStruct((nseq, D_MODEL // HEAD_DIM, HEAD_DIM, HEAD_DIM), F32)],
        scratch_shapes=[pltpu.VMEM((nb, gpb, GROUP_W, GROUP_W), F32),
                        pltpu.VMEM((srows + 8, 3 * w), F32),
                        pltpu.VMEM((srows + 8, LORA_PAD), F32)],
        compiler_params=_params(("arbitrary", "arbitrary", "arbitrary")),
        name="rwkv",
    )(proj, proj, proj, proj, prev_rkv, prev_rkv, prev_rkv, prev_lora, state0,
      prm["mu_rkv"], prm["mu_rkv"], prm["mu_rkv"], prm["mu_lora"],
      prm["w0"], prm["a0"], prm["k_k"], prm["k_a"], prm["r_k"], prm["gn_g"], prm["gn_b"],
      prm["w_up"], prm["a_up"], prm["g_up"])


def _attn_rows(q, kcat, vcat, sink_ref, first_key, store, npr):
    m = q.shape[0]
    kt_all = kcat.T
    qi = _iota((npr * m, 2 * WINDOW), 0) & (m - 1)
    ki = _iota((npr * m, 2 * WINDOW), 1)
    allowed = (ki >= qi + 1) & (ki <= qi + WINDOW) & (ki >= first_key)
    lane = _iota((2 * WINDOW, 128), 1)
    ones_k = jnp.ones((2 * WINDOW, 128), BF16)
    zk = jnp.zeros((HEAD_DIM, 2 * WINDOW), F32)
    nkv = KV_W // HEAD_DIM
    k2s, v2s = [], []
    for kv in range(nkv):
        kt = kt_all[kv * HEAD_DIM:(kv + 1) * HEAD_DIM, :]
        k2s.append(jnp.concatenate([jnp.concatenate([kt, zk], axis=1),
                                    jnp.concatenate([zk, kt], axis=1)], axis=0).astype(BF16))
        slab = vcat[:, (kv // 2) * 128:(kv // 2 + 1) * 128]
        rolled = pltpu.roll(slab, HEAD_DIM, axis=1)
        lo_src, hi_src = (slab, rolled) if kv % 2 == 0 else (rolled, slab)
        v2s.append(jnp.concatenate([jnp.where(lane < HEAD_DIM, lo_src, 0.0),
                                    jnp.where(lane >= HEAD_DIM, hi_src, 0.0)], axis=0).astype(BF16))
    items = [(kv, p0) for kv in range(nkv) for p0 in range(0, 4, npr)]
    cols = [[kv * 512 + (p0 + i) * 128 for i in range(npr)] for kv, p0 in items]
    scores = [_bdot((jnp.concatenate([q[:, col:col + 128] for col in cs], axis=0) * ATT_SCALE).astype(BF16),
                    k2s[kv]) for (kv, _), cs in zip(items, cols)]
    probs = []
    for (kv, p0), s in zip(items, scores):
        ps = []
        for hb in range(2):
            sink = jnp.concatenate(
                [jnp.broadcast_to(sink_ref[:, (kv * 8 + (p0 + i) * 2 + hb) * 128:
                                           (kv * 8 + (p0 + i) * 2 + hb) * 128 + 1], (m, 1))
                 for i in range(npr)], axis=0)
            sh = jnp.where(allowed, s[:, hb * 256:(hb + 1) * 256], -jnp.inf)
            mx = jnp.maximum(jnp.max(sh, axis=1, keepdims=True), sink)
            e = jnp.exp(sh - mx)
            inv = 1.0 / (_dot_xr(e, ones_k, passes=2) + jnp.exp(sink - mx))
            ps += [e[:, :128] * inv, e[:, 128:] * inv]
        probs.append(jnp.concatenate(ps, axis=1).astype(BF16))
    outs = [_bdot(p, v2s[kv]) for (kv, _), p in zip(items, probs)]
    for cs, o in zip(cols, outs):
        for i, col in enumerate(cs):
            store(col, o[i * m:(i + 1) * m])


def _attn_prompt_body(q_ref, kvc_ref, kvp_ref, sink_ref, o_ref):
    n = pl.program_id(0)
    kvc = kvc_ref[...]
    kvp = kvp_ref[...]
    kcat = jnp.concatenate([kvp[:, :KV_W], kvc[:, :KV_W]], axis=0)
    vcat = jnp.concatenate([kvp[:, KV_W:], kvc[:, KV_W:]], axis=0)

    def store(col, val):
        o_ref[:, col:col + 128] = val

    _attn_rows(q_ref[...], kcat, vcat, sink_ref, jnp.where(n > 0, 0, WINDOW), store, npr=1)


def _attn_prompt(proj, sinks_e):
    t = proj.shape[0]
    nb = t // WINDOW
    qb, kvb = COL_Q // D_MODEL, COL_KV // (2 * KV_W)
    return pl.pallas_call(
        _attn_prompt_body,
        grid=(nb,),
        in_specs=[pl.BlockSpec((WINDOW, D_MODEL), lambda n: (n, qb)),
                  pl.BlockSpec((WINDOW, 2 * KV_W), lambda n: (n, kvb)),
                  pl.BlockSpec((WINDOW, 2 * KV_W), lambda n: (jnp.maximum(n - 1, 0), kvb)),
                  pl.BlockSpec((1, 32 * 128), lambda n: (0, 0))],
        out_specs=pl.BlockSpec((WINDOW, D_MODEL), lambda n: (n, 0)),
        out_shape=jax.ShapeDtypeStruct((t, D_MODEL), F32),
        compiler_params=_params(("parallel",)),
        name="attn_prompt",
    )(proj, proj, proj, sinks_e)


def _attn_sample_body(L, bt, q_ref, kvn_ref, ck_ref, cv_ref, sink_ref, o_ref):
    pad = jnp.zeros((WINDOW - L, KV_W), F32)

    def one(b, carry):
        rows = pl.ds(pl.multiple_of(b * L, L), L)
        kvn = kvn_ref[rows, :]
        kcat = jnp.concatenate([ck_ref[b], kvn[:, :KV_W], pad], axis=0)
        vcat = jnp.concatenate([cv_ref[b], kvn[:, KV_W:], pad], axis=0)

        def store(col, val):
            o_ref[rows, col:col + 128] = val

        _attn_rows(q_ref[rows, :], kcat, vcat, sink_ref, 0, store, npr=4)
        return carry

    lax.fori_loop(0, bt, one, 0, unroll=2)


def _attn_sample(proj, ck, cv, sinks_e, L, bt=16):
    rows = proj.shape[0]
    nb = rows // L
    bt = min(bt, nb)
    qb, kvb = COL_Q // D_MODEL, COL_KV // (2 * KV_W)
    return pl.pallas_call(
        functools.partial(_attn_sample_body, L, bt),
        grid=(nb // bt,),
        in_specs=[pl.BlockSpec((bt * L, D_MODEL), lambda i: (i, qb)),
                  pl.BlockSpec((bt * L, 2 * KV_W), lambda i: (i, kvb)),
                  pl.BlockSpec((bt, WINDOW, KV_W), lambda i: (i, 0, 0)),
                  pl.BlockSpec((bt, WINDOW, KV_W), lambda i: (i, 0, 0)),
                  pl.BlockSpec((1, 32 * 128), lambda i: (0, 0))],
        out_specs=pl.BlockSpec((bt * L, D_MODEL), lambda i: (i, 0)),
        out_shape=jax.ShapeDtypeStruct((rows, D_MODEL), F32),
        compiler_params=_params(("parallel",)),
        name="attn_sample",
    )(proj, proj, ck, cv, sinks_e)


def _layer_norm(x, g, b):
    mu = jnp.mean(x, axis=-1, keepdims=True)
    xc = x - mu
    var = jnp.mean(xc * xc, axis=-1, keepdims=True)
    return xc * lax.rsqrt(var + LN_EPS) * g + b


def _merge_body(ga_ref, gb_ref, oa_ref, ob_ref, x_ref, wout_ref, g_ref, b_ref, h_ref, ht_ref):
    mixed = _sigmoid(ga_ref[...]) * oa_ref[...] + _sigmoid(gb_ref[...]) * ob_ref[...]
    y = _bdot(mixed.astype(BF16), wout_ref[...])
    h = _layer_norm(DN_ALPHA * x_ref[...] + y, g_ref[...], b_ref[...])
    h_ref[...] = h
    ht_ref[...] = h.T.astype(BF16)


def _merge(proj, o_a, o_b, x2d, wout_bf16, ln_g, ln_b, tm=256):
    m = x2d.shape[0]
    tm = min(tm, m)
    gab = COL_GATE // D_MODEL
    row = lambda i: (i, 0)
    return pl.pallas_call(
        _merge_body,
        grid=(m // tm,),
        in_specs=[pl.BlockSpec((tm, D_MODEL), lambda i: (i, gab)),
                  pl.BlockSpec((tm, D_MODEL), lambda i: (i, gab + 1)),
                  pl.BlockSpec((tm, D_MODEL), row),
                  pl.BlockSpec((tm, D_MODEL), row),
                  pl.BlockSpec((tm, D_MODEL), row),
                  pl.BlockSpec((D_MODEL, D_MODEL), lambda i: (0, 0)),
                  pl.BlockSpec((1, D_MODEL), lambda i: (0, 0)),
                  pl.BlockSpec((1, D_MODEL), lambda i: (0, 0))],
        out_specs=[pl.BlockSpec((tm, D_MODEL), row),
                   pl.BlockSpec((D_MODEL, tm), lambda i: (0, i))],
        out_shape=[jax.ShapeDtypeStruct((m, D_MODEL), F32),
                   jax.ShapeDtypeStruct((D_MODEL, m), BF16)],
        compiler_params=_params(("parallel",)),
        name="merge",
    )(proj, proj, o_a, o_b, x2d, wout_bf16, ln_g, ln_b)


def _top_rows(x, count):
    return _top_rows_multi([x], count)[0]


def _top_rows_multi(xs, count):
    rows = _iota(xs[0].shape, 0)
    big = xs[0].shape[0]
    outs = [[] for _ in xs]
    for _ in range(count):
        mxs = [jnp.max(x, axis=0, keepdims=True) for x in xs]
        for o, mx in zip(outs, mxs):
            o.append(mx)
        firsts = [jnp.min(jnp.where(x == mx, rows, big), axis=0, keepdims=True) for x, mx in zip(xs, mxs)]
        xs = [jnp.where(rows == f, -jnp.inf, x) for x, f in zip(xs, firsts)]
    return outs


def _bitonic_network(n):
    comps = []
    k = 2
    while k <= n:
        j = k // 2
        while j >= 1:
            comps += [(i, i ^ j, (i & k) == 0) for i in range(n) if (i ^ j) > i]
            j //= 2
        k *= 2
    return comps


def _top16_of_128(xs):
    nt = N_KEYS // 8
    tiles = [[x[8 * j:8 * (j + 1)] for j in range(nt)] for x in xs]
    for i, l, desc in _bitonic_network(nt):
        for t in tiles:
            hi, lo = jnp.maximum(t[i], t[l]), jnp.minimum(t[i], t[l])
            t[i], t[l] = (hi, lo) if desc else (lo, hi)
    sub = _iota(tiles[0][0].shape, 0)
    outs = [[] for _ in xs]
    for r in range(PEER_TOPK):
        for t, o in zip(tiles, outs):
            mx = jnp.max(t[0], axis=0, keepdims=True)
            o.append(mx)
            first = jnp.min(jnp.where(t[0] == mx, sub, 8), axis=0, keepdims=True)
            pop = sub == first
            for j in range(PEER_TOPK - r - 1):
                t[j] = jnp.where(pop, t[j + 1], t[j])
    return outs


def _route_body(tm, h_ref, wq_ref, keys_ref, s0_ref, s1_ref, aux_ref, q_s):
    q = _bdot(h_ref[...].astype(BF16), wq_ref[...])
    for hc in range(2 * PEER_HEADS):
        q_s[hc] = q[:, hc * N_KEYS:(hc + 1) * N_KEYS]
    pairs = [(i, j) for i in range(PEER_TOPK) for j in range(PEER_TOPK) if (i + 1) * (j + 1) <= PEER_TOPK]
    npad = -len(pairs) % 8
    neg = jnp.full((1, tm), -jnp.inf, F32)
    zero = jnp.zeros((1, tm), F32)

    def head(hh, carry):
        s0 = _bdot(keys_ref[hh, 0], q_s[2 * hh].astype(BF16), _NT)
        s1 = _bdot(keys_ref[hh, 1], q_s[2 * hh + 1].astype(BF16), _NT)
        s0_ref[hh] = s0
        for lb in range(tm // 128):
            s1_ref[hh, lb] = s1[:, lb * 128:(lb + 1) * 128]
        top0, top1 = _top16_of_128([s0, s1])
        cand = jnp.concatenate([top0[i] + top1[j] for i, j in pairs] + [neg] * npad, axis=0)
        best = _top_rows(cand, PEER_TOPK)
        mx = best[0]
        z = zero
        for bsum in best:
            z = z + jnp.exp(bsum - mx)
        aux_ref[hh] = jnp.concatenate([best[-1], top0[0], top1[0], 1.0 / z, zero, zero, zero, zero], axis=0)
        return carry

    lax.fori_loop(0, PEER_HEADS, head, 0)


def _route(h, wq_bf16, keys_bf16, tm=512):
    m = h.shape[0]
    tm = min(tm, m)
    tok3 = lambda i: (0, 0, i)
    return pl.pallas_call(
        functools.partial(_route_body, tm),
        grid=(m // tm,),
        in_specs=[pl.BlockSpec((tm, D_MODEL), lambda i: (i, 0)),
                  pl.BlockSpec((D_MODEL, D_MODEL), lambda i: (0, 0)),
                  pl.BlockSpec((PEER_HEADS, 2, N_KEYS, N_KEYS), lambda i: (0, 0, 0, 0))],
        out_specs=[pl.BlockSpec((PEER_HEADS, N_KEYS, tm), tok3),
                   pl.BlockSpec((PEER_HEADS, tm // 128, N_KEYS, 128), lambda i: (0, i, 0, 0)),
                   pl.BlockSpec((PEER_HEADS, 8, tm), tok3)],
        out_shape=[jax.ShapeDtypeStruct((PEER_HEADS, N_KEYS, m), F32),
                   jax.ShapeDtypeStruct((PEER_HEADS, m // 128, N_KEYS, 128), F32),
                   jax.ShapeDtypeStruct((PEER_HEADS, 8, m), F32)],
        scratch_shapes=[pltpu.VMEM((2 * PEER_HEADS, tm, N_KEYS), F32)],
        compiler_params=_params(("parallel",)),
        name="peer_route",
    )(h, wq_bf16, keys_bf16)


def _peer_body(te, ne, ht_ref, u_ref, vt_ref, s0_ref, s1_ref, aux_ref, h_ref, g_ref, b_ref, y_ref,
               acc_ref, e1_ref, cf_ref, st0_ref, st1_ref, w0_ref, w1_ref):
    j = pl.program_id(1)
    tm = acc_ref.shape[1]

    @pl.when(j == 0)
    def _():
        acc_ref[...] = jnp.zeros_like(acc_ref)
        for ref in (st0_ref, st1_ref, w0_ref, w1_ref):
            ref[...] = jnp.zeros_like(ref)
        for hh in range(PEER_HEADS):
            aux = aux_ref[hh]
            for lb in range(tm // 128):
                e1_ref[hh, lb] = jnp.exp(s1_ref[hh, lb] - aux[2:3, lb * 128:(lb + 1) * 128])
            cf_ref[hh] = jnp.exp(s0_ref[hh] - aux[1:2, :]) * aux[3:4, :]

    def stages(st_new, st_old, w_new, w_old):
        jb = j - 1
        live = (jb >= 0) & (jb < ne)
        jb_c = jnp.clip(jb, 0, ne - 1)
        taus = [jnp.where(live, aux_ref[hh, 0:1, :], jnp.inf) for hh in range(PEER_HEADS)]
        ncc = te // N_KEYS
        s0rows = [[s0_ref[hh, pl.ds(jb_c * ncc + cc, 1), :] for hh in range(PEER_HEADS)] for cc in range(ncc)]
        cfrows = [[cf_ref[hh, pl.ds(jb_c * ncc + cc, 1), :] for hh in range(PEER_HEADS)] for cc in range(ncc)]
        tw = min(256, tm)
        mxu_pieces, gate_tiles = [], []
        for t0 in range(0, tm, tw):
            ts = slice(t0, t0 + tw)
            lbs = range(t0 // 128, (t0 + tw) // 128)

            def mix(mr, ts=ts, lbs=lbs):
                acc_ref[mr, ts] += _bdot(vt_ref[mr, :], jnp.concatenate([w_old[lb] for lb in lbs], axis=1))

            def score(er, t0=t0, ts=ts, lbs=lbs):
                st = _bdot(u_ref[er, :], ht_ref[:, ts])
                for lb in lbs:
                    st_new[lb, er, :] = st[:, lb * 128 - t0:(lb + 1) * 128 - t0]

            for q in range(4):
                mxu_pieces.append(functools.partial(mix, slice(q * (D_MODEL // 4), (q + 1) * (D_MODEL // 4))))
            for q in range(2):
                mxu_pieces.append(functools.partial(score, slice(q * (te // 2), (q + 1) * (te // 2))))

            def gate(lb, cc):
                ls = slice(lb * 128, (lb + 1) * 128)
                rows = slice(cc * N_KEYS, (cc + 1) * N_KEYS)
                gsum = None
                for hh in range(PEER_HEADS):
                    sel = (s1_ref[hh, lb] + s0rows[cc][hh][:, ls]) >= taus[hh][:, ls]
                    term = jnp.where(sel, e1_ref[hh, lb] * cfrows[cc][hh][:, ls], 0.0)
                    gsum = term if gsum is None else gsum + term
                so = st_old[lb, rows, :]
                act = 0.5 * so * (1.0 + lax.erf(so * np.float32(np.sqrt(0.5))))
                w_new[lb, rows, :] = (gsum * act).astype(BF16)

            for lb in lbs:
                for cc in range(ncc):
                    gate_tiles.append(functools.partial(gate, lb, cc))
        for i in range(max(len(mxu_pieces), len(gate_tiles))):
            if i < len(gate_tiles):
                gate_tiles[i]()
            if i < len(mxu_pieces):
                mxu_pieces[i]()

    @pl.when(j % 2 == 0)
    def _():
        stages(st0_ref, st1_ref, w1_ref, w0_ref)

    @pl.when(j % 2 == 1)
    def _():
        stages(st1_ref, st0_ref, w0_ref, w1_ref)

    @pl.when(j == ne + 1)
    def _():
        out = acc_ref[...].T
        y_ref[...] = _layer_norm(DN_ALPHA * h_ref[...] + out, g_ref[...], b_ref[...])


def _peer(ht, u_bf16, vt_bf16, s0t, s1t, aux, h, ln_g, ln_b, tm=512, te=512):
    m = h.shape[0]
    tm = min(tm, m)
    ne = N_EXPERTS // te
    tok3 = lambda i, j: (0, 0, i)
    return pl.pallas_call(
        functools.partial(_peer_body, te, ne),
        grid=(m // tm, ne + 2),
        in_specs=[pl.BlockSpec((D_MODEL, tm), lambda i, j: (0, i)),
                  pl.BlockSpec((te, D_MODEL), lambda i, j: (jnp.minimum(j, ne - 1), 0)),
                  pl.BlockSpec((D_MODEL, te), lambda i, j: (0, jnp.clip(j - 2, 0, ne - 1))),
                  pl.BlockSpec((PEER_HEADS, N_KEYS, tm), tok3),
                  pl.BlockSpec((PEER_HEADS, tm // 128, N_KEYS, 128), lambda i, j: (0, i, 0, 0)),
                  pl.BlockSpec((PEER_HEADS, 8, tm), tok3),
                  pl.BlockSpec((tm, D_MODEL), lambda i, j: (i, 0)),
                  pl.BlockSpec((1, D_MODEL), lambda i, j: (0, 0)),
                  pl.BlockSpec((1, D_MODEL), lambda i, j: (0, 0))],
        out_specs=pl.BlockSpec((tm, D_MODEL), lambda i, j: (i, 0)),
        out_shape=jax.ShapeDtypeStruct((m, D_MODEL), F32),
        scratch_shapes=[pltpu.VMEM((D_MODEL, tm), F32),
                        pltpu.VMEM((PEER_HEADS, tm // 128, N_KEYS, 128), F32),
                        pltpu.VMEM((PEER_HEADS, N_KEYS, tm), F32),
                        pltpu.VMEM((tm // 128, te, 128), F32),
                        pltpu.VMEM((tm // 128, te, 128), F32),
                        pltpu.VMEM((tm // 128, te, 128), BF16),
                        pltpu.VMEM((tm // 128, te, 128), BF16)],
        compiler_params=_params(("parallel", "arbitrary")),
        name="peer_dense",
    )(ht, u_bf16, vt_bf16, s0t, s1t, aux, h, ln_g, ln_b)


def _pad_rows(w, lo, total):
    return jnp.zeros((total, w.shape[1]), w.dtype).at[lo:lo + w.shape[0]].set(w)


def _layer(x2d, nseq, seq_len, chunk, prev_shift, state0, cache, wts):
    proj_a = _inproj(x2d, wts["w_in_t"], PROJ_A_W)
    proj_b = _inproj(x2d, wts["w_in_b"], PROJ_B_W)
    o_a, s_last = _rwkv(proj_a, nseq, seq_len // chunk, chunk, prev_shift, state0, wts)
    if cache is None:
        o_b = _attn_prompt(proj_b, wts["sinks"])
    else:
        o_b = _attn_sample(proj_b, cache[0], cache[1], wts["sinks"], seq_len)
    h, ht = _merge(proj_b, o_a, o_b, x2d, wts["w_out"], wts["ln1_g"], wts["ln1_b"])
    s0t, s1t, aux = _route(h, wts["peer_wq"], wts["peer_keys"])
    y = _peer(ht, wts["peer_u"], wts["peer_vt"], s0t, s1t, aux, h, wts["ln2_g"], wts["ln2_b"])
    return y, proj_a, proj_b, s_last


def kernel(x_prompt, x_sample, state_shift, state_wkv, cache_k, cache_v, w_in, rw_mu, rw_w0, rw_w_up, rw_a0,
           rw_a_up, rw_g_up, rw_k_k, rw_k_a, rw_r_k, rw_gn_g, rw_gn_b, att_sinks, w_out, ln1_g, ln1_b,
           peer_wq, peer_keys, peer_u, peer_v, ln2_g, ln2_b):
    depth = w_in.shape[0]
    assert depth == 1
    l = 0
    bp, tp, _ = x_prompt.shape
    bs, ts, _ = x_sample.shape
    assert bp == 1
    win = cache_k.shape[2]
    assert win == WINDOW and tp % WINDOW == 0 and ts <= 8

    w_t = jnp.transpose(w_in[l]).astype(BF16)
    q0 = SHIFT_W
    k0 = q0 + D_MODEL
    g0 = k0 + 2 * KV_W
    w_b = jnp.concatenate([w_t[q0:q0 + D_MODEL], w_t[g0:g0 + 2 * D_MODEL], w_t[k0:k0 + 2 * KV_W]], axis=0)
    mu = rw_mu[l]
    row = lambda v: v.reshape(1, -1)
    wts = dict(
        w_in_t=w_t, w_in_b=w_b,
        mu_rkv=row(mu[:RKV_W]),
        mu_lora=row(jnp.pad(mu[RKV_W:], (0, LORA_PAD - LORA_W))),
        w0=row(rw_w0[l]), a0=row(rw_a0[l]), k_k=row(rw_k_k[l]), k_a=row(rw_k_a[l]), r_k=row(rw_r_k[l]),
        gn_g=row(rw_gn_g[l]), gn_b=row(rw_gn_b[l]),
        w_up=_pad_rows(rw_w_up[l], 0, LORA_PAD).astype(BF16),
        a_up=_pad_rows(rw_a_up[l], 96, LORA_PAD).astype(BF16),
        g_up=_pad_rows(rw_g_up[l], 192, LORA_PAD).astype(BF16),
        sinks=jnp.repeat(att_sinks[l], 128).reshape(1, 32 * 128),
        w_out=w_out[l].astype(BF16),
        ln1_g=row(ln1_g[l]), ln1_b=row(ln1_b[l]), ln2_g=row(ln2_g[l]), ln2_b=row(ln2_b[l]),
        peer_wq=peer_wq[l].astype(BF16),
        peer_keys=peer_keys[l].astype(BF16),
        peer_u=peer_u[l].astype(BF16),
        peer_vt=peer_v[l].T.astype(BF16),
    )

    chunk_p = 64
    y_p, pa_p, pb_p, s_p = _layer(x_prompt[0], 1, tp, chunk_p,
                                  jnp.zeros((1, SHIFT_W), F32),
                                  jnp.zeros((1, D_MODEL // HEAD_DIM, HEAD_DIM, HEAD_DIM), F32), None, wts)
    keep = min(WINDOW, tp)
    kv_p = pb_p[tp - keep:, COL_KV:]
    y_prompt = y_p[None]
    new_shift_prompt = pa_p[tp - 1, :SHIFT_W][None, None]
    new_wkv_prompt = s_p[None]
    new_k_prompt = kv_p[:, :KV_W].reshape(1, 1, keep, KV_W // HEAD_DIM, HEAD_DIM)
    new_v_prompt = kv_p[:, KV_W:].reshape(1, 1, keep, KV_W // HEAD_DIM, HEAD_DIM)

    ck = cache_k[l].reshape(bs, win, KV_W)
    cv = cache_v[l].reshape(bs, win, KV_W)
    y_s, pa_s, pb_s, s_s = _layer(x_sample.reshape(bs * ts, D_MODEL), bs, ts, ts,
                                  state_shift[l], state_wkv[l], (ck, cv), wts)
    y_sample = y_s.reshape(bs, ts, D_MODEL)
    new_shift_sample = pa_s.reshape(bs, ts, PROJ_A_W)[:, ts - 1, :SHIFT_W][None]
    new_wkv_sample = s_s[None]
    kv_s = pb_s.reshape(bs, ts, PROJ_B_W)[:, :, COL_KV:]
    new_k_sample = jnp.concatenate([ck, kv_s[:, :, :KV_W]], axis=1)[:, ts:].reshape(
        1, bs, win, KV_W // HEAD_DIM, HEAD_DIM)
    new_v_sample = jnp.concatenate([cv, kv_s[:, :, KV_W:]], axis=1)[:, ts:].reshape(
        1, bs, win, KV_W // HEAD_DIM, HEAD_DIM)

    return (y_prompt, y_sample, new_shift_prompt, new_wkv_prompt, new_k_prompt, new_v_prompt,
            new_shift_sample, new_wkv_sample, new_k_sample, new_v_sample)
```

```python
import functools

import numpy as np
import jax
import jax.numpy as jnp
from jax import lax
from jax.experimental import pallas as pl
from jax.experimental.pallas import tpu as pltpu

F32, BF16 = jnp.float32, jnp.bfloat16

D_MODEL = 2048
HEAD_DIM = 64
HEADS_PER_GROUP = 2
GROUP_W = HEADS_PER_GROUP * HEAD_DIM
N_GROUPS = D_MODEL // GROUP_W
RKV_W = 3 * D_MODEL
LORA_W = 96 + 96 + 256
LORA_PAD = 512
SHIFT_W = RKV_W + LORA_W
KV_W = 256
GN_EPS = 64e-5
LN_EPS = 1e-5
WINDOW = 128
ATT_SCALE = HEAD_DIM ** -0.5
N_KEYS = 128
N_EXPERTS = N_KEYS * N_KEYS
PEER_HEADS = 8
PEER_TOPK = 16
DN_ALPHA = 2.0 ** 0.25

COL_LORA = RKV_W
PROJ_A_W = RKV_W + LORA_PAD
COL_Q = 0
COL_GATE = D_MODEL
COL_KV = 3 * D_MODEL
PROJ_B_W = COL_KV + 2 * KV_W

VMEM_LIMIT = 56 * 1024 * 1024

_NN = (((1,), (0,)), ((), ()))
_NT = (((1,), (1,)), ((), ()))
_TN = (((0,), (0,)), ((), ()))


def _bdot(a, b, dn=_NN):
    return lax.dot_general(a, b, dn, preferred_element_type=F32)


def _dot(a, b, dn=_NN):
    return _bdot(a.astype(BF16), b.astype(BF16), dn)


def _split(x, n):
    parts, r = [], x
    for i in range(n):
        p = r.astype(BF16)
        parts.append(p)
        if i + 1 < n:
            r = r - p.astype(F32)
    return parts


def _dot3(a, b, dn=_NN):
    a1, a2 = _split(a, 2)
    b1, b2 = _split(b, 2)
    return (_bdot(a1, b2, dn) + _bdot(a2, b1, dn)) + _bdot(a1, b1, dn)


def _dot_xl(a_bf16, b, dn=_NN, passes=3):
    out = None
    for part in reversed(_split(b, passes)):
        t = _bdot(a_bf16, part, dn)
        out = t if out is None else out + t
    return out


def _dot_xr(a, b_bf16, dn=_NN, passes=3):
    out = None
    for part in reversed(_split(a, passes)):
        t = _bdot(part, b_bf16, dn)
        out = t if out is None else out + t
    return out


def _mm(passes):
    return _dot if passes == 1 else _dot3


def _sigmoid(x):
    return 1.0 / (1.0 + jnp.exp(-x))


def _iota(shape, dim):
    return lax.broadcasted_iota(jnp.int32, shape, dim)


def _head_of(idx):
    return jnp.right_shift(idx, 6)


def _ones_where(mask):
    return jnp.where(mask, 1.0, 0.0).astype(BF16)


def _params(sem):
    return pltpu.CompilerParams(dimension_semantics=sem, vmem_limit_bytes=VMEM_LIMIT)


def _inproj_body(x_ref, w_ref, o_ref, xb_ref):
    @pl.when(pl.program_id(1) == 0)
    def _():
        xb_ref[...] = x_ref[...].astype(BF16)

    o_ref[...] = _bdot(xb_ref[...], w_ref[...].astype(BF16), _NT)


def _inproj(x2d, w_t, n, tn=1664):
    assert n % tn == 0 and tn % 128 == 0
    m = x2d.shape[0]
    tm = min(m, 1024)
    return pl.pallas_call(
        _inproj_body,
        grid=(m // tm, n // tn),
        in_specs=[pl.BlockSpec((tm, D_MODEL), lambda i, j: (i, 0)),
                  pl.BlockSpec((tn, D_MODEL), lambda i, j: (j, 0))],
        out_specs=pl.BlockSpec((tm, tn), lambda i, j: (i, j)),
        out_shape=jax.ShapeDtypeStruct((m, n), F32),
        scratch_shapes=[pltpu.VMEM((tm, D_MODEL), BF16)],
        compiler_params=_params(("parallel", "arbitrary")),
        name="inproj",
    )(x2d, w_t)


RWKV_PASSES = dict(seg=2, cum=3, lblk=1, inv=1, apply=1, state=1, gn=2)
ROWS = 64


def _rwkv_body(C, nb, ncs, nsteps, gpb, ps,
               r_ref, k_ref, v_ref, lo_ref, pr_ref, pk_ref, pv_ref, plo_ref, s0_ref,
               mur_ref, muk_ref, muv_ref, mulo_ref, w0_ref, a0_ref, kk_ref, ka_ref, rk_ref, gng_ref, gnb_ref,
               wup_ref, aup_ref, gup_ref,
               o_ref, sout_ref,
               sbd_ref, sh_ref, shl_ref):
    assert HEADS_PER_GROUP == 2 and HEADS_PER_GROUP * ROWS == GROUP_W and (ncs == 1 or nb == 1)
    c = pl.program_id(2)
    w = gpb * GROUP_W
    srows = ncs * ROWS
    log2c = C.bit_length() - 1
    st = HEADS_PER_GROUP * ROWS
    sq = (st, st)
    ri, ci = _iota(sq, 0), _iota(sq, 1)
    same_head = _head_of(ri) == _head_of(ci)
    tr, tc = ri & (ROWS - 1), ci & (ROWS - 1)
    blk = same_head & (jnp.right_shift(tr, log2c) == jnp.right_shift(tc, log2c))
    strict = blk & (tr > tc)
    incl = blk & (tr >= tc)
    eye = jnp.where(ri == ci, 1.0, 0.0)
    block_ones = _ones_where(same_head)
    r64, c64 = _iota((ROWS, ROWS), 0), _iota((ROWS, ROWS), 1)
    tri_seq = _ones_where((jnp.right_shift(r64, log2c) == jnp.right_shift(c64, log2c)) & (r64 >= c64))
    lane_head = _head_of(_iota((ROWS, GROUP_W), 1))
    first_head_rows = ri < HEAD_DIM

    def to_block_diag(s2):
        wide = jnp.concatenate([s2, jnp.zeros_like(s2)], axis=1)
        return jnp.where(first_head_rows, wide, pltpu.roll(wide, HEAD_DIM, axis=1))

    def from_block_diag(s):
        return jnp.where(first_head_rows, s, pltpu.roll(s, HEAD_DIM, axis=1))[:, :HEAD_DIM]
    mm_l, mm_i, mm_a, mm_s = _mm(ps["lblk"]), _mm(ps["inv"]), _mm(ps["apply"]), _mm(ps["state"])

    @pl.when(c == 0)
    def _init():
        if nb == 1:
            sh_ref[7:8, 0:w] = pr_ref[0:1, :]
            sh_ref[7:8, w:2 * w] = pk_ref[0:1, :]
            sh_ref[7:8, 2 * w:3 * w] = pv_ref[0:1, :]
            shl_ref[7:8, :] = plo_ref[0:1, :]
        for b in range(nb):
            for gi in range(gpb):
                s0 = s0_ref[b, gi * HEADS_PER_GROUP:(gi + 1) * HEADS_PER_GROUP].reshape(GROUP_W, HEAD_DIM)
                sbd_ref[b, gi] = to_block_diag(s0)

    def lerp(x, ref, lo, hi, prev_ref, mu):
        ref[pl.ds(8, srows), lo:hi] = x
        xs = ref[pl.ds(7, srows), lo:hi]
        ref[7:8, lo:hi] = x[srows - 1:srows, :]
        if nb > 1:
            first = (_iota(x.shape, 0) & (C - 1)) == 0
            xs = jnp.where(first, prev_ref[...], xs)
        return x + (xs - x) * mu

    r = lerp(r_ref[...], sh_ref, 0, w, pr_ref, mur_ref[...])
    k = lerp(k_ref[...], sh_ref, w, 2 * w, pk_ref, muk_ref[...])
    v = lerp(v_ref[...], sh_ref, 2 * w, 3 * w, pv_ref, muv_ref[...])
    lo = lerp(lo_ref[...], shl_ref, 0, LORA_PAD, plo_ref, mulo_ref[...])

    wl = w0_ref[...] + _dot(jnp.tanh(lo), wup_ref[...])
    w_log = -(jnp.maximum(-wl, 0.0) + jnp.log1p(jnp.exp(-jnp.abs(wl)))) - 0.5
    logw_all = -jnp.exp(w_log)
    a_all = _sigmoid(a0_ref[...] + _dot(lo, aup_ref[...]))
    gate_all = _dot(_sigmoid(lo), gup_ref[...])
    kk_all = k * kk_ref[...]
    k2_all = k * (1.0 + (a_all - 1.0) * ka_ref[...])

    def stack(x):
        return jnp.concatenate([jnp.where(lane_head == j, x, 0.0) for j in range(HEADS_PER_GROUP)], axis=0)

    def fold(x):
        return x[0:ROWS] + x[ROWS:2 * ROWS]

    gs = range(ncs * gpb)
    sls = [slice(gi * GROUP_W, (gi + 1) * GROUP_W) for _ in range(ncs) for gi in range(gpb)]
    rws = [slice(ck * ROWS, (ck + 1) * ROWS) for ck in range(ncs) for _ in range(gpb)]
    rg = [r[rw, sl] for rw, sl in zip(rws, sls)]
    vg = [v[rw, sl] for rw, sl in zip(rws, sls)]
    k2 = [k2_all[rw, sl] for rw, sl in zip(rws, sls)]
    kk = [kk_all[rw, sl] for rw, sl in zip(rws, sls)]
    ss = [_dot_xr(x * x, block_ones, passes=ps["seg"]) for x in kk]
    kkn = [x / jnp.maximum(jnp.sqrt(s2), 1e-12) for x, s2 in zip(kk, ss)]
    logw = [logw_all[rw, sl] for rw, sl in zip(rws, sls)]
    cum = [_dot_xl(tri_seq, x, passes=ps["cum"]) for x in logw]
    gam = [jnp.exp(x) for x in cum]
    ginv = [jnp.exp(-x) for x in cum]
    a4 = [-(jnp.exp(cm - lw) * kn) for cm, lw, kn in zip(cum, logw, kkn)]
    b4 = [kn * a_all[rw, sl] * gi_ for kn, rw, sl, gi_ in zip(kkn, rws, sls, ginv)]
    k4 = [x * gi_ for x, gi_ in zip(k2, ginv)]
    r4 = [gm * x for gm, x in zip(gam, rg)]

    single = all(ps[c_] == 1 for c_ in ("lblk", "inv", "apply", "state"))
    nar = (lambda x: x.astype(BF16)) if single else (lambda x: x)
    a_st = [nar(stack(x)) for x in a4]
    r_st = [nar(stack(x)) for x in r4]
    v_st = [nar(stack(x)) for x in vg]
    prod = [mm_l(jnp.concatenate([x, y_], axis=0), jnp.concatenate([z, q_], axis=0), _NT)
            for x, y_, z, q_ in zip(a_st, r_st, b4, k4)]
    low = _iota(sq, 1) < HEAD_DIM

    def diag_blocks(x, mask):
        xr = pltpu.roll(x, HEAD_DIM, axis=1)
        return jnp.where(mask, jnp.where(low, x, xr), 0.0), jnp.where(mask, jnp.where(low, xr, x), 0.0)

    l_bk = [diag_blocks(x[:st], strict) for x in prod]
    m_bk = [diag_blocks(x[st:], incl) for x in prod]
    l_b, l_k = [x[0] for x in l_bk], [nar(x[1]) for x in l_bk]
    m_b, m_k = [nar(x[0]) for x in m_bk], [nar(x[1]) for x in m_bk]
    t = [nar(eye + x) for x in l_b]
    l_bn = [nar(x) for x in l_b]
    pw = [nar(mm_i(x, x)) for x in l_bn]
    cov = 2
    while cov < C:
        if 2 * cov >= C:
            t = [nar(x + mm_i(p_, x)) for x, p_ in zip(t, pw)]
        else:
            both = [mm_i(p_, jnp.concatenate([p_, x], axis=1)) for x, p_ in zip(t, pw)]
            pw = [nar(x[:, :st]) for x in both]
            t = [nar(x + y_[:, st:]) for x, y_ in zip(t, both)]
        cov *= 2
    lm_v = [mm_a(jnp.concatenate([x, y_], axis=0), z) for x, y_, z in zip(l_k, m_k, v_st)]
    mkv4 = [fold(x[st:]) for x in lm_v]
    t_av = [mm_a(x, jnp.concatenate([y_, nar(z[:st])], axis=1)) for x, y_, z in zip(t, a_st, lm_v)]
    wm4 = [fold(x[:, :st]) for x in t_av]
    uk4 = [fold(x[:, st:]) for x in t_av]

    us = [[None] * nb for _ in gs]
    ys = [[None] * nb for _ in gs]
    for ck in range(ncs):
        cgs = range(ck * gpb, (ck + 1) * gpb)
        for b in range(nb):
            rows = slice(b * C, (b + 1) * C)
            s_old = {gi: sbd_ref[b, gi - ck * gpb] for gi in cgs}
            uy = {gi: mm_s(jnp.concatenate([wm4[gi][rows], r4[gi][rows]], axis=0), s_old[gi], _NT) for gi in cgs}
            for gi in cgs:
                us[gi][b] = uy[gi][:C] + uk4[gi][rows]
                ys[gi][b] = uy[gi][C:]
            ds = {gi: mm_s(jnp.concatenate([us[gi][b], vg[gi][rows]], axis=0),
                           jnp.concatenate([b4[gi][rows], k4[gi][rows]], axis=0), _TN) for gi in cgs}
            for gi in cgs:
                s_new = (s_old[gi] + jnp.where(same_head, ds[gi], 0.0)) * gam[gi][(b + 1) * C - 1:(b + 1) * C, :]
                sbd_ref[b, gi - ck * gpb] = s_new
                if ck == ncs - 1:
                    @pl.when(c == nsteps - 1)
                    def _fin():
                        hs = slice((gi - ck * gpb) * HEADS_PER_GROUP, (gi - ck * gpb + 1) * HEADS_PER_GROUP)
                        sout_ref[b, hs] = from_block_diag(s_new).reshape(HEADS_PER_GROUP, HEAD_DIM, HEAD_DIM)

    u = [x[0] if nb == 1 else jnp.concatenate(x, axis=0) for x in us]
    ysum = [x[0] if nb == 1 else jnp.concatenate(x, axis=0) for x in ys]
    yb = [fold(mm_s(x, stack(y_))) for x, y_ in zip(m_b, u)]
    y = [x + y_ + z for x, y_, z in zip(ysum, mkv4, yb)]

    ym = [_dot_xr(x, block_ones, passes=ps["gn"]) * (1.0 / HEAD_DIM) for x in y]
    yc = [x - y_ for x, y_ in zip(y, ym)]
    yv = [_dot_xr(x * x, block_ones, passes=ps["gn"]) * (1.0 / HEAD_DIM) for x in yc]
    bonus = [_dot_xr(rg[gi] * k2[gi] * rk_ref[:, sls[gi]], block_ones, passes=ps["gn"]) * vg[gi] for gi in gs]
    for gi in gs:
        sl, rw = sls[gi], rws[gi]
        yn = yc[gi] * lax.rsqrt(yv[gi] + GN_EPS) * gng_ref[:, sl] + gnb_ref[:, sl]
        o_ref[rw, sl] = (yn + bonus[gi]) * gate_all[rw, sl]


def _rwkv(proj, nseq, nchunk, C, prev_shift, state0, prm, gpb=None, ncs=None):
    nb = ROWS // C
    if gpb is None:
        gpb = N_GROUPS if nb == 1 else N_GROUPS // 2
    if ncs is None:
        ncs = 2 if nb == 1 and nchunk % 2 == 0 else 1
    assert nb * C == ROWS and nseq % nb == 0 and (nb == 1 or nchunk == 1) and nchunk % ncs == 0
    rows = nseq * nchunk * C
    nsteps = nchunk // ncs
    srows = ncs * ROWS
    w = gpb * GROUP_W
    sec = D_MODEL // w
    prows = 8 if nb == 1 else C
    pblk = 8 if nb == 1 else srows
    prev = jnp.zeros((nseq, prows, SHIFT_W), F32).at[:, 0].set(prev_shift).reshape(nseq * prows, SHIFT_W)
    prev_rkv = prev[:, :RKV_W]
    prev_lora = jnp.pad(prev[:, RKV_W:], ((0, 0), (0, LORA_PAD - LORA_W)))

    def tok(off):
        return pl.BlockSpec((srows, w), lambda s, g, c: (s * nsteps + c, off + g))

    def prv(off):
        return pl.BlockSpec((pblk, w), lambda s, g, c: (s, off + g))

    def vec(off=0):
        return pl.BlockSpec((1, w), lambda s, g, c: (0, off + g))

    up = pl.BlockSpec((LORA_PAD, w), lambda s, g, c: (0, g))
    in_specs = [
        tok(0), tok(sec), tok(2 * sec),
        pl.BlockSpec((srows, LORA_PAD), lambda s, g, c: (s * nsteps + c, COL_LORA // LORA_PAD)),
        prv(0), prv(sec), prv(2 * sec),
        pl.BlockSpec((pblk, LORA_PAD), lambda s, g, c: (s, 0)),
        pl.BlockSpec((nb, gpb * HEADS_PER_GROUP, HEAD_DIM, HEAD_DIM), lambda s, g, c: (s, g, 0, 0)),
        vec(0), vec(sec), vec(2 * sec),
        pl.BlockSpec((1, LORA_PAD), lambda s, g, c: (0, 0)),
        vec(), vec(), vec(), vec(), vec(), vec(), vec(),
        up, up, up,
    ]
    out_specs = [pl.BlockSpec((srows, w), lambda s, g, c: (s * nsteps + c, g)),
                 pl.BlockSpec((nb, gpb * HEADS_PER_GROUP, HEAD_DIM, HEAD_DIM), lambda s, g, c: (s, g, 0, 0))]
    out_shape = [jax.ShapeDtypeStruct((rows, D_MODEL), F32),
                 jax.ShapeDtypeStruct((nseq, D_MODEL // HEAD_DIM, HEAD_DIM, HEAD_DIM), F32)]
    operands = [proj, proj, proj, proj, prev_rkv, prev_rkv, prev_rkv, prev_lora, state0,
                prm["mu_rkv"], prm["mu_rkv"], prm["mu_rkv"], prm["mu_lora"],
                prm["w0"], prm["a0"], prm["k_k"], prm["k_a"], prm["r_k"], prm["gn_g"], prm["gn_b"],
                prm["w_up"], prm["a_up"], prm["g_up"]]
    return pl.pallas_call(
        functools.partial(_rwkv_body, C, nb, ncs, nsteps, gpb, RWKV_PASSES),
        grid=(nseq // nb, N_GROUPS // gpb, nsteps),
        in_specs=in_specs,
        out_specs=out_specs,
        out_shape=out_shape,
        scratch_shapes=[pltpu.VMEM((nb, gpb, GROUP_W, GROUP_W), F32),
                        pltpu.VMEM((srows + 8, 3 * w), F32),
                        pltpu.VMEM((srows + 8, LORA_PAD), F32)],
        compiler_params=_params(("arbitrary", "arbitrary", "arbitrary")),
        name="rwkv",
    )(*operands)


def _attn_phases(q, kcat, vcat, sink_ref, first_key, store, npr):
    m = q.shape[0]
    kt_all = kcat.T
    qi = _iota((npr * m, 2 * WINDOW), 0) & (m - 1)
    ki = _iota((npr * m, 2 * WINDOW), 1)
    allowed = (ki >= qi + 1) & (ki <= qi + WINDOW) & (ki >= first_key)
    lane = _iota((2 * WINDOW, 128), 1)
    ones_k = jnp.ones((2 * WINDOW, 128), BF16)
    zk = jnp.zeros((HEAD_DIM, 2 * WINDOW), F32)
    nkv = KV_W // HEAD_DIM
    k2s, v2s = [], []
    for kv in range(nkv):
        kt = kt_all[kv * HEAD_DIM:(kv + 1) * HEAD_DIM, :]
        k2s.append(jnp.concatenate([jnp.concatenate([kt, zk], axis=1),
                                    jnp.concatenate([zk, kt], axis=1)], axis=0).astype(BF16))
        slab = vcat[:, (kv // 2) * 128:(kv // 2 + 1) * 128]
        rolled = pltpu.roll(slab, HEAD_DIM, axis=1)
        lo_src, hi_src = (slab, rolled) if kv % 2 == 0 else (rolled, slab)
        v2s.append(jnp.concatenate([jnp.where(lane < HEAD_DIM, lo_src, 0.0),
                                    jnp.where(lane >= HEAD_DIM, hi_src, 0.0)], axis=0).astype(BF16))
    items = [(kv, p0) for kv in range(nkv) for p0 in range(0, 4, npr)]
    cols = [[kv * 512 + (p0 + i) * 128 for i in range(npr)] for kv, p0 in items]
    scores, probs = [], [None] * len(items)

    def score_phase():
        for (kv, _), cs in zip(items, cols):
            scores.append(_bdot((jnp.concatenate([q[:, col:col + 128] for col in cs], axis=0)
                                 * ATT_SCALE).astype(BF16), k2s[kv]))

    def softmax_phase(lo, hi):
        for it in range(lo, hi):
            (kv, p0), s = items[it], scores[it]
            ps = []
            for hb in range(2):
                sink = jnp.concatenate(
                    [jnp.broadcast_to(sink_ref[:, (kv * 8 + (p0 + i) * 2 + hb) * 128:
                                               (kv * 8 + (p0 + i) * 2 + hb) * 128 + 1], (m, 1))
                     for i in range(npr)], axis=0)
                sh = jnp.where(allowed, s[:, hb * 256:(hb + 1) * 256], -jnp.inf)
                mx = jnp.maximum(jnp.max(sh, axis=1, keepdims=True), sink)
                e = jnp.exp(sh - mx)
                inv = 1.0 / (_dot_xr(e, ones_k, passes=2) + jnp.exp(sink - mx))
                ps += [e[:, :128] * inv, e[:, 128:] * inv]
            probs[it] = jnp.concatenate(ps, axis=1).astype(BF16)

    def value_phase():
        outs = [_bdot(p, v2s[kv]) for (kv, _), p in zip(items, probs)]
        for cs, o in zip(cols, outs):
            for i, col in enumerate(cs):
                store(col, o[i * m:(i + 1) * m])

    quarter = max(len(items) // 4, 1)
    return ([score_phase]
            + [functools.partial(softmax_phase, lo, min(lo + quarter, len(items)))
               for lo in range(0, len(items), quarter)]
            + [value_phase])


def _attn_rows(q, kcat, vcat, sink_ref, first_key, store, npr):
    for phase in _attn_phases(q, kcat, vcat, sink_ref, first_key, store, npr):
        phase()


def _attn_prompt_body(q_ref, kvc_ref, kvp_ref, sink_ref, o_ref):
    n = pl.program_id(0)
    kvc = kvc_ref[...]
    kvp = kvp_ref[...]
    kcat = jnp.concatenate([kvp[:, :KV_W], kvc[:, :KV_W]], axis=0)
    vcat = jnp.concatenate([kvp[:, KV_W:], kvc[:, KV_W:]], axis=0)

    def store(col, val):
        o_ref[:, col:col + 128] = val

    _attn_rows(q_ref[...], kcat, vcat, sink_ref, jnp.where(n > 0, 0, WINDOW), store, npr=1)


def _attn_prompt(proj, sinks_e):
    t = proj.shape[0]
    nb = t // WINDOW
    qb, kvb = COL_Q // D_MODEL, COL_KV // (2 * KV_W)
    return pl.pallas_call(
        _attn_prompt_body,
        grid=(nb,),
        in_specs=[pl.BlockSpec((WINDOW, D_MODEL), lambda n: (n, qb)),
                  pl.BlockSpec((WINDOW, 2 * KV_W), lambda n: (n, kvb)),
                  pl.BlockSpec((WINDOW, 2 * KV_W), lambda n: (jnp.maximum(n - 1, 0), kvb)),
                  pl.BlockSpec((1, 32 * 128), lambda n: (0, 0))],
        out_specs=pl.BlockSpec((WINDOW, D_MODEL), lambda n: (n, 0)),
        out_shape=jax.ShapeDtypeStruct((t, D_MODEL), F32),
        compiler_params=_params(("parallel",)),
        name="attn_prompt",
    )(proj, proj, proj, sinks_e)


def _attn_sample_body(L, bt, q_ref, kvn_ref, ck_ref, cv_ref, sink_ref, o_ref):
    pad = jnp.zeros((WINDOW - L, KV_W), F32)

    def one(b, carry):
        rows = pl.ds(pl.multiple_of(b * L, L), L)
        kvn = kvn_ref[rows, :]
        kcat = jnp.concatenate([ck_ref[b], kvn[:, :KV_W], pad], axis=0)
        vcat = jnp.concatenate([cv_ref[b], kvn[:, KV_W:], pad], axis=0)

        def store(col, val):
            o_ref[rows, col:col + 128] = val

        _attn_rows(q_ref[rows, :], kcat, vcat, sink_ref, 0, store, npr=4)
        return carry

    lax.fori_loop(0, bt, one, 0, unroll=2)


def _attn_sample(proj, ck, cv, sinks_e, L, bt=16):
    rows = proj.shape[0]
    nb = rows // L
    bt = min(bt, nb)
    qb, kvb = COL_Q // D_MODEL, COL_KV // (2 * KV_W)
    return pl.pallas_call(
        functools.partial(_attn_sample_body, L, bt),
        grid=(nb // bt,),
        in_specs=[pl.BlockSpec((bt * L, D_MODEL), lambda i: (i, qb)),
                  pl.BlockSpec((bt * L, 2 * KV_W), lambda i: (i, kvb)),
                  pl.BlockSpec((bt, WINDOW, KV_W), lambda i: (i, 0, 0)),
                  pl.BlockSpec((bt, WINDOW, KV_W), lambda i: (i, 0, 0)),
                  pl.BlockSpec((1, 32 * 128), lambda i: (0, 0))],
        out_specs=pl.BlockSpec((bt * L, D_MODEL), lambda i: (i, 0)),
        out_shape=jax.ShapeDtypeStruct((rows, D_MODEL), F32),
        compiler_params=_params(("parallel",)),
        name="attn_sample",
    )(proj, proj, ck, cv, sinks_e)


def _layer_norm(x, g, b):
    mu = jnp.mean(x, axis=-1, keepdims=True)
    xc = x - mu
    var = jnp.mean(xc * xc, axis=-1, keepdims=True)
    return xc * lax.rsqrt(var + LN_EPS) * g + b


def _merge_body(ga_ref, gb_ref, oa_ref, ob_ref, x_ref, wout_ref, g_ref, b_ref, h_ref, ht_ref):
    mixed = _sigmoid(ga_ref[...]) * oa_ref[...] + _sigmoid(gb_ref[...]) * ob_ref[...]
    y = _bdot(mixed.astype(BF16), wout_ref[...])
    h = _layer_norm(DN_ALPHA * x_ref[...] + y, g_ref[...], b_ref[...])
    h_ref[...] = h
    ht_ref[...] = h.T.astype(BF16)


def _merge(proj, o_a, o_b, x2d, wout_bf16, ln_g, ln_b, tm=256):
    m = x2d.shape[0]
    tm = min(tm, m)
    gab = COL_GATE // D_MODEL
    row = lambda i: (i, 0)
    return pl.pallas_call(
        _merge_body,
        grid=(m // tm,),
        in_specs=[pl.BlockSpec((tm, D_MODEL), lambda i: (i, gab)),
                  pl.BlockSpec((tm, D_MODEL), lambda i: (i, gab + 1)),
                  pl.BlockSpec((tm, D_MODEL), row),
                  pl.BlockSpec((tm, D_MODEL), row),
                  pl.BlockSpec((tm, D_MODEL), row),
                  pl.BlockSpec((D_MODEL, D_MODEL), lambda i: (0, 0)),
                  pl.BlockSpec((1, D_MODEL), lambda i: (0, 0)),
                  pl.BlockSpec((1, D_MODEL), lambda i: (0, 0))],
        out_specs=[pl.BlockSpec((tm, D_MODEL), row),
                   pl.BlockSpec((D_MODEL, tm), lambda i: (0, i))],
        out_shape=[jax.ShapeDtypeStruct((m, D_MODEL), F32),
                   jax.ShapeDtypeStruct((D_MODEL, m), BF16)],
        compiler_params=_params(("parallel",)),
        name="merge",
    )(proj, proj, o_a, o_b, x2d, wout_bf16, ln_g, ln_b)


def _top_rows(x, count):
    return _top_rows_multi([x], count)[0]


def _top_rows_multi(xs, count):
    rows = _iota(xs[0].shape, 0)
    big = xs[0].shape[0]
    outs = [[] for _ in xs]
    for _ in range(count):
        mxs = [jnp.max(x, axis=0, keepdims=True) for x in xs]
        for o, mx in zip(outs, mxs):
            o.append(mx)
        firsts = [jnp.min(jnp.where(x == mx, rows, big), axis=0, keepdims=True) for x, mx in zip(xs, mxs)]
        xs = [jnp.where(rows == f, -jnp.inf, x) for x, f in zip(xs, firsts)]
    return outs


def _bitonic_network(n):
    comps = []
    k = 2
    while k <= n:
        j = k // 2
        while j >= 1:
            comps += [(i, i ^ j, (i & k) == 0) for i in range(n) if (i ^ j) > i]
            j //= 2
        k *= 2
    return comps


def _top16_of_128(xs):
    nt = N_KEYS // 8
    tiles = [[x[8 * j:8 * (j + 1)] for j in range(nt)] for x in xs]
    for i, l, desc in _bitonic_network(nt):
        for t in tiles:
            hi, lo = jnp.maximum(t[i], t[l]), jnp.minimum(t[i], t[l])
            t[i], t[l] = (hi, lo) if desc else (lo, hi)
    sub = _iota(tiles[0][0].shape, 0)
    outs = [[] for _ in xs]
    for r in range(PEER_TOPK):
        for t, o in zip(tiles, outs):
            mx = jnp.max(t[0], axis=0, keepdims=True)
            o.append(mx)
            first = jnp.min(jnp.where(t[0] == mx, sub, 8), axis=0, keepdims=True)
            pop = sub == first
            for j in range(PEER_TOPK - r - 1):
                t[j] = jnp.where(pop, t[j + 1], t[j])
    return outs


def _route_body(tm, h_ref, wq_ref, keys_ref, s0_ref, s1_ref, aux_ref, q_s):
    q = _bdot(h_ref[...].astype(BF16), wq_ref[...])
    for hc in range(2 * PEER_HEADS):
        q_s[hc] = q[:, hc * N_KEYS:(hc + 1) * N_KEYS]
    pairs = [(i, j) for i in range(PEER_TOPK) for j in range(PEER_TOPK) if (i + 1) * (j + 1) <= PEER_TOPK]
    npad = -len(pairs) % 8
    neg = jnp.full((1, tm), -jnp.inf, F32)
    zero = jnp.zeros((1, tm), F32)

    def head(hh, carry):
        s0 = _bdot(keys_ref[hh, 0], q_s[2 * hh].astype(BF16), _NT)
        s1 = _bdot(keys_ref[hh, 1], q_s[2 * hh + 1].astype(BF16), _NT)
        s0_ref[hh] = s0
        for lb in range(tm // 128):
            s1_ref[hh, lb] = s1[:, lb * 128:(lb + 1) * 128]
        top0, top1 = _top16_of_128([s0, s1])
        cand = jnp.concatenate([top0[i] + top1[j] for i, j in pairs] + [neg] * npad, axis=0)
        best = _top_rows(cand, PEER_TOPK)
        mx = best[0]
        z = zero
        for bsum in best:
            z = z + jnp.exp(bsum - mx)
        aux_ref[hh] = jnp.concatenate([best[-1], top0[0], top1[0], 1.0 / z, zero, zero, zero, zero], axis=0)
        return carry

    lax.fori_loop(0, PEER_HEADS, head, 0)


def _route(h, wq_bf16, keys_bf16, tm=512):
    m = h.shape[0]
    tm = min(tm, m)
    tok3 = lambda i: (0, 0, i)
    return pl.pallas_call(
        functools.partial(_route_body, tm),
        grid=(m // tm,),
        in_specs=[pl.BlockSpec((tm, D_MODEL), lambda i: (i, 0)),
                  pl.BlockSpec((D_MODEL, D_MODEL), lambda i: (0, 0)),
                  pl.BlockSpec((PEER_HEADS, 2, N_KEYS, N_KEYS), lambda i: (0, 0, 0, 0))],
        out_specs=[pl.BlockSpec((PEER_HEADS, N_KEYS, tm), tok3),
                   pl.BlockSpec((PEER_HEADS, tm // 128, N_KEYS, 128), lambda i: (0, i, 0, 0)),
                   pl.BlockSpec((PEER_HEADS, 8, tm), tok3)],
        out_shape=[jax.ShapeDtypeStruct((PEER_HEADS, N_KEYS, m), F32),
                   jax.ShapeDtypeStruct((PEER_HEADS, m // 128, N_KEYS, 128), F32),
                   jax.ShapeDtypeStruct((PEER_HEADS, 8, m), F32)],
        scratch_shapes=[pltpu.VMEM((2 * PEER_HEADS, tm, N_KEYS), F32)],
        compiler_params=_params(("parallel",)),
        name="peer_route",
    )(h, wq_bf16, keys_bf16)


def _peer_body(te, ne, ht_ref, u_ref, vt_ref, s0_ref, s1_ref, aux_ref, h_ref, g_ref, b_ref, y_ref,
               acc_ref, e1_ref, cf_ref, st0_ref, st1_ref, w0_ref, w1_ref):
    j = pl.program_id(1)
    tm = acc_ref.shape[1]

    @pl.when(j == 0)
    def _():
        acc_ref[...] = jnp.zeros_like(acc_ref)
        for ref in (st0_ref, st1_ref, w0_ref, w1_ref):
            ref[...] = jnp.zeros_like(ref)
        for hh in range(PEER_HEADS):
            aux = aux_ref[hh]
            for lb in range(tm // 128):
                e1_ref[hh, lb] = jnp.exp(s1_ref[hh, lb] - aux[2:3, lb * 128:(lb + 1) * 128])
            cf_ref[hh] = jnp.exp(s0_ref[hh] - aux[1:2, :]) * aux[3:4, :]

    def stages(st_new, st_old, w_new, w_old):
        jb = j - 1
        live = (jb >= 0) & (jb < ne)
        jb_c = jnp.clip(jb, 0, ne - 1)
        taus = [jnp.where(live, aux_ref[hh, 0:1, :], jnp.inf) for hh in range(PEER_HEADS)]
        ncc = te // N_KEYS
        s0rows = [[s0_ref[hh, pl.ds(jb_c * ncc + cc, 1), :] for hh in range(PEER_HEADS)] for cc in range(ncc)]
        cfrows = [[cf_ref[hh, pl.ds(jb_c * ncc + cc, 1), :] for hh in range(PEER_HEADS)] for cc in range(ncc)]
        tw = min(256, tm)
        mxu_pieces, gate_tiles = [], []
        for t0 in range(0, tm, tw):
            ts = slice(t0, t0 + tw)
            lbs = range(t0 // 128, (t0 + tw) // 128)

            def mix(mr, ts=ts, lbs=lbs):
                acc_ref[mr, ts] += _bdot(vt_ref[mr, :], jnp.concatenate([w_old[lb] for lb in lbs], axis=1))

            def score(er, t0=t0, ts=ts, lbs=lbs):
                st = _bdot(u_ref[er, :], ht_ref[:, ts])
                for lb in lbs:
                    st_new[lb, er, :] = st[:, lb * 128 - t0:(lb + 1) * 128 - t0]

            for q in range(4):
                mxu_pieces.append(functools.partial(mix, slice(q * (D_MODEL // 4), (q + 1) * (D_MODEL // 4))))
            for q in range(2):
                mxu_pieces.append(functools.partial(score, slice(q * (te // 2), (q + 1) * (te // 2))))

            def gate(lb, cc):
                ls = slice(lb * 128, (lb + 1) * 128)
                rows = slice(cc * N_KEYS, (cc + 1) * N_KEYS)
                gsum = None
                for hh in range(PEER_HEADS):
                    sel = (s1_ref[hh, lb] + s0rows[cc][hh][:, ls]) >= taus[hh][:, ls]
                    term = jnp.where(sel, e1_ref[hh, lb] * cfrows[cc][hh][:, ls], 0.0)
                    gsum = term if gsum is None else gsum + term
                so = st_old[lb, rows, :]
                act = 0.5 * so * (1.0 + lax.erf(so * np.float32(np.sqrt(0.5))))
                w_new[lb, rows, :] = (gsum * act).astype(BF16)

            for lb in lbs:
                for cc in range(ncc):
                    gate_tiles.append(functools.partial(gate, lb, cc))
        for i in range(max(len(mxu_pieces), len(gate_tiles))):
            if i < len(gate_tiles):
                gate_tiles[i]()
            if i < len(mxu_pieces):
                mxu_pieces[i]()

    @pl.when(j % 2 == 0)
    def _():
        stages(st0_ref, st1_ref, w1_ref, w0_ref)

    @pl.when(j % 2 == 1)
    def _():
        stages(st1_ref, st0_ref, w0_ref, w1_ref)

    @pl.when(j == ne + 1)
    def _():
        out = acc_ref[...].T
        y_ref[...] = _layer_norm(DN_ALPHA * h_ref[...] + out, g_ref[...], b_ref[...])


def _peer(ht, u_bf16, vt_bf16, s0t, s1t, aux, h, ln_g, ln_b, tm=512, te=512):
    m = h.shape[0]
    tm = min(tm, m)
    ne = N_EXPERTS // te
    tok3 = lambda i, j: (0, 0, i)
    return pl.pallas_call(
        functools.partial(_peer_body, te, ne),
        grid=(m // tm, ne + 2),
        in_specs=[pl.BlockSpec((D_MODEL, tm), lambda i, j: (0, i)),
                  pl.BlockSpec((te, D_MODEL), lambda i, j: (jnp.minimum(j, ne - 1), 0)),
                  pl.BlockSpec((D_MODEL, te), lambda i, j: (0, jnp.clip(j - 2, 0, ne - 1))),
                  pl.BlockSpec((PEER_HEADS, N_KEYS, tm), tok3),
                  pl.BlockSpec((PEER_HEADS, tm // 128, N_KEYS, 128), lambda i, j: (0, i, 0, 0)),
                  pl.BlockSpec((PEER_HEADS, 8, tm), tok3),
                  pl.BlockSpec((tm, D_MODEL), lambda i, j: (i, 0)),
                  pl.BlockSpec((1, D_MODEL), lambda i, j: (0, 0)),
                  pl.BlockSpec((1, D_MODEL), lambda i, j: (0, 0))],
        out_specs=pl.BlockSpec((tm, D_MODEL), lambda i, j: (i, 0)),
        out_shape=jax.ShapeDtypeStruct((m, D_MODEL), F32),
        scratch_shapes=[pltpu.VMEM((D_MODEL, tm), F32),
                        pltpu.VMEM((PEER_HEADS, tm // 128, N_KEYS, 128), F32),
                        pltpu.VMEM((PEER_HEADS, N_KEYS, tm), F32),
                        pltpu.VMEM((tm // 128, te, 128), F32),
                        pltpu.VMEM((tm // 128, te, 128), F32),
                        pltpu.VMEM((tm // 128, te, 128), BF16),
                        pltpu.VMEM((tm // 128, te, 128), BF16)],
        compiler_params=_params(("parallel", "arbitrary")),
        name="peer_dense",
    )(ht, u_bf16, vt_bf16, s0t, s1t, aux, h, ln_g, ln_b)


def _pad_rows(w, lo, total):
    return jnp.zeros((total, w.shape[1]), w.dtype).at[lo:lo + w.shape[0]].set(w)


def _layer(x2d, nseq, seq_len, chunk, prev_shift, state0, cache, wts):
    proj_a = _inproj(x2d, wts["w_in_t"], PROJ_A_W)
    proj_b = _inproj(x2d, wts["w_in_b"], PROJ_B_W)
    o_a, s_last = _rwkv(proj_a, nseq, seq_len // chunk, chunk, prev_shift, state0, wts)
    if cache is None:
        o_b = _attn_prompt(proj_b, wts["sinks"])
    else:
        o_b = _attn_sample(proj_b, cache[0], cache[1], wts["sinks"], seq_len)
    h, ht = _merge(proj_b, o_a, o_b, x2d, wts["w_out"], wts["ln1_g"], wts["ln1_b"])
    s0t, s1t, aux = _route(h, wts["peer_wq"], wts["peer_keys"])
    y = _peer(ht, wts["peer_u"], wts["peer_vt"], s0t, s1t, aux, h, wts["ln2_g"], wts["ln2_b"])
    return y, proj_a, proj_b, s_last


def kernel(x_prompt, x_sample, state_shift, state_wkv, cache_k, cache_v, w_in, rw_mu, rw_w0, rw_w_up, rw_a0,
           rw_a_up, rw_g_up, rw_k_k, rw_k_a, rw_r_k, rw_gn_g, rw_gn_b, att_sinks, w_out, ln1_g, ln1_b,
           peer_wq, peer_keys, peer_u, peer_v, ln2_g, ln2_b):
    depth = w_in.shape[0]
    assert depth == 1
    l = 0
    bp, tp, _ = x_prompt.shape
    bs, ts, _ = x_sample.shape
    assert bp == 1
    win = cache_k.shape[2]
    assert win == WINDOW and tp % WINDOW == 0 and ts <= 8

    w_t = jnp.transpose(w_in[l]).astype(BF16)
    q0 = SHIFT_W
    k0 = q0 + D_MODEL
    g0 = k0 + 2 * KV_W
    w_b = jnp.concatenate([w_t[q0:q0 + D_MODEL], w_t[g0:g0 + 2 * D_MODEL], w_t[k0:k0 + 2 * KV_W]], axis=0)
    mu = rw_mu[l]
    row = lambda v: v.reshape(1, -1)
    wts = dict(
        w_in_t=w_t, w_in_b=w_b,
        mu_rkv=row(mu[:RKV_W]),
        mu_lora=row(jnp.pad(mu[RKV_W:], (0, LORA_PAD - LORA_W))),
        w0=row(rw_w0[l]), a0=row(rw_a0[l]), k_k=row(rw_k_k[l]), k_a=row(rw_k_a[l]), r_k=row(rw_r_k[l]),
        gn_g=row(rw_gn_g[l]), gn_b=row(rw_gn_b[l]),
        w_up=_pad_rows(rw_w_up[l], 0, LORA_PAD).astype(BF16),
        a_up=_pad_rows(rw_a_up[l], 96, LORA_PAD).astype(BF16),
        g_up=_pad_rows(rw_g_up[l], 192, LORA_PAD).astype(BF16),
        sinks=jnp.repeat(att_sinks[l], 128).reshape(1, 32 * 128),
        w_out=w_out[l].astype(BF16),
        ln1_g=row(ln1_g[l]), ln1_b=row(ln1_b[l]), ln2_g=row(ln2_g[l]), ln2_b=row(ln2_b[l]),
        peer_wq=peer_wq[l].astype(BF16),
        peer_keys=peer_keys[l].astype(BF16),
        peer_u=peer_u[l].astype(BF16),
        peer_vt=peer_v[l].T.astype(BF16),
    )

    chunk_p = 64
    y_p, pa_p, pb_p, s_p = _layer(x_prompt[0], 1, tp, chunk_p,
                                  jnp.zeros((1, SHIFT_W), F32),
                                  jnp.zeros((1, D_MODEL // HEAD_DIM, HEAD_DIM, HEAD_DIM), F32), None, wts)
    keep = min(WINDOW, tp)
    kv_p = pb_p[tp - keep:, COL_KV:]
    y_prompt = y_p[None]
    new_shift_prompt = pa_p[tp - 1, :SHIFT_W][None, None]
    new_wkv_prompt = s_p[None]
    new_k_prompt = kv_p[:, :KV_W].reshape(1, 1, keep, KV_W // HEAD_DIM, HEAD_DIM)
    new_v_prompt = kv_p[:, KV_W:].reshape(1, 1, keep, KV_W // HEAD_DIM, HEAD_DIM)

    ck = cache_k[l].reshape(bs, win, KV_W)
    cv = cache_v[l].reshape(bs, win, KV_W)
    y_s, pa_s, pb_s, s_s = _layer(x_sample.reshape(bs * ts, D_MODEL), bs, ts, ts,
                                  state_shift[l], state_wkv[l], (ck, cv), wts)
    y_sample = y_s.reshape(bs, ts, D_MODEL)
    new_shift_sample = pa_s.reshape(bs, ts, PROJ_A_W)[:, ts - 1, :SHIFT_W][None]
    new_wkv_sample = s_s[None]
    kv_s = pb_s.reshape(bs, ts, PROJ_B_W)[:, :, COL_KV:]
    new_k_sample = jnp.concatenate([ck, kv_s[:, :, :KV_W]], axis=1)[:, ts:].reshape(
        1, bs, win, KV_W // HEAD_DIM, HEAD_DIM)
    new_v_sample = jnp.concatenate([cv, kv_s[:, :, KV_W:]], axis=1)[:, ts:].reshape(
        1, bs, win, KV_W // HEAD_DIM, HEAD_DIM)

    return (y_prompt, y_sample, new_shift_prompt, new_wkv_prompt, new_k_prompt, new_v_prompt,
            new_shift_sample, new_wkv_sample, new_k_sample, new_v_sample)
```

```python
import functools

import numpy as np
import jax
import jax.numpy as jnp
from jax import lax
from jax.experimental import pallas as pl
from jax.experimental.pallas import tpu as pltpu

F32, BF16 = jnp.float32, jnp.bfloat16

D_MODEL = 2048
HEAD_DIM = 64
HEADS_PER_GROUP = 2
GROUP_W = HEADS_PER_GROUP * HEAD_DIM
N_GROUPS = D_MODEL // GROUP_W
RKV_W = 3 * D_MODEL
LORA_W = 96 + 96 + 256
LORA_PAD = 512
SHIFT_W = RKV_W + LORA_W
KV_W = 256
GN_EPS = 64e-5
LN_EPS = 1e-5
WINDOW = 128
ATT_SCALE = HEAD_DIM ** -0.5
N_KEYS = 128
N_EXPERTS = N_KEYS * N_KEYS
PEER_HEADS = 8
PEER_TOPK = 16
DN_ALPHA = 2.0 ** 0.25

COL_LORA = RKV_W
PROJ_A_W = RKV_W + LORA_PAD
COL_Q = 0
COL_GATE = D_MODEL
COL_KV = 3 * D_MODEL
PROJ_B_W = COL_KV + 2 * KV_W

VMEM_LIMIT = 56 * 1024 * 1024

_NN = (((1,), (0,)), ((), ()))
_NT = (((1,), (1,)), ((), ()))
_TN = (((0,), (0,)), ((), ()))


def _bdot(a, b, dn=_NN):
    return lax.dot_general(a, b, dn, preferred_element_type=F32)


def _dot(a, b, dn=_NN):
    return _bdot(a.astype(BF16), b.astype(BF16), dn)


def _split(x, n):
    parts, r = [], x
    for i in range(n):
        p = r.astype(BF16)
        parts.append(p)
        if i + 1 < n:
            r = r - p.astype(F32)
    return parts


def _dot3(a, b, dn=_NN):
    a1, a2 = _split(a, 2)
    b1, b2 = _split(b, 2)
    return (_bdot(a1, b2, dn) + _bdot(a2, b1, dn)) + _bdot(a1, b1, dn)


def _dot_xl(a_bf16, b, dn=_NN, passes=3):
    out = None
    for part in reversed(_split(b, passes)):
        t = _bdot(a_bf16, part, dn)
        out = t if out is None else out + t
    return out


def _dot_xr(a, b_bf16, dn=_NN, passes=3):
    out = None
    for part in reversed(_split(a, passes)):
        t = _bdot(part, b_bf16, dn)
        out = t if out is None else out + t
    return out


def _mm(passes):
    return _dot if passes == 1 else _dot3


def _sigmoid(x):
    return 1.0 / (1.0 + jnp.exp(-x))


def _iota(shape, dim):
    return lax.broadcasted_iota(jnp.int32, shape, dim)


def _head_of(idx):
    return jnp.right_shift(idx, 6)


def _ones_where(mask):
    return jnp.where(mask, 1.0, 0.0).astype(BF16)


def _params(sem):
    return pltpu.CompilerParams(dimension_semantics=sem, vmem_limit_bytes=VMEM_LIMIT)


def _inproj_body(x_ref, w_ref, o_ref, xb_ref):
    @pl.when(pl.program_id(1) == 0)
    def _():
        xb_ref[...] = x_ref[...].astype(BF16)

    o_ref[...] = _bdot(xb_ref[...], w_ref[...].astype(BF16), _NT)


def _inproj(x2d, w_t, n, tn=1664):
    assert n % tn == 0 and tn % 128 == 0
    m = x2d.shape[0]
    tm = min(m, 1024)
    return pl.pallas_call(
        _inproj_body,
        grid=(m // tm, n // tn),
        in_specs=[pl.BlockSpec((tm, D_MODEL), lambda i, j: (i, 0)),
                  pl.BlockSpec((tn, D_MODEL), lambda i, j: (j, 0))],
        out_specs=pl.BlockSpec((tm, tn), lambda i, j: (i, j)),
        out_shape=jax.ShapeDtypeStruct((m, n), F32),
        scratch_shapes=[pltpu.VMEM((tm, D_MODEL), BF16)],
        compiler_params=_params(("parallel", "arbitrary")),
        name="inproj",
    )(x2d, w_t)


RWKV_PASSES = dict(seg=2, cum=3, lblk=1, inv=1, apply=1, state=1, gn=2)
ROWS = 64


def _rwkv_body(C, nb, ncs, nsteps, gpb, ps,
               r_ref, k_ref, v_ref, lo_ref, pr_ref, pk_ref, pv_ref, plo_ref, s0_ref,
               mur_ref, muk_ref, muv_ref, mulo_ref, w0_ref, a0_ref, kk_ref, ka_ref, rk_ref, gng_ref, gnb_ref,
               wup_ref, aup_ref, gup_ref,
               o_ref, sout_ref,
               sbd_ref, sh_ref, shl_ref):
    assert HEADS_PER_GROUP == 2 and HEADS_PER_GROUP * ROWS == GROUP_W and (ncs == 1 or nb == 1)
    c = pl.program_id(2)
    w = gpb * GROUP_W
    srows = ncs * ROWS
    log2c = C.bit_length() - 1
    st = HEADS_PER_GROUP * ROWS
    sq = (st, st)
    ri, ci = _iota(sq, 0), _iota(sq, 1)
    same_head = _head_of(ri) == _head_of(ci)
    tr, tc = ri & (ROWS - 1), ci & (ROWS - 1)
    blk = same_head & (jnp.right_shift(tr, log2c) == jnp.right_shift(tc, log2c))
    strict = blk & (tr > tc)
    incl = blk & (tr >= tc)
    eye = jnp.where(ri == ci, 1.0, 0.0)
    block_ones = _ones_where(same_head)
    r64, c64 = _iota((ROWS, ROWS), 0), _iota((ROWS, ROWS), 1)
    tri_seq = _ones_where((jnp.right_shift(r64, log2c) == jnp.right_shift(c64, log2c)) & (r64 >= c64))
    lane_head = _head_of(_iota((ROWS, GROUP_W), 1))
    first_head_rows = ri < HEAD_DIM

    def to_block_diag(s2):
        wide = jnp.concatenate([s2, jnp.zeros_like(s2)], axis=1)
        return jnp.where(first_head_rows, wide, pltpu.roll(wide, HEAD_DIM, axis=1))

    def from_block_diag(s):
        return jnp.where(first_head_rows, s, pltpu.roll(s, HEAD_DIM, axis=1))[:, :HEAD_DIM]
    mm_l, mm_i, mm_a, mm_s = _mm(ps["lblk"]), _mm(ps["inv"]), _mm(ps["apply"]), _mm(ps["state"])

    @pl.when(c == 0)
    def _init():
        if nb == 1:
            sh_ref[7:8, 0:w] = pr_ref[0:1, :]
            sh_ref[7:8, w:2 * w] = pk_ref[0:1, :]
            sh_ref[7:8, 2 * w:3 * w] = pv_ref[0:1, :]
            shl_ref[7:8, :] = plo_ref[0:1, :]
        for b in range(nb):
            for gi in range(gpb):
                s0 = s0_ref[b, gi * HEADS_PER_GROUP:(gi + 1) * HEADS_PER_GROUP].reshape(GROUP_W, HEAD_DIM)
                sbd_ref[b, gi] = to_block_diag(s0)

    def lerp(x, ref, lo, hi, prev_ref, mu):
        ref[pl.ds(8, srows), lo:hi] = x
        xs = ref[pl.ds(7, srows), lo:hi]
        ref[7:8, lo:hi] = x[srows - 1:srows, :]
        if nb > 1:
            first = (_iota(x.shape, 0) & (C - 1)) == 0
            xs = jnp.where(first, prev_ref[...], xs)
        return x + (xs - x) * mu

    r = lerp(r_ref[...], sh_ref, 0, w, pr_ref, mur_ref[...])
    k = lerp(k_ref[...], sh_ref, w, 2 * w, pk_ref, muk_ref[...])
    v = lerp(v_ref[...], sh_ref, 2 * w, 3 * w, pv_ref, muv_ref[...])
    lo = lerp(lo_ref[...], shl_ref, 0, LORA_PAD, plo_ref, mulo_ref[...])

    wl = w0_ref[...] + _dot(jnp.tanh(lo), wup_ref[...])
    w_log = -(jnp.maximum(-wl, 0.0) + jnp.log1p(jnp.exp(-jnp.abs(wl)))) - 0.5
    logw_all = -jnp.exp(w_log)
    a_all = _sigmoid(a0_ref[...] + _dot(lo, aup_ref[...]))
    gate_all = _dot(_sigmoid(lo), gup_ref[...])
    kk_all = k * kk_ref[...]
    k2_all = k * (1.0 + (a_all - 1.0) * ka_ref[...])

    def stack(x):
        return jnp.concatenate([jnp.where(lane_head == j, x, 0.0) for j in range(HEADS_PER_GROUP)], axis=0)

    def fold(x):
        return x[0:ROWS] + x[ROWS:2 * ROWS]

    gs = range(ncs * gpb)
    sls = [slice(gi * GROUP_W, (gi + 1) * GROUP_W) for _ in range(ncs) for gi in range(gpb)]
    rws = [slice(ck * ROWS, (ck + 1) * ROWS) for ck in range(ncs) for _ in range(gpb)]
    rg = [r[rw, sl] for rw, sl in zip(rws, sls)]
    vg = [v[rw, sl] for rw, sl in zip(rws, sls)]
    k2 = [k2_all[rw, sl] for rw, sl in zip(rws, sls)]
    kk = [kk_all[rw, sl] for rw, sl in zip(rws, sls)]
    ss = [_dot_xr(x * x, block_ones, passes=ps["seg"]) for x in kk]
    kkn = [x / jnp.maximum(jnp.sqrt(s2), 1e-12) for x, s2 in zip(kk, ss)]
    logw = [logw_all[rw, sl] for rw, sl in zip(rws, sls)]
    cum = [_dot_xl(tri_seq, x, passes=ps["cum"]) for x in logw]
    gam = [jnp.exp(x) for x in cum]
    ginv = [jnp.exp(-x) for x in cum]
    a4 = [-(jnp.exp(cm - lw) * kn) for cm, lw, kn in zip(cum, logw, kkn)]
    b4 = [kn * a_all[rw, sl] * gi_ for kn, rw, sl, gi_ in zip(kkn, rws, sls, ginv)]
    k4 = [x * gi_ for x, gi_ in zip(k2, ginv)]
    r4 = [gm * x for gm, x in zip(gam, rg)]

    single = all(ps[c_] == 1 for c_ in ("lblk", "inv", "apply", "state"))
    nar = (lambda x: x.astype(BF16)) if single else (lambda x: x)
    a_st = [nar(stack(x)) for x in a4]
    r_st = [nar(stack(x)) for x in r4]
    v_st = [nar(stack(x)) for x in vg]
    prod = [mm_l(jnp.concatenate([x, y_], axis=0), jnp.concatenate([z, q_], axis=0), _NT)
            for x, y_, z, q_ in zip(a_st, r_st, b4, k4)]
    low = _iota(sq, 1) < HEAD_DIM

    def diag_blocks(x, mask):
        xr = pltpu.roll(x, HEAD_DIM, axis=1)
        return jnp.where(mask, jnp.where(low, x, xr), 0.0), jnp.where(mask, jnp.where(low, xr, x), 0.0)

    l_bk = [diag_blocks(x[:st], strict) for x in prod]
    m_bk = [diag_blocks(x[st:], incl) for x in prod]
    l_b, l_k = [x[0] for x in l_bk], [nar(x[1]) for x in l_bk]
    m_b, m_k = [nar(x[0]) for x in m_bk], [nar(x[1]) for x in m_bk]
    t = [nar(eye + x) for x in l_b]
    l_bn = [nar(x) for x in l_b]
    pw = [nar(mm_i(x, x)) for x in l_bn]
    cov = 2
    while cov < C:
        if 2 * cov >= C:
            t = [nar(x + mm_i(p_, x)) for x, p_ in zip(t, pw)]
        else:
            both = [mm_i(p_, jnp.concatenate([p_, x], axis=1)) for x, p_ in zip(t, pw)]
            pw = [nar(x[:, :st]) for x in both]
            t = [nar(x + y_[:, st:]) for x, y_ in zip(t, both)]
        cov *= 2
    lm_v = [mm_a(jnp.concatenate([x, y_], axis=0), z) for x, y_, z in zip(l_k, m_k, v_st)]
    mkv4 = [fold(x[st:]) for x in lm_v]
    t_av = [mm_a(x, jnp.concatenate([y_, nar(z[:st])], axis=1)) for x, y_, z in zip(t, a_st, lm_v)]
    wm4 = [fold(x[:, :st]) for x in t_av]
    uk4 = [fold(x[:, st:]) for x in t_av]

    us = [[None] * nb for _ in gs]
    ys = [[None] * nb for _ in gs]
    for ck in range(ncs):
        cgs = range(ck * gpb, (ck + 1) * gpb)
        for b in range(nb):
            rows = slice(b * C, (b + 1) * C)
            s_old = {gi: sbd_ref[b, gi - ck * gpb] for gi in cgs}
            uy = {gi: mm_s(jnp.concatenate([wm4[gi][rows], r4[gi][rows]], axis=0), s_old[gi], _NT) for gi in cgs}
            for gi in cgs:
                us[gi][b] = uy[gi][:C] + uk4[gi][rows]
                ys[gi][b] = uy[gi][C:]
            ds = {gi: mm_s(jnp.concatenate([us[gi][b], vg[gi][rows]], axis=0),
                           jnp.concatenate([b4[gi][rows], k4[gi][rows]], axis=0), _TN) for gi in cgs}
            for gi in cgs:
                s_new = (s_old[gi] + jnp.where(same_head, ds[gi], 0.0)) * gam[gi][(b + 1) * C - 1:(b + 1) * C, :]
                sbd_ref[b, gi - ck * gpb] = s_new
                if ck == ncs - 1:
                    @pl.when(c == nsteps - 1)
                    def _fin():
                        hs = slice((gi - ck * gpb) * HEADS_PER_GROUP, (gi - ck * gpb + 1) * HEADS_PER_GROUP)
                        sout_ref[b, hs] = from_block_diag(s_new).reshape(HEADS_PER_GROUP, HEAD_DIM, HEAD_DIM)

    u = [x[0] if nb == 1 else jnp.concatenate(x, axis=0) for x in us]
    ysum = [x[0] if nb == 1 else jnp.concatenate(x, axis=0) for x in ys]
    yb = [fold(mm_s(x, stack(y_))) for x, y_ in zip(m_b, u)]
    y = [x + y_ + z for x, y_, z in zip(ysum, mkv4, yb)]

    ym = [_dot_xr(x, block_ones, passes=ps["gn"]) * (1.0 / HEAD_DIM) for x in y]
    yc = [x - y_ for x, y_ in zip(y, ym)]
    yv = [_dot_xr(x * x, block_ones, passes=ps["gn"]) * (1.0 / HEAD_DIM) for x in yc]
    bonus = [_dot_xr(rg[gi] * k2[gi] * rk_ref[:, sls[gi]], block_ones, passes=ps["gn"]) * vg[gi] for gi in gs]
    for gi in gs:
        sl, rw = sls[gi], rws[gi]
        yn = yc[gi] * lax.rsqrt(yv[gi] + GN_EPS) * gng_ref[:, sl] + gnb_ref[:, sl]
        o_ref[rw, sl] = (yn + bonus[gi]) * gate_all[rw, sl]


def _rwkv(proj, nseq, nchunk, C, prev_shift, state0, prm, gpb=None, ncs=None):
    nb = ROWS // C
    if gpb is None:
        gpb = N_GROUPS if nb == 1 else N_GROUPS // 2
    if ncs is None:
        ncs = 2 if nb == 1 and nchunk % 2 == 0 else 1
    assert nb * C == ROWS and nseq % nb == 0 and (nb == 1 or nchunk == 1) and nchunk % ncs == 0
    rows = nseq * nchunk * C
    nsteps = nchunk // ncs
    srows = ncs * ROWS
    w = gpb * GROUP_W
    sec = D_MODEL // w
    prows = 8 if nb == 1 else C
    pblk = 8 if nb == 1 else srows
    prev = jnp.zeros((nseq, prows, SHIFT_W), F32).at[:, 0].set(prev_shift).reshape(nseq * prows, SHIFT_W)
    prev_rkv = prev[:, :RKV_W]
    prev_lora = jnp.pad(prev[:, RKV_W:], ((0, 0), (0, LORA_PAD - LORA_W)))

    def tok(off):
        return pl.BlockSpec((srows, w), lambda s, g, c: (s * nsteps + c, off + g))

    def prv(off):
        return pl.BlockSpec((pblk, w), lambda s, g, c: (s, off + g))

    def vec(off=0):
        return pl.BlockSpec((1, w), lambda s, g, c: (0, off + g))

    up = pl.BlockSpec((LORA_PAD, w), lambda s, g, c: (0, g))
    in_specs = [
        tok(0), tok(sec), tok(2 * sec),
        pl.BlockSpec((srows, LORA_PAD), lambda s, g, c: (s * nsteps + c, COL_LORA // LORA_PAD)),
        prv(0), prv(sec), prv(2 * sec),
        pl.BlockSpec((pblk, LORA_PAD), lambda s, g, c: (s, 0)),
        pl.BlockSpec((nb, gpb * HEADS_PER_GROUP, HEAD_DIM, HEAD_DIM), lambda s, g, c: (s, g, 0, 0)),
        vec(0), vec(sec), vec(2 * sec),
        pl.BlockSpec((1, LORA_PAD), lambda s, g, c: (0, 0)),
        vec(), vec(), vec(), vec(), vec(), vec(), vec(),
        up, up, up,
    ]
    out_specs = [pl.BlockSpec((srows, w), lambda s, g, c: (s * nsteps + c, g)),
                 pl.BlockSpec((nb, gpb * HEADS_PER_GROUP, HEAD_DIM, HEAD_DIM), lambda s, g, c: (s, g, 0, 0))]
    out_shape = [jax.ShapeDtypeStruct((rows, D_MODEL), F32),
                 jax.ShapeDtypeStruct((nseq, D_MODEL // HEAD_DIM, HEAD_DIM, HEAD_DIM), F32)]
    operands = [proj, proj, proj, proj, prev_rkv, prev_rkv, prev_rkv, prev_lora, state0,
                prm["mu_rkv"], prm["mu_rkv"], prm["mu_rkv"], prm["mu_lora"],
                prm["w0"], prm["a0"], prm["k_k"], prm["k_a"], prm["r_k"], prm["gn_g"], prm["gn_b"],
                prm["w_up"], prm["a_up"], prm["g_up"]]
    return pl.pallas_call(
        functools.partial(_rwkv_body, C, nb, ncs, nsteps, gpb, RWKV_PASSES),
        grid=(nseq // nb, N_GROUPS // gpb, nsteps),
        in_specs=in_specs,
        out_specs=out_specs,
        out_shape=out_shape,
        scratch_shapes=[pltpu.VMEM((nb, gpb, GROUP_W, GROUP_W), F32),
                        pltpu.VMEM((srows + 8, 3 * w), F32),
                        pltpu.VMEM((srows + 8, LORA_PAD), F32)],
        compiler_params=_params(("arbitrary", "arbitrary", "arbitrary")),
        name="rwkv",
    )(*operands)


def _attn_rows(q, kcat, vcat, sink_ref, first_key, store, npr):
    m = q.shape[0]
    kt_all = kcat.T
    qi = _iota((npr * m, 2 * WINDOW), 0) & (m - 1)
    ki = _iota((npr * m, 2 * WINDOW), 1)
    allowed = (ki >= qi + 1) & (ki <= qi + WINDOW) & (ki >= first_key)
    lane = _iota((2 * WINDOW, 128), 1)
    ones_k = jnp.ones((2 * WINDOW, 128), BF16)
    zk = jnp.zeros((HEAD_DIM, 2 * WINDOW), F32)
    nkv = KV_W // HEAD_DIM
    k2s, v2s = [], []
    for kv in range(nkv):
        kt = kt_all[kv * HEAD_DIM:(kv + 1) * HEAD_DIM, :]
        k2s.append(jnp.concatenate([jnp.concatenate([kt, zk], axis=1),
                                    jnp.concatenate([zk, kt], axis=1)], axis=0).astype(BF16))
        slab = vcat[:, (kv // 2) * 128:(kv // 2 + 1) * 128]
        rolled = pltpu.roll(slab, HEAD_DIM, axis=1)
        lo_src, hi_src = (slab, rolled) if kv % 2 == 0 else (rolled, slab)
        v2s.append(jnp.concatenate([jnp.where(lane < HEAD_DIM, lo_src, 0.0),
                                    jnp.where(lane >= HEAD_DIM, hi_src, 0.0)], axis=0).astype(BF16))
    items = [(kv, p0) for kv in range(nkv) for p0 in range(0, 4, npr)]
    cols = [[kv * 512 + (p0 + i) * 128 for i in range(npr)] for kv, p0 in items]
    scores = [_bdot((jnp.concatenate([q[:, col:col + 128] for col in cs], axis=0) * ATT_SCALE).astype(BF16),
                    k2s[kv]) for (kv, _), cs in zip(items, cols)]
    probs = []
    for (kv, p0), s in zip(items, scores):
        ps = []
        for hb in range(2):
            sink = jnp.concatenate(
                [jnp.broadcast_to(sink_ref[:, (kv * 8 + (p0 + i) * 2 + hb) * 128:
                                           (kv * 8 + (p0 + i) * 2 + hb) * 128 + 1], (m, 1))
                 for i in range(npr)], axis=0)
            sh = jnp.where(allowed, s[:, hb * 256:(hb + 1) * 256], -jnp.inf)
            mx = jnp.maximum(jnp.max(sh, axis=1, keepdims=True), sink)
            e = jnp.exp(sh - mx)
            inv = 1.0 / (_dot_xr(e, ones_k, passes=2) + jnp.exp(sink - mx))
            ps += [e[:, :128] * inv, e[:, 128:] * inv]
        probs.append(jnp.concatenate(ps, axis=1).astype(BF16))
    outs = [_bdot(p, v2s[kv]) for (kv, _), p in zip(items, probs)]
    for cs, o in zip(cols, outs):
        for i, col in enumerate(cs):
            store(col, o[i * m:(i + 1) * m])


def _attn_prompt_body(q_ref, kvc_ref, kvp_ref, sink_ref, o_ref):
    n = pl.program_id(0)
    kvc = kvc_ref[...]
    kvp = kvp_ref[...]
    kcat = jnp.concatenate([kvp[:, :KV_W], kvc[:, :KV_W]], axis=0)
    vcat = jnp.concatenate([kvp[:, KV_W:], kvc[:, KV_W:]], axis=0)

    def store(col, val):
        o_ref[:, col:col + 128] = val

    _attn_rows(q_ref[...], kcat, vcat, sink_ref, jnp.where(n > 0, 0, WINDOW), store, npr=1)


def _attn_prompt(proj, sinks_e):
    t = proj.shape[0]
    nb = t // WINDOW
    qb, kvb = COL_Q // D_MODEL, COL_KV // (2 * KV_W)
    return pl.pallas_call(
        _attn_prompt_body,
        grid=(nb,),
        in_specs=[pl.BlockSpec((WINDOW, D_MODEL), lambda n: (n, qb)),
                  pl.BlockSpec((WINDOW, 2 * KV_W), lambda n: (n, kvb)),
                  pl.BlockSpec((WINDOW, 2 * KV_W), lambda n: (jnp.maximum(n - 1, 0), kvb)),
                  pl.BlockSpec((1, 32 * 128), lambda n: (0, 0))],
        out_specs=pl.BlockSpec((WINDOW, D_MODEL), lambda n: (n, 0)),
        out_shape=jax.ShapeDtypeStruct((t, D_MODEL), F32),
        compiler_params=_params(("parallel",)),
        name="attn_prompt",
    )(proj, proj, proj, sinks_e)


def _attn_sample_body(L, bt, q_ref, kvn_ref, ck_ref, cv_ref, sink_ref, o_ref):
    pad = jnp.zeros((WINDOW - L, KV_W), F32)

    def one(b, carry):
        rows = pl.ds(pl.multiple_of(b * L, L), L)
        kvn = kvn_ref[rows, :]
        kcat = jnp.concatenate([ck_ref[b], kvn[:, :KV_W], pad], axis=0)
        vcat = jnp.concatenate([cv_ref[b], kvn[:, KV_W:], pad], axis=0)

        def store(col, val):
            o_ref[rows, col:col + 128] = val

        _attn_rows(q_ref[rows, :], kcat, vcat, sink_ref, 0, store, npr=4)
        return carry

    lax.fori_loop(0, bt, one, 0, unroll=2)


def _attn_sample(proj, ck, cv, sinks_e, L, bt=16):
    rows = proj.shape[0]
    nb = rows // L
    bt = min(bt, nb)
    qb, kvb = COL_Q // D_MODEL, COL_KV // (2 * KV_W)
    return pl.pallas_call(
        functools.partial(_attn_sample_body, L, bt),
        grid=(nb // bt,),
        in_specs=[pl.BlockSpec((bt * L, D_MODEL), lambda i: (i, qb)),
                  pl.BlockSpec((bt * L, 2 * KV_W), lambda i: (i, kvb)),
                  pl.BlockSpec((bt, WINDOW, KV_W), lambda i: (i, 0, 0)),
                  pl.BlockSpec((bt, WINDOW, KV_W), lambda i: (i, 0, 0)),
                  pl.BlockSpec((1, 32 * 128), lambda i: (0, 0))],
        out_specs=pl.BlockSpec((bt * L, D_MODEL), lambda i: (i, 0)),
        out_shape=jax.ShapeDtypeStruct((rows, D_MODEL), F32),
        compiler_params=_params(("parallel",)),
        name="attn_sample",
    )(proj, proj, ck, cv, sinks_e)


def _layer_norm(x, g, b):
    mu = jnp.mean(x, axis=-1, keepdims=True)
    xc = x - mu
    var = jnp.mean(xc * xc, axis=-1, keepdims=True)
    return xc * lax.rsqrt(var + LN_EPS) * g + b


def _merge_body(ga_ref, gb_ref, oa_ref, ob_ref, x_ref, wout_ref, g_ref, b_ref, h_ref, ht_ref):
    mixed = _sigmoid(ga_ref[...]) * oa_ref[...] + _sigmoid(gb_ref[...]) * ob_ref[...]
    y = _bdot(mixed.astype(BF16), wout_ref[...])
    h = _layer_norm(DN_ALPHA * x_ref[...] + y, g_ref[...], b_ref[...])
    h_ref[...] = h
    ht_ref[...] = h.T.astype(BF16)


def _merge(proj, o_a, o_b, x2d, wout_bf16, ln_g, ln_b, tm=256):
    m = x2d.shape[0]
    tm = min(tm, m)
    gab = COL_GATE // D_MODEL
    row = lambda i: (i, 0)
    return pl.pallas_call(
        _merge_body,
        grid=(m // tm,),
        in_specs=[pl.BlockSpec((tm, D_MODEL), lambda i: (i, gab)),
                  pl.BlockSpec((tm, D_MODEL), lambda i: (i, gab + 1)),
                  pl.BlockSpec((tm, D_MODEL), row),
                  pl.BlockSpec((tm, D_MODEL), row),
                  pl.BlockSpec((tm, D_MODEL), row),
                  pl.BlockSpec((D_MODEL, D_MODEL), lambda i: (0, 0)),
                  pl.BlockSpec((1, D_MODEL), lambda i: (0, 0)),
                  pl.BlockSpec((1, D_MODEL), lambda i: (0, 0))],
        out_specs=[pl.BlockSpec((tm, D_MODEL), row),
                   pl.BlockSpec((D_MODEL, tm), lambda i: (0, i))],
        out_shape=[jax.ShapeDtypeStruct((m, D_MODEL), F32),
                   jax.ShapeDtypeStruct((D_MODEL, m), BF16)],
        compiler_params=_params(("parallel",)),
        name="merge",
    )(proj, proj, o_a, o_b, x2d, wout_bf16, ln_g, ln_b)


def _top_rows(x, count):
    rows = _iota(x.shape, 0)
    out = []
    for _ in range(count):
        mx = jnp.max(x, axis=0, keepdims=True)
        out.append(mx)
        first = jnp.min(jnp.where(x == mx, rows, x.shape[0]), axis=0, keepdims=True)
        x = jnp.where(rows == first, -jnp.inf, x)
    return out


def _bitonic_network(n):
    comps = []
    k = 2
    while k <= n:
        j = k // 2
        while j >= 1:
            comps += [(i, i ^ j, (i & k) == 0) for i in range(n) if (i ^ j) > i]
            j //= 2
        k *= 2
    return comps


def _top16_of_128(xs):
    nt = N_KEYS // 8
    tiles = [[x[8 * j:8 * (j + 1)] for j in range(nt)] for x in xs]
    for i, l, desc in _bitonic_network(nt):
        for t in tiles:
            hi, lo = jnp.maximum(t[i], t[l]), jnp.minimum(t[i], t[l])
            t[i], t[l] = (hi, lo) if desc else (lo, hi)
    sub = _iota(tiles[0][0].shape, 0)
    outs = [[] for _ in xs]
    for r in range(PEER_TOPK):
        for t, o in zip(tiles, outs):
            mx = jnp.max(t[0], axis=0, keepdims=True)
            o.append(mx)
            first = jnp.min(jnp.where(t[0] == mx, sub, 8), axis=0, keepdims=True)
            pop = sub == first
            for j in range(PEER_TOPK - r - 1):
                t[j] = jnp.where(pop, t[j + 1], t[j])
    return outs


def _route_body(tm, h_ref, wq_ref, keys_ref, s0_ref, s1_ref, aux_ref, q_s):
    q = _bdot(h_ref[...].astype(BF16), wq_ref[...])
    for hc in range(2 * PEER_HEADS):
        q_s[hc] = q[:, hc * N_KEYS:(hc + 1) * N_KEYS]
    pairs = [(i, j) for i in range(PEER_TOPK) for j in range(PEER_TOPK) if (i + 1) * (j + 1) <= PEER_TOPK]
    npad = -len(pairs) % 8
    neg = jnp.full((1, tm), -jnp.inf, F32)
    zero = jnp.zeros((1, tm), F32)

    def head(hh, carry):
        s0 = _bdot(keys_ref[hh, 0], q_s[2 * hh].astype(BF16), _NT)
        s1 = _bdot(keys_ref[hh, 1], q_s[2 * hh + 1].astype(BF16), _NT)
        s0_ref[hh] = s0
        for lb in range(tm // 128):
            s1_ref[hh, lb] = s1[:, lb * 128:(lb + 1) * 128]
        top0, top1 = _top16_of_128([s0, s1])
        cand = jnp.concatenate([top0[i] + top1[j] for i, j in pairs] + [neg] * npad, axis=0)
        best = _top_rows(cand, PEER_TOPK)
        mx = best[0]
        z = zero
        for bsum in best:
            z = z + jnp.exp(bsum - mx)
        aux_ref[hh] = jnp.concatenate([best[-1], top0[0], top1[0], 1.0 / z, zero, zero, zero, zero], axis=0)
        return carry

    lax.fori_loop(0, PEER_HEADS, head, 0, unroll=2)


def _route(h, wq_bf16, keys_bf16, tm=512):
    m = h.shape[0]
    tm = min(tm, m)
    tok3 = lambda i: (0, 0, i)
    return pl.pallas_call(
        functools.partial(_route_body, tm),
        grid=(m // tm,),
        in_specs=[pl.BlockSpec((tm, D_MODEL), lambda i: (i, 0)),
                  pl.BlockSpec((D_MODEL, D_MODEL), lambda i: (0, 0)),
                  pl.BlockSpec((PEER_HEADS, 2, N_KEYS, N_KEYS), lambda i: (0, 0, 0, 0))],
        out_specs=[pl.BlockSpec((PEER_HEADS, N_KEYS, tm), tok3),
                   pl.BlockSpec((PEER_HEADS, tm // 128, N_KEYS, 128), lambda i: (0, i, 0, 0)),
                   pl.BlockSpec((PEER_HEADS, 8, tm), tok3)],
        out_shape=[jax.ShapeDtypeStruct((PEER_HEADS, N_KEYS, m), F32),
                   jax.ShapeDtypeStruct((PEER_HEADS, m // 128, N_KEYS, 128), F32),
                   jax.ShapeDtypeStruct((PEER_HEADS, 8, m), F32)],
        scratch_shapes=[pltpu.VMEM((2 * PEER_HEADS, tm, N_KEYS), F32)],
        compiler_params=_params(("parallel",)),
        name="peer_route",
    )(h, wq_bf16, keys_bf16)


def _peer_body(te, ne, ht_ref, u_ref, vt_ref, s0_ref, s1_ref, aux_ref, h_ref, g_ref, b_ref, y_ref,
               acc_ref, e1_ref, cf_ref, st0_ref, st1_ref, w0_ref, w1_ref):
    j = pl.program_id(1)
    tm = acc_ref.shape[1]

    @pl.when(j == 0)
    def _():
        acc_ref[...] = jnp.zeros_like(acc_ref)
        for ref in (st0_ref, st1_ref, w0_ref, w1_ref):
            ref[...] = jnp.zeros_like(ref)
        for hh in range(PEER_HEADS):
            aux = aux_ref[hh]
            for lb in range(tm // 128):
                e1_ref[hh, lb] = jnp.exp(s1_ref[hh, lb] - aux[2:3, lb * 128:(lb + 1) * 128])
            cf_ref[hh] = jnp.exp(s0_ref[hh] - aux[1:2, :]) * aux[3:4, :]

    def stages(st_new, st_old, w_new, w_old):
        jb = j - 1
        live = (jb >= 0) & (jb < ne)
        jb_c = jnp.clip(jb, 0, ne - 1)
        taus = [jnp.where(live, aux_ref[hh, 0:1, :], jnp.inf) for hh in range(PEER_HEADS)]
        ncc = te // N_KEYS
        s0rows = [[s0_ref[hh, pl.ds(jb_c * ncc + cc, 1), :] for hh in range(PEER_HEADS)] for cc in range(ncc)]
        cfrows = [[cf_ref[hh, pl.ds(jb_c * ncc + cc, 1), :] for hh in range(PEER_HEADS)] for cc in range(ncc)]
        tw = min(256, tm)
        mxu_pieces, gate_tiles = [], []
        for t0 in range(0, tm, tw):
            ts = slice(t0, t0 + tw)
            lbs = range(t0 // 128, (t0 + tw) // 128)

            def mix(mr, ts=ts, lbs=lbs):
                acc_ref[mr, ts] += _bdot(vt_ref[mr, :], jnp.concatenate([w_old[lb] for lb in lbs], axis=1))

            def score(er, t0=t0, ts=ts, lbs=lbs):
                st = _bdot(u_ref[er, :], ht_ref[:, ts])
                for lb in lbs:
                    st_new[lb, er, :] = st[:, lb * 128 - t0:(lb + 1) * 128 - t0]

            for q in range(4):
                mxu_pieces.append(functools.partial(mix, slice(q * (D_MODEL // 4), (q + 1) * (D_MODEL // 4))))
            for q in range(2):
                mxu_pieces.append(functools.partial(score, slice(q * (te // 2), (q + 1) * (te // 2))))

            def gate(lb, cc):
                ls = slice(lb * 128, (lb + 1) * 128)
                rows = slice(cc * N_KEYS, (cc + 1) * N_KEYS)
                gsum = None
                for hh in range(PEER_HEADS):
                    sel = (s1_ref[hh, lb] + s0rows[cc][hh][:, ls]) >= taus[hh][:, ls]
                    term = jnp.where(sel, e1_ref[hh, lb] * cfrows[cc][hh][:, ls], 0.0)
                    gsum = term if gsum is None else gsum + term
                so = st_old[lb, rows, :]
                act = 0.5 * so * (1.0 + lax.erf(so * np.float32(np.sqrt(0.5))))
                w_new[lb, rows, :] = (gsum * act).astype(BF16)

            for lb in lbs:
                for cc in range(ncc):
                    gate_tiles.append(functools.partial(gate, lb, cc))
        for i in range(max(len(mxu_pieces), len(gate_tiles))):
            if i < len(gate_tiles):
                gate_tiles[i]()
            if i < len(mxu_pieces):
                mxu_pieces[i]()

    @pl.when(j % 2 == 0)
    def _():
        stages(st0_ref, st1_ref, w1_ref, w0_ref)

    @pl.when(j % 2 == 1)
    def _():
        stages(st1_ref, st0_ref, w0_ref, w1_ref)

    @pl.when(j == ne + 1)
    def _():
        out = acc_ref[...].T
        y_ref[...] = _layer_norm(DN_ALPHA * h_ref[...] + out, g_ref[...], b_ref[...])


def _peer(ht, u_bf16, vt_bf16, s0t, s1t, aux, h, ln_g, ln_b, tm=512, te=512):
    m = h.shape[0]
    tm = min(tm, m)
    ne = N_EXPERTS // te
    tok3 = lambda i, j: (0, 0, i)
    return pl.pallas_call(
        functools.partial(_peer_body, te, ne),
        grid=(m // tm, ne + 2),
        in_specs=[pl.BlockSpec((D_MODEL, tm), lambda i, j: (0, i)),
                  pl.BlockSpec((te, D_MODEL), lambda i, j: (jnp.minimum(j, ne - 1), 0)),
                  pl.BlockSpec((D_MODEL, te), lambda i, j: (0, jnp.clip(j - 2, 0, ne - 1))),
                  pl.BlockSpec((PEER_HEADS, N_KEYS, tm), tok3),
                  pl.BlockSpec((PEER_HEADS, tm // 128, N_KEYS, 128), lambda i, j: (0, i, 0, 0)),
                  pl.BlockSpec((PEER_HEADS, 8, tm), tok3),
                  pl.BlockSpec((tm, D_MODEL), lambda i, j: (i, 0)),
                  pl.BlockSpec((1, D_MODEL), lambda i, j: (0, 0)),
                  pl.BlockSpec((1, D_MODEL), lambda i, j: (0, 0))],
        out_specs=pl.BlockSpec((tm, D_MODEL), lambda i, j: (i, 0)),
        out_shape=jax.ShapeDtypeStruct((m, D_MODEL), F32),
        scratch_shapes=[pltpu.VMEM((D_MODEL, tm), F32),
                        pltpu.VMEM((PEER_HEADS, tm // 128, N_KEYS, 128), F32),
                        pltpu.VMEM((PEER_HEADS, N_KEYS, tm), F32),
                        pltpu.VMEM((tm // 128, te, 128), F32),
                        pltpu.VMEM((tm // 128, te, 128), F32),
                        pltpu.VMEM((tm // 128, te, 128), BF16),
                        pltpu.VMEM((tm // 128, te, 128), BF16)],
        compiler_params=_params(("parallel", "arbitrary")),
        name="peer_dense",
    )(ht, u_bf16, vt_bf16, s0t, s1t, aux, h, ln_g, ln_b)


def _pad_rows(w, lo, total):
    return jnp.zeros((total, w.shape[1]), w.dtype).at[lo:lo + w.shape[0]].set(w)


def _layer(x2d, nseq, seq_len, chunk, prev_shift, state0, cache, wts):
    proj_a = _inproj(x2d, wts["w_in_t"], PROJ_A_W)
    proj_b = _inproj(x2d, wts["w_in_b"], PROJ_B_W)
    o_a, s_last = _rwkv(proj_a, nseq, seq_len // chunk, chunk, prev_shift, state0, wts)
    if cache is None:
        o_b = _attn_prompt(proj_b, wts["sinks"])
    else:
        o_b = _attn_sample(proj_b, cache[0], cache[1], wts["sinks"], seq_len)
    h, ht = _merge(proj_b, o_a, o_b, x2d, wts["w_out"], wts["ln1_g"], wts["ln1_b"])
    s0t, s1t, aux = _route(h, wts["peer_wq"], wts["peer_keys"])
    y = _peer(ht, wts["peer_u"], wts["peer_vt"], s0t, s1t, aux, h, wts["ln2_g"], wts["ln2_b"])
    return y, proj_a, proj_b, s_last


def kernel(x_prompt, x_sample, state_shift, state_wkv, cache_k, cache_v, w_in, rw_mu, rw_w0, rw_w_up, rw_a0,
           rw_a_up, rw_g_up, rw_k_k, rw_k_a, rw_r_k, rw_gn_g, rw_gn_b, att_sinks, w_out, ln1_g, ln1_b,
           peer_wq, peer_keys, peer_u, peer_v, ln2_g, ln2_b):
    depth = w_in.shape[0]
    assert depth == 1
    l = 0
    bp, tp, _ = x_prompt.shape
    bs, ts, _ = x_sample.shape
    assert bp == 1
    win = cache_k.shape[2]
    assert win == WINDOW and tp % WINDOW == 0 and ts <= 8

    w_t = jnp.transpose(w_in[l]).astype(BF16)
    q0 = SHIFT_W
    k0 = q0 + D_MODEL
    g0 = k0 + 2 * KV_W
    w_b = jnp.concatenate([w_t[q0:q0 + D_MODEL], w_t[g0:g0 + 2 * D_MODEL], w_t[k0:k0 + 2 * KV_W]], axis=0)
    mu = rw_mu[l]
    row = lambda v: v.reshape(1, -1)
    wts = dict(
        w_in_t=w_t, w_in_b=w_b,
        mu_rkv=row(mu[:RKV_W]),
        mu_lora=row(jnp.pad(mu[RKV_W:], (0, LORA_PAD - LORA_W))),
        w0=row(rw_w0[l]), a0=row(rw_a0[l]), k_k=row(rw_k_k[l]), k_a=row(rw_k_a[l]), r_k=row(rw_r_k[l]),
        gn_g=row(rw_gn_g[l]), gn_b=row(rw_gn_b[l]),
        w_up=_pad_rows(rw_w_up[l], 0, LORA_PAD).astype(BF16),
        a_up=_pad_rows(rw_a_up[l], 96, LORA_PAD).astype(BF16),
        g_up=_pad_rows(rw_g_up[l], 192, LORA_PAD).astype(BF16),
        sinks=jnp.repeat(att_sinks[l], 128).reshape(1, 32 * 128),
        w_out=w_out[l].astype(BF16),
        ln1_g=row(ln1_g[l]), ln1_b=row(ln1_b[l]), ln2_g=row(ln2_g[l]), ln2_b=row(ln2_b[l]),
        peer_wq=peer_wq[l].astype(BF16),
        peer_keys=peer_keys[l].astype(BF16),
        peer_u=peer_u[l].astype(BF16),
        peer_vt=peer_v[l].T.astype(BF16),
    )

    chunk_p = 64
    y_p, pa_p, pb_p, s_p = _layer(x_prompt[0], 1, tp, chunk_p,
                                  jnp.zeros((1, SHIFT_W), F32),
                                  jnp.zeros((1, D_MODEL // HEAD_DIM, HEAD_DIM, HEAD_DIM), F32), None, wts)
    keep = min(WINDOW, tp)
    kv_p = pb_p[tp - keep:, COL_KV:]
    y_prompt = y_p[None]
    new_shift_prompt = pa_p[tp - 1, :SHIFT_W][None, None]
    new_wkv_prompt = s_p[None]
    new_k_prompt = kv_p[:, :KV_W].reshape(1, 1, keep, KV_W // HEAD_DIM, HEAD_DIM)
    new_v_prompt = kv_p[:, KV_W:].reshape(1, 1, keep, KV_W // HEAD_DIM, HEAD_DIM)

    ck = cache_k[l].reshape(bs, win, KV_W)
    cv = cache_v[l].reshape(bs, win, KV_W)
    y_s, pa_s, pb_s, s_s = _layer(x_sample.reshape(bs * ts, D_MODEL), bs, ts, ts,
                                  state_shift[l], state_wkv[l], (ck, cv), wts)
    y_sample = y_s.reshape(bs, ts, D_MODEL)
    new_shift_sample = pa_s.reshape(bs, ts, PROJ_A_W)[:, ts - 1, :SHIFT_W][None]
    new_wkv_sample = s_s[None]
    kv_s = pb_s.reshape(bs, ts, PROJ_B_W)[:, :, COL_KV:]
    new_k_sample = jnp.concatenate([ck, kv_s[:, :, :KV_W]], axis=1)[:, ts:].reshape(
        1, bs, win, KV_W // HEAD_DIM, HEAD_DIM)
    new_v_sample = jnp.concatenate([cv, kv_s[:, :, KV_W:]], axis=1)[:, ts:].reshape(
        1, bs, win, KV_W // HEAD_DIM, HEAD_DIM)

    return (y_prompt, y_sample, new_shift_prompt, new_wkv_prompt, new_k_prompt, new_v_prompt,
            new_shift_sample, new_wkv_sample, new_k_sample, new_v_sample)
```

```python
import functools

import numpy as np
import jax
import jax.numpy as jnp
from jax import lax
from jax.experimental import pallas as pl
from jax.experimental.pallas import tpu as pltpu

F32, BF16 = jnp.float32, jnp.bfloat16

D_MODEL = 2048
HEAD_DIM = 64
HEADS_PER_GROUP = 2
GROUP_W = HEADS_PER_GROUP * HEAD_DIM
N_GROUPS = D_MODEL // GROUP_W
RKV_W = 3 * D_MODEL
LORA_W = 96 + 96 + 256
LORA_PAD = 512
SHIFT_W = RKV_W + LORA_W
KV_W = 256
GN_EPS = 64e-5
LN_EPS = 1e-5
WINDOW = 128
ATT_SCALE = HEAD_DIM ** -0.5
N_KEYS = 128
N_EXPERTS = N_KEYS * N_KEYS
PEER_HEADS = 8
PEER_TOPK = 16
DN_ALPHA = 2.0 ** 0.25

COL_LORA = RKV_W
PROJ_A_W = RKV_W + LORA_PAD
COL_Q = 0
COL_GATE = D_MODEL
COL_KV = 3 * D_MODEL
PROJ_B_W = COL_KV + 2 * KV_W

VMEM_LIMIT = 56 * 1024 * 1024

INPROJ_TM, INPROJ_TN = 1024, 1664
MERGE_TM = 256
ROUTE_TM = 512
PEER_TM, PEER_TE = 512, 512
ATTN_SAMPLE_SEQS = 16

_NN = (((1,), (0,)), ((), ()))
_NT = (((1,), (1,)), ((), ()))
_TN = (((0,), (0,)), ((), ()))


def _bdot(a, b, dn=_NN):
    return lax.dot_general(a, b, dn, preferred_element_type=F32)


def _dot(a, b, dn=_NN):
    return _bdot(a.astype(BF16), b.astype(BF16), dn)


def _split(x, n):
    parts, r = [], x
    for i in range(n):
        p = r.astype(BF16)
        parts.append(p)
        if i + 1 < n:
            r = r - p.astype(F32)
    return parts


def _dot3(a, b, dn=_NN):
    a1, a2 = _split(a, 2)
    b1, b2 = _split(b, 2)
    return (_bdot(a1, b2, dn) + _bdot(a2, b1, dn)) + _bdot(a1, b1, dn)


def _dot_xl(a_bf16, b, dn=_NN, passes=3):
    out = None
    for part in reversed(_split(b, passes)):
        t = _bdot(a_bf16, part, dn)
        out = t if out is None else out + t
    return out


def _dot_xr(a, b_bf16, dn=_NN, passes=3):
    out = None
    for part in reversed(_split(a, passes)):
        t = _bdot(part, b_bf16, dn)
        out = t if out is None else out + t
    return out


def _mm(passes):
    return _dot if passes == 1 else _dot3


def _sigmoid(x):
    return 1.0 / (1.0 + jnp.exp(-x))


def _iota(shape, dim):
    return lax.broadcasted_iota(jnp.int32, shape, dim)


def _head_of(idx):
    return jnp.right_shift(idx, 6)


def _ones_where(mask):
    return jnp.where(mask, 1.0, 0.0).astype(BF16)


def _params(sem):
    return pltpu.CompilerParams(dimension_semantics=sem, vmem_limit_bytes=VMEM_LIMIT)


def _inproj_body(x_ref, w_ref, o_ref, xb_ref):
    @pl.when(pl.program_id(1) == 0)
    def _():
        xb_ref[...] = x_ref[...].astype(BF16)

    o_ref[...] = _bdot(xb_ref[...], w_ref[...].astype(BF16), _NT)


def _inproj(x2d, w_t, n, tn=INPROJ_TN):
    assert n % tn == 0 and tn % 128 == 0
    m = x2d.shape[0]
    tm = min(m, INPROJ_TM)
    return pl.pallas_call(
        _inproj_body,
        grid=(m // tm, n // tn),
        in_specs=[pl.BlockSpec((tm, D_MODEL), lambda i, j: (i, 0)),
                  pl.BlockSpec((tn, D_MODEL), lambda i, j: (j, 0))],
        out_specs=pl.BlockSpec((tm, tn), lambda i, j: (i, j)),
        out_shape=jax.ShapeDtypeStruct((m, n), F32),
        scratch_shapes=[pltpu.VMEM((tm, D_MODEL), BF16)],
        compiler_params=_params(("parallel", "arbitrary")),
        name="inproj",
    )(x2d, w_t)


RWKV_PASSES = dict(seg=1, cum=3, lblk=1, inv=1, apply=1, state=1, gn=2)
ROWS = 64


def _rwkv_body(C, nb, ncs, nsteps, gpb, ps,
               r_ref, k_ref, v_ref, lo_ref, pr_ref, pk_ref, pv_ref, plo_ref, s0_ref,
               mur_ref, muk_ref, muv_ref, mulo_ref, w0_ref, a0_ref, kk_ref, ka_ref, rk_ref, gng_ref, gnb_ref,
               wup_ref, aup_ref, gup_ref,
               o_ref, sout_ref,
               sbd_ref, sh_ref, shl_ref):
    assert HEADS_PER_GROUP == 2 and HEADS_PER_GROUP * ROWS == GROUP_W and (ncs == 1 or nb == 1)
    c = pl.program_id(2)
    w = gpb * GROUP_W
    srows = ncs * ROWS
    log2c = C.bit_length() - 1
    st = HEADS_PER_GROUP * ROWS
    sq = (st, st)
    ri, ci = _iota(sq, 0), _iota(sq, 1)
    same_head = _head_of(ri) == _head_of(ci)
    tr, tc = ri & (ROWS - 1), ci & (ROWS - 1)
    blk = same_head & (jnp.right_shift(tr, log2c) == jnp.right_shift(tc, log2c))
    strict = blk & (tr > tc)
    incl = blk & (tr >= tc)
    eye = jnp.where(ri == ci, 1.0, 0.0)
    block_ones = _ones_where(same_head)
    r64, c64 = _iota((ROWS, ROWS), 0), _iota((ROWS, ROWS), 1)
    tri_seq = _ones_where((jnp.right_shift(r64, log2c) == jnp.right_shift(c64, log2c)) & (r64 >= c64))
    lane_head = _head_of(_iota((ROWS, GROUP_W), 1))
    first_head_rows = ri < HEAD_DIM

    def to_block_diag(s2):
        wide = jnp.concatenate([s2, jnp.zeros_like(s2)], axis=1)
        return jnp.where(first_head_rows, wide, pltpu.roll(wide, HEAD_DIM, axis=1))

    def from_block_diag(s):
        return jnp.where(first_head_rows, s, pltpu.roll(s, HEAD_DIM, axis=1))[:, :HEAD_DIM]
    mm_l, mm_i, mm_a, mm_s = _mm(ps["lblk"]), _mm(ps["inv"]), _mm(ps["apply"]), _mm(ps["state"])

    @pl.when(c == 0)
    def _init():
        if nb == 1:
            sh_ref[7:8, 0:w] = pr_ref[0:1, :]
            sh_ref[7:8, w:2 * w] = pk_ref[0:1, :]
            sh_ref[7:8, 2 * w:3 * w] = pv_ref[0:1, :]
            shl_ref[7:8, :] = plo_ref[0:1, :]
        for b in range(nb):
            for gi in range(gpb):
                s0 = s0_ref[b, gi * HEADS_PER_GROUP:(gi + 1) * HEADS_PER_GROUP].reshape(GROUP_W, HEAD_DIM)
                sbd_ref[b, gi] = to_block_diag(s0)

    def lerp(x, ref, lo, hi, prev_ref, mu):
        ref[pl.ds(8, srows), lo:hi] = x
        xs = ref[pl.ds(7, srows), lo:hi]
        ref[7:8, lo:hi] = x[srows - 1:srows, :]
        if nb > 1:
            first = (_iota(x.shape, 0) & (C - 1)) == 0
            xs = jnp.where(first, prev_ref[...], xs)
        return x + (xs - x) * mu

    r = lerp(r_ref[...], sh_ref, 0, w, pr_ref, mur_ref[...])
    k = lerp(k_ref[...], sh_ref, w, 2 * w, pk_ref, muk_ref[...])
    v = lerp(v_ref[...], sh_ref, 2 * w, 3 * w, pv_ref, muv_ref[...])
    lo = lerp(lo_ref[...], shl_ref, 0, LORA_PAD, plo_ref, mulo_ref[...])

    wl = w0_ref[...] + _dot(jnp.tanh(lo), wup_ref[...])
    w_log = -(jnp.maximum(-wl, 0.0) + jnp.log1p(jnp.exp(-jnp.abs(wl)))) - 0.5
    logw_all = -jnp.exp(w_log)
    a_all = _sigmoid(a0_ref[...] + _dot(lo, aup_ref[...]))
    gate_all = _dot(_sigmoid(lo), gup_ref[...])
    kk_all = k * kk_ref[...]
    k2_all = k * (1.0 + (a_all - 1.0) * ka_ref[...])

    def stack(x):
        return jnp.concatenate([jnp.where(lane_head == j, x, 0.0) for j in range(HEADS_PER_GROUP)], axis=0)

    def fold(x):
        return x[0:ROWS] + x[ROWS:2 * ROWS]

    gs = range(ncs * gpb)
    sls = [slice(gi * GROUP_W, (gi + 1) * GROUP_W) for _ in range(ncs) for gi in range(gpb)]
    rws = [slice(ck * ROWS, (ck + 1) * ROWS) for ck in range(ncs) for _ in range(gpb)]
    rg = [r[rw, sl] for rw, sl in zip(rws, sls)]
    vg = [v[rw, sl] for rw, sl in zip(rws, sls)]
    k2 = [k2_all[rw, sl] for rw, sl in zip(rws, sls)]
    kk = [kk_all[rw, sl] for rw, sl in zip(rws, sls)]
    ss = [_dot_xr(x * x, block_ones, passes=ps["seg"]) for x in kk]
    kkn = [x / jnp.maximum(jnp.sqrt(s2), 1e-12) for x, s2 in zip(kk, ss)]
    logw = [logw_all[rw, sl] for rw, sl in zip(rws, sls)]
    cum = [_dot_xl(tri_seq, x, passes=ps["cum"]) for x in logw]
    gam = [jnp.exp(x) for x in cum]
    ginv = [jnp.exp(-x) for x in cum]
    a4 = [-(jnp.exp(cm - lw) * kn) for cm, lw, kn in zip(cum, logw, kkn)]
    b4 = [kn * a_all[rw, sl] * gi_ for kn, rw, sl, gi_ in zip(kkn, rws, sls, ginv)]
    k4 = [x * gi_ for x, gi_ in zip(k2, ginv)]
    r4 = [gm * x for gm, x in zip(gam, rg)]

    single = all(ps[c_] == 1 for c_ in ("lblk", "inv", "apply", "state"))
    nar = (lambda x: x.astype(BF16)) if single else (lambda x: x)
    a_st = [nar(stack(x)) for x in a4]
    r_st = [nar(stack(x)) for x in r4]
    v_st = [nar(stack(x)) for x in vg]
    prod = [mm_l(jnp.concatenate([x, y_], axis=0), jnp.concatenate([z, q_], axis=0), _NT)
            for x, y_, z, q_ in zip(a_st, r_st, b4, k4)]
    low = _iota(sq, 1) < HEAD_DIM

    def diag_blocks(x, mask):
        xr = pltpu.roll(x, HEAD_DIM, axis=1)
        return jnp.where(mask, jnp.where(low, x, xr), 0.0), jnp.where(mask, jnp.where(low, xr, x), 0.0)

    l_bk = [diag_blocks(x[:st], strict) for x in prod]
    m_bk = [diag_blocks(x[st:], incl) for x in prod]
    l_b, l_k = [x[0] for x in l_bk], [nar(x[1]) for x in l_bk]
    m_b, m_k = [nar(x[0]) for x in m_bk], [nar(x[1]) for x in m_bk]
    t = [nar(eye + x) for x in l_b]
    l_bn = [nar(x) for x in l_b]
    pw = [nar(mm_i(x, x)) for x in l_bn]
    cov = 2
    while cov < C:
        if 2 * cov >= C:
            t = [nar(x + mm_i(p_, x)) for x, p_ in zip(t, pw)]
        else:
            both = [mm_i(p_, jnp.concatenate([p_, x], axis=1)) for x, p_ in zip(t, pw)]
            pw = [nar(x[:, :st]) for x in both]
            t = [nar(x + y_[:, st:]) for x, y_ in zip(t, both)]
        cov *= 2
    lm_v = [mm_a(jnp.concatenate([x, y_], axis=0), z) for x, y_, z in zip(l_k, m_k, v_st)]
    mkv4 = [fold(x[st:]) for x in lm_v]
    t_av = [mm_a(x, jnp.concatenate([y_, nar(z[:st])], axis=1)) for x, y_, z in zip(t, a_st, lm_v)]
    wm4 = [fold(x[:, :st]) for x in t_av]
    uk4 = [fold(x[:, st:]) for x in t_av]

    us = [[None] * nb for _ in gs]
    ys = [[None] * nb for _ in gs]
    for ck in range(ncs):
        cgs = range(ck * gpb, (ck + 1) * gpb)
        for b in range(nb):
            rows = slice(b * C, (b + 1) * C)
            s_old = {gi: sbd_ref[b, gi - ck * gpb] for gi in cgs}
            uy = {gi: mm_s(jnp.concatenate([wm4[gi][rows], r4[gi][rows]], axis=0), s_old[gi], _NT) for gi in cgs}
            for gi in cgs:
                us[gi][b] = uy[gi][:C] + uk4[gi][rows]
                ys[gi][b] = uy[gi][C:]
            ds = {gi: mm_s(jnp.concatenate([us[gi][b], vg[gi][rows]], axis=0),
                           jnp.concatenate([b4[gi][rows], k4[gi][rows]], axis=0), _TN) for gi in cgs}
            for gi in cgs:
                s_new = (s_old[gi] + jnp.where(same_head, ds[gi], 0.0)) * gam[gi][(b + 1) * C - 1:(b + 1) * C, :]
                sbd_ref[b, gi - ck * gpb] = s_new
                if ck == ncs - 1:
                    @pl.when(c == nsteps - 1)
                    def _fin():
                        hs = slice((gi - ck * gpb) * HEADS_PER_GROUP, (gi - ck * gpb + 1) * HEADS_PER_GROUP)
                        sout_ref[b, hs] = from_block_diag(s_new).reshape(HEADS_PER_GROUP, HEAD_DIM, HEAD_DIM)

    u = [x[0] if nb == 1 else jnp.concatenate(x, axis=0) for x in us]
    ysum = [x[0] if nb == 1 else jnp.concatenate(x, axis=0) for x in ys]
    yb = [fold(mm_s(x, stack(y_))) for x, y_ in zip(m_b, u)]
    y = [x + y_ + z for x, y_, z in zip(ysum, mkv4, yb)]

    ym = [_dot_xr(x, block_ones, passes=ps["gn"]) * (1.0 / HEAD_DIM) for x in y]
    yc = [x - y_ for x, y_ in zip(y, ym)]
    yv = [_dot_xr(x * x, block_ones, passes=ps["gn"]) * (1.0 / HEAD_DIM) for x in yc]
    bonus = [_dot_xr(rg[gi] * k2[gi] * rk_ref[:, sls[gi]], block_ones, passes=ps["gn"]) * vg[gi] for gi in gs]
    for gi in gs:
        sl, rw = sls[gi], rws[gi]
        yn = yc[gi] * lax.rsqrt(yv[gi] + GN_EPS) * gng_ref[:, sl] + gnb_ref[:, sl]
        o_ref[rw, sl] = (yn + bonus[gi]) * gate_all[rw, sl]


def _rwkv(proj, nseq, nchunk, C, prev_shift, state0, prm, gpb=None, ncs=None):
    nb = ROWS // C
    if gpb is None:
        gpb = N_GROUPS if nb == 1 else N_GROUPS // 2
    if ncs is None:
        ncs = 2 if nb == 1 and nchunk % 2 == 0 else 1
    assert nb * C == ROWS and nseq % nb == 0 and (nb == 1 or nchunk == 1) and nchunk % ncs == 0
    rows = nseq * nchunk * C
    nsteps = nchunk // ncs
    srows = ncs * ROWS
    w = gpb * GROUP_W
    sec = D_MODEL // w
    prows = 8 if nb == 1 else C
    pblk = 8 if nb == 1 else srows
    prev = jnp.zeros((nseq, prows, SHIFT_W), F32).at[:, 0].set(prev_shift).reshape(nseq * prows, SHIFT_W)
    prev_rkv = prev[:, :RKV_W]
    prev_lora = jnp.pad(prev[:, RKV_W:], ((0, 0), (0, LORA_PAD - LORA_W)))

    def tok(off):
        return pl.BlockSpec((srows, w), lambda s, g, c: (s * nsteps + c, off + g))

    def prv(off):
        return pl.BlockSpec((pblk, w), lambda s, g, c: (s, off + g))

    def vec(off=0):
        return pl.BlockSpec((1, w), lambda s, g, c: (0, off + g))

    up = pl.BlockSpec((LORA_PAD, w), lambda s, g, c: (0, g))
    in_specs = [
        tok(0), tok(sec), tok(2 * sec),
        pl.BlockSpec((srows, LORA_PAD), lambda s, g, c: (s * nsteps + c, COL_LORA // LORA_PAD)),
        prv(0), prv(sec), prv(2 * sec),
        pl.BlockSpec((pblk, LORA_PAD), lambda s, g, c: (s, 0)),
        pl.BlockSpec((nb, gpb * HEADS_PER_GROUP, HEAD_DIM, HEAD_DIM), lambda s, g, c: (s, g, 0, 0)),
        vec(0), vec(sec), vec(2 * sec),
        pl.BlockSpec((1, LORA_PAD), lambda s, g, c: (0, 0)),
        vec(), vec(), vec(), vec(), vec(), vec(), vec(),
        up, up, up,
    ]
    out_specs = [pl.BlockSpec((srows, w), lambda s, g, c: (s * nsteps + c, g)),
                 pl.BlockSpec((nb, gpb * HEADS_PER_GROUP, HEAD_DIM, HEAD_DIM), lambda s, g, c: (s, g, 0, 0))]
    out_shape = [jax.ShapeDtypeStruct((rows, D_MODEL), F32),
                 jax.ShapeDtypeStruct((nseq, D_MODEL // HEAD_DIM, HEAD_DIM, HEAD_DIM), F32)]
    operands = [proj, proj, proj, proj, prev_rkv, prev_rkv, prev_rkv, prev_lora, state0,
                prm["mu_rkv"], prm["mu_rkv"], prm["mu_rkv"], prm["mu_lora"],
                prm["w0"], prm["a0"], prm["k_k"], prm["k_a"], prm["r_k"], prm["gn_g"], prm["gn_b"],
                prm["w_up"], prm["a_up"], prm["g_up"]]
    return pl.pallas_call(
        functools.partial(_rwkv_body, C, nb, ncs, nsteps, gpb, RWKV_PASSES),
        grid=(nseq // nb, N_GROUPS // gpb, nsteps),
        in_specs=in_specs,
        out_specs=out_specs,
        out_shape=out_shape,
        scratch_shapes=[pltpu.VMEM((nb, gpb, GROUP_W, GROUP_W), F32),
                        pltpu.VMEM((srows + 8, 3 * w), F32),
                        pltpu.VMEM((srows + 8, LORA_PAD), F32)],
        compiler_params=_params(("arbitrary", "arbitrary", "arbitrary")),
        name="rwkv",
    )(*operands)


def _attn_rows(q, kcat, vcat, sink_ref, first_key, store, npr):
    m = q.shape[0]
    kt_all = kcat.T
    qi = _iota((npr * m, 2 * WINDOW), 0) & (m - 1)
    ki = _iota((npr * m, 2 * WINDOW), 1)
    allowed = (ki >= qi + 1) & (ki <= qi + WINDOW) & (ki >= first_key)
    lane = _iota((2 * WINDOW, 128), 1)
    ones_k = jnp.ones((2 * WINDOW, 128), BF16)
    zk = jnp.zeros((HEAD_DIM, 2 * WINDOW), F32)
    nkv = KV_W // HEAD_DIM
    k2s, v2s = [], []
    for kv in range(nkv):
        kt = kt_all[kv * HEAD_DIM:(kv + 1) * HEAD_DIM, :]
        k2s.append(jnp.concatenate([jnp.concatenate([kt, zk], axis=1),
                                    jnp.concatenate([zk, kt], axis=1)], axis=0).astype(BF16))
        slab = vcat[:, (kv // 2) * 128:(kv // 2 + 1) * 128]
        rolled = pltpu.roll(slab, HEAD_DIM, axis=1)
        lo_src, hi_src = (slab, rolled) if kv % 2 == 0 else (rolled, slab)
        v2s.append(jnp.concatenate([jnp.where(lane < HEAD_DIM, lo_src, 0.0),
                                    jnp.where(lane >= HEAD_DIM, hi_src, 0.0)], axis=0).astype(BF16))
    items = [(kv, p0) for kv in range(nkv) for p0 in range(0, 4, npr)]
    cols = [[kv * 512 + (p0 + i) * 128 for i in range(npr)] for kv, p0 in items]
    scores = [_bdot((jnp.concatenate([q[:, col:col + 128] for col in cs], axis=0) * ATT_SCALE).astype(BF16),
                    k2s[kv]) for (kv, _), cs in zip(items, cols)]
    probs = []
    for (kv, p0), s in zip(items, scores):
        ps = []
        for hb in range(2):
            sink = jnp.concatenate(
                [jnp.broadcast_to(sink_ref[:, (kv * 8 + (p0 + i) * 2 + hb) * 128:
                                           (kv * 8 + (p0 + i) * 2 + hb) * 128 + 1], (m, 1))
                 for i in range(npr)], axis=0)
            sh = jnp.where(allowed, s[:, hb * 256:(hb + 1) * 256], -jnp.inf)
            mx = jnp.maximum(jnp.max(sh, axis=1, keepdims=True), sink)
            e = jnp.exp(sh - mx)
            inv = 1.0 / (_dot_xr(e, ones_k, passes=2) + jnp.exp(sink - mx))
            ps += [e[:, :128] * inv, e[:, 128:] * inv]
        probs.append(jnp.concatenate(ps, axis=1).astype(BF16))
    outs = [_bdot(p, v2s[kv]) for (kv, _), p in zip(items, probs)]
    for cs, o in zip(cols, outs):
        for i, col in enumerate(cs):
            store(col, o[i * m:(i + 1) * m])


def _attn_prompt_body(q_ref, kvc_ref, kvp_ref, sink_ref, o_ref):
    n = pl.program_id(0)
    kvc = kvc_ref[...]
    kvp = kvp_ref[...]
    kcat = jnp.concatenate([kvp[:, :KV_W], kvc[:, :KV_W]], axis=0)
    vcat = jnp.concatenate([kvp[:, KV_W:], kvc[:, KV_W:]], axis=0)

    def store(col, val):
        o_ref[:, col:col + 128] = val

    _attn_rows(q_ref[...], kcat, vcat, sink_ref, jnp.where(n > 0, 0, WINDOW), store, npr=1)


def _attn_prompt(proj, sinks_e):
    t = proj.shape[0]
    nb = t // WINDOW
    qb, kvb = COL_Q // D_MODEL, COL_KV // (2 * KV_W)
    return pl.pallas_call(
        _attn_prompt_body,
        grid=(nb,),
        in_specs=[pl.BlockSpec((WINDOW, D_MODEL), lambda n: (n, qb)),
                  pl.BlockSpec((WINDOW, 2 * KV_W), lambda n: (n, kvb)),
                  pl.BlockSpec((WINDOW, 2 * KV_W), lambda n: (jnp.maximum(n - 1, 0), kvb)),
                  pl.BlockSpec((1, 32 * 128), lambda n: (0, 0))],
        out_specs=pl.BlockSpec((WINDOW, D_MODEL), lambda n: (n, 0)),
        out_shape=jax.ShapeDtypeStruct((t, D_MODEL), F32),
        compiler_params=_params(("parallel",)),
        name="attn_prompt",
    )(proj, proj, proj, sinks_e)


def _attn_sample_body(L, bt, q_ref, kvn_ref, ck_ref, cv_ref, sink_ref, o_ref):
    pad = jnp.zeros((WINDOW - L, KV_W), F32)

    def one(b, carry):
        rows = pl.ds(pl.multiple_of(b * L, L), L)
        kvn = kvn_ref[rows, :]
        kcat = jnp.concatenate([ck_ref[b], kvn[:, :KV_W], pad], axis=0)
        vcat = jnp.concatenate([cv_ref[b], kvn[:, KV_W:], pad], axis=0)

        def store(col, val):
            o_ref[rows, col:col + 128] = val

        _attn_rows(q_ref[rows, :], kcat, vcat, sink_ref, 0, store, npr=4)
        return carry

    lax.fori_loop(0, bt, one, 0, unroll=2)


def _attn_sample(proj, ck, cv, sinks_e, L, bt=ATTN_SAMPLE_SEQS):
    rows = proj.shape[0]
    nb = rows // L
    bt = min(bt, nb)
    qb, kvb = COL_Q // D_MODEL, COL_KV // (2 * KV_W)
    return pl.pallas_call(
        functools.partial(_attn_sample_body, L, bt),
        grid=(nb // bt,),
        in_specs=[pl.BlockSpec((bt * L, D_MODEL), lambda i: (i, qb)),
                  pl.BlockSpec((bt * L, 2 * KV_W), lambda i: (i, kvb)),
                  pl.BlockSpec((bt, WINDOW, KV_W), lambda i: (i, 0, 0)),
                  pl.BlockSpec((bt, WINDOW, KV_W), lambda i: (i, 0, 0)),
                  pl.BlockSpec((1, 32 * 128), lambda i: (0, 0))],
        out_specs=pl.BlockSpec((bt * L, D_MODEL), lambda i: (i, 0)),
        out_shape=jax.ShapeDtypeStruct((rows, D_MODEL), F32),
        compiler_params=_params(("parallel",)),
        name="attn_sample",
    )(proj, proj, ck, cv, sinks_e)


def _layer_norm(x, g, b):
    mu = jnp.mean(x, axis=-1, keepdims=True)
    xc = x - mu
    var = jnp.mean(xc * xc, axis=-1, keepdims=True)
    return xc * lax.rsqrt(var + LN_EPS) * g + b


def _merge_body(ga_ref, gb_ref, oa_ref, ob_ref, x_ref, wout_ref, g_ref, b_ref, h_ref, ht_ref):
    mixed = _sigmoid(ga_ref[...]) * oa_ref[...] + _sigmoid(gb_ref[...]) * ob_ref[...]
    y = _bdot(mixed.astype(BF16), wout_ref[...])
    h = _layer_norm(DN_ALPHA * x_ref[...] + y, g_ref[...], b_ref[...])
    h_ref[...] = h
    ht_ref[...] = h.T.astype(BF16)


def _merge(proj, o_a, o_b, x2d, wout_bf16, ln_g, ln_b, tm=MERGE_TM):
    m = x2d.shape[0]
    tm = min(tm, m)
    gab = COL_GATE // D_MODEL
    row = lambda i: (i, 0)
    return pl.pallas_call(
        _merge_body,
        grid=(m // tm,),
        in_specs=[pl.BlockSpec((tm, D_MODEL), lambda i: (i, gab)),
                  pl.BlockSpec((tm, D_MODEL), lambda i: (i, gab + 1)),
                  pl.BlockSpec((tm, D_MODEL), row),
                  pl.BlockSpec((tm, D_MODEL), row),
                  pl.BlockSpec((tm, D_MODEL), row),
                  pl.BlockSpec((D_MODEL, D_MODEL), lambda i: (0, 0)),
                  pl.BlockSpec((1, D_MODEL), lambda i: (0, 0)),
                  pl.BlockSpec((1, D_MODEL), lambda i: (0, 0))],
        out_specs=[pl.BlockSpec((tm, D_MODEL), row),
                   pl.BlockSpec((D_MODEL, tm), lambda i: (0, i))],
        out_shape=[jax.ShapeDtypeStruct((m, D_MODEL), F32),
                   jax.ShapeDtypeStruct((D_MODEL, m), BF16)],
        compiler_params=_params(("parallel",)),
        name="merge",
    )(proj, proj, o_a, o_b, x2d, wout_bf16, ln_g, ln_b)


def _top_rows(x, count):
    rows = _iota(x.shape, 0)
    out = []
    for _ in range(count):
        mx = jnp.max(x, axis=0, keepdims=True)
        out.append(mx)
        first = jnp.min(jnp.where(x == mx, rows, x.shape[0]), axis=0, keepdims=True)
        x = jnp.where(rows == first, -jnp.inf, x)
    return out


def _bitonic_network(n):
    comps = []
    k = 2
    while k <= n:
        j = k // 2
        while j >= 1:
            comps += [(i, i ^ j, (i & k) == 0) for i in range(n) if (i ^ j) > i]
            j //= 2
        k *= 2
    return comps


def _top16_of_128(xs):
    nt = N_KEYS // 8
    tiles = [[x[8 * j:8 * (j + 1)] for j in range(nt)] for x in xs]
    for i, l, desc in _bitonic_network(nt):
        for t in tiles:
            hi, lo = jnp.maximum(t[i], t[l]), jnp.minimum(t[i], t[l])
            t[i], t[l] = (hi, lo) if desc else (lo, hi)
    sub = _iota(tiles[0][0].shape, 0)
    outs = [[] for _ in xs]
    for r in range(PEER_TOPK):
        for t, o in zip(tiles, outs):
            mx = jnp.max(t[0], axis=0, keepdims=True)
            o.append(mx)
            first = jnp.min(jnp.where(t[0] == mx, sub, 8), axis=0, keepdims=True)
            pop = sub == first
            for j in range(PEER_TOPK - r - 1):
                t[j] = jnp.where(pop, t[j + 1], t[j])
    return outs


def _route_body(tm, h_ref, wq_ref, keys_ref, s0_ref, s1_ref, aux_ref, q_s):
    q = _bdot(h_ref[...].astype(BF16), wq_ref[...])
    for hc in range(2 * PEER_HEADS):
        q_s[hc] = q[:, hc * N_KEYS:(hc + 1) * N_KEYS]
    pairs = [(i, j) for i in range(PEER_TOPK) for j in range(PEER_TOPK) if (i + 1) * (j + 1) <= PEER_TOPK]
    npad = -len(pairs) % 8
    neg = jnp.full((1, tm), -jnp.inf, F32)
    zero = jnp.zeros((1, tm), F32)

    def head(hh, carry):
        s0 = _bdot(keys_ref[hh, 0], q_s[2 * hh].astype(BF16), _NT)
        s1 = _bdot(keys_ref[hh, 1], q_s[2 * hh + 1].astype(BF16), _NT)
        s0_ref[hh] = s0
        for lb in range(tm // 128):
            s1_ref[hh, lb] = s1[:, lb * 128:(lb + 1) * 128]
        top0, top1 = _top16_of_128([s0, s1])
        cand = jnp.concatenate([top0[i] + top1[j] for i, j in pairs] + [neg] * npad, axis=0)
        best = _top_rows(cand, PEER_TOPK)
        mx = best[0]
        z = zero
        for bsum in best:
            z = z + jnp.exp(bsum - mx)
        aux_ref[hh] = jnp.concatenate([best[-1], top0[0], top1[0], 1.0 / z, zero, zero, zero, zero], axis=0)
        return carry

    lax.fori_loop(0, PEER_HEADS, head, 0, unroll=2)


def _route(h, wq_bf16, keys_bf16, tm=ROUTE_TM):
    m = h.shape[0]
    tm = min(tm, m)
    tok3 = lambda i: (0, 0, i)
    return pl.pallas_call(
        functools.partial(_route_body, tm),
        grid=(m // tm,),
        in_specs=[pl.BlockSpec((tm, D_MODEL), lambda i: (i, 0)),
                  pl.BlockSpec((D_MODEL, D_MODEL), lambda i: (0, 0)),
                  pl.BlockSpec((PEER_HEADS, 2, N_KEYS, N_KEYS), lambda i: (0, 0, 0, 0))],
        out_specs=[pl.BlockSpec((PEER_HEADS, N_KEYS, tm), tok3),
                   pl.BlockSpec((PEER_HEADS, tm // 128, N_KEYS, 128), lambda i: (0, i, 0, 0)),
                   pl.BlockSpec((PEER_HEADS, 8, tm), tok3)],
        out_shape=[jax.ShapeDtypeStruct((PEER_HEADS, N_KEYS, m), F32),
                   jax.ShapeDtypeStruct((PEER_HEADS, m // 128, N_KEYS, 128), F32),
                   jax.ShapeDtypeStruct((PEER_HEADS, 8, m), F32)],
        scratch_shapes=[pltpu.VMEM((2 * PEER_HEADS, tm, N_KEYS), F32)],
        compiler_params=_params(("parallel",)),
        name="peer_route",
    )(h, wq_bf16, keys_bf16)


def _peer_body(te, ne, ht_ref, u_ref, vt_ref, s0_ref, s1_ref, aux_ref, h_ref, g_ref, b_ref, y_ref,
               acc_ref, e1_ref, cf_ref, st0_ref, st1_ref, w0_ref, w1_ref):
    j = pl.program_id(1)
    tm = acc_ref.shape[1]

    @pl.when(j == 0)
    def _():
        acc_ref[...] = jnp.zeros_like(acc_ref)
        for ref in (st0_ref, st1_ref, w0_ref, w1_ref):
            ref[...] = jnp.zeros_like(ref)
        for hh in range(PEER_HEADS):
            aux = aux_ref[hh]
            for lb in range(tm // 128):
                e1_ref[hh, lb] = jnp.exp(s1_ref[hh, lb] - aux[2:3, lb * 128:(lb + 1) * 128])
            cf_ref[hh] = jnp.exp(s0_ref[hh] - aux[1:2, :]) * aux[3:4, :]

    def stages(st_new, st_old, w_new, w_old):
        jb = j - 1
        live = (jb >= 0) & (jb < ne)
        jb_c = jnp.clip(jb, 0, ne - 1)
        taus = [jnp.where(live, aux_ref[hh, 0:1, :], jnp.inf) for hh in range(PEER_HEADS)]
        ncc = te // N_KEYS
        s0rows = [[s0_ref[hh, pl.ds(jb_c * ncc + cc, 1), :] for hh in range(PEER_HEADS)] for cc in range(ncc)]
        cfrows = [[cf_ref[hh, pl.ds(jb_c * ncc + cc, 1), :] for hh in range(PEER_HEADS)] for cc in range(ncc)]
        tw = min(256, tm)
        mxu_pieces, gate_tiles = [], []
        for t0 in range(0, tm, tw):
            ts = slice(t0, t0 + tw)
            lbs = range(t0 // 128, (t0 + tw) // 128)

            def mix(mr, ts=ts, lbs=lbs):
                acc_ref[mr, ts] += _bdot(vt_ref[mr, :], jnp.concatenate([w_old[lb] for lb in lbs], axis=1))

            def score(er, t0=t0, ts=ts, lbs=lbs):
                st = _bdot(u_ref[er, :], ht_ref[:, ts])
                for lb in lbs:
                    st_new[lb, er, :] = st[:, lb * 128 - t0:(lb + 1) * 128 - t0]

            for q in range(4):
                mxu_pieces.append(functools.partial(mix, slice(q * (D_MODEL // 4), (q + 1) * (D_MODEL // 4))))
            for q in range(2):
                mxu_pieces.append(functools.partial(score, slice(q * (te // 2), (q + 1) * (te // 2))))

            def gate(lb, cc):
                ls = slice(lb * 128, (lb + 1) * 128)
                rows = slice(cc * N_KEYS, (cc + 1) * N_KEYS)
                gsum = None
                for hh in range(PEER_HEADS):
                    sel = (s1_ref[hh, lb] + s0rows[cc][hh][:, ls]) >= taus[hh][:, ls]
                    term = jnp.where(sel, e1_ref[hh, lb] * cfrows[cc][hh][:, ls], 0.0)
                    gsum = term if gsum is None else gsum + term
                so = st_old[lb, rows, :]
                act = 0.5 * so * (1.0 + lax.erf(so * np.float32(np.sqrt(0.5))))
                w_new[lb, rows, :] = (gsum * act).astype(BF16)

            for lb in lbs:
                for cc in range(ncc):
                    gate_tiles.append(functools.partial(gate, lb, cc))
        for i in range(max(len(mxu_pieces), len(gate_tiles))):
            if i < len(gate_tiles):
                gate_tiles[i]()
            if i < len(mxu_pieces):
                mxu_pieces[i]()

    @pl.when(j % 2 == 0)
    def _():
        stages(st0_ref, st1_ref, w1_ref, w0_ref)

    @pl.when(j % 2 == 1)
    def _():
        stages(st1_ref, st0_ref, w0_ref, w1_ref)

    @pl.when(j == ne + 1)
    def _():
        out = acc_ref[...].T
        y_ref[...] = _layer_norm(DN_ALPHA * h_ref[...] + out, g_ref[...], b_ref[...])


def _peer(ht, u_bf16, vt_bf16, s0t, s1t, aux, h, ln_g, ln_b, tm=PEER_TM, te=PEER_TE):
    m = h.shape[0]
    tm = min(tm, m)
    ne = N_EXPERTS // te
    tok3 = lambda i, j: (0, 0, i)
    return pl.pallas_call(
        functools.partial(_peer_body, te, ne),
        grid=(m // tm, ne + 2),
        in_specs=[pl.BlockSpec((D_MODEL, tm), lambda i, j: (0, i)),
                  pl.BlockSpec((te, D_MODEL), lambda i, j: (jnp.minimum(j, ne - 1), 0)),
                  pl.BlockSpec((D_MODEL, te), lambda i, j: (0, jnp.clip(j - 2, 0, ne - 1))),
                  pl.BlockSpec((PEER_HEADS, N_KEYS, tm), tok3),
                  pl.BlockSpec((PEER_HEADS, tm // 128, N_KEYS, 128), lambda i, j: (0, i, 0, 0)),
                  pl.BlockSpec((PEER_HEADS, 8, tm), tok3),
                  pl.BlockSpec((tm, D_MODEL), lambda i, j: (i, 0)),
                  pl.BlockSpec((1, D_MODEL), lambda i, j: (0, 0)),
                  pl.BlockSpec((1, D_MODEL), lambda i, j: (0, 0))],
        out_specs=pl.BlockSpec((tm, D_MODEL), lambda i, j: (i, 0)),
        out_shape=jax.ShapeDtypeStruct((m, D_MODEL), F32),
        scratch_shapes=[pltpu.VMEM((D_MODEL, tm), F32),
                        pltpu.VMEM((PEER_HEADS, tm // 128, N_KEYS, 128), F32),
                        pltpu.VMEM((PEER_HEADS, N_KEYS, tm), F32),
                        pltpu.VMEM((tm // 128, te, 128), F32),
                        pltpu.VMEM((tm // 128, te, 128), F32),
                        pltpu.VMEM((tm // 128, te, 128), BF16),
                        pltpu.VMEM((tm // 128, te, 128), BF16)],
        compiler_params=_params(("parallel", "arbitrary")),
        name="peer_dense",
    )(ht, u_bf16, vt_bf16, s0t, s1t, aux, h, ln_g, ln_b)


def _pad_rows(w, lo, total):
    return jnp.zeros((total, w.shape[1]), w.dtype).at[lo:lo + w.shape[0]].set(w)


def _layer(x2d, nseq, seq_len, chunk, prev_shift, state0, cache, wts):
    proj_a = _inproj(x2d, wts["w_in_t"], PROJ_A_W)
    proj_b = _inproj(x2d, wts["w_in_b"], PROJ_B_W)
    o_a, s_last = _rwkv(proj_a, nseq, seq_len // chunk, chunk, prev_shift, state0, wts)
    if cache is None:
        o_b = _attn_prompt(proj_b, wts["sinks"])
    else:
        o_b = _attn_sample(proj_b, cache[0], cache[1], wts["sinks"], seq_len)
    h, ht = _merge(proj_b, o_a, o_b, x2d, wts["w_out"], wts["ln1_g"], wts["ln1_b"])
    s0t, s1t, aux = _route(h, wts["peer_wq"], wts["peer_keys"])
    y = _peer(ht, wts["peer_u"], wts["peer_vt"], s0t, s1t, aux, h, wts["ln2_g"], wts["ln2_b"])
    return y, proj_a, proj_b, s_last


def kernel(x_prompt, x_sample, state_shift, state_wkv, cache_k, cache_v, w_in, rw_mu, rw_w0, rw_w_up, rw_a0,
           rw_a_up, rw_g_up, rw_k_k, rw_k_a, rw_r_k, rw_gn_g, rw_gn_b, att_sinks, w_out, ln1_g, ln1_b,
           peer_wq, peer_keys, peer_u, peer_v, ln2_g, ln2_b):
    depth = w_in.shape[0]
    assert depth == 1
    l = 0
    bp, tp, _ = x_prompt.shape
    bs, ts, _ = x_sample.shape
    assert bp == 1
    win = cache_k.shape[2]
    assert win == WINDOW and tp % WINDOW == 0 and ts <= 8

    w_t = jnp.transpose(w_in[l]).astype(BF16)
    q0 = SHIFT_W
    k0 = q0 + D_MODEL
    g0 = k0 + 2 * KV_W
    w_b = jnp.concatenate([w_t[q0:q0 + D_MODEL], w_t[g0:g0 + 2 * D_MODEL], w_t[k0:k0 + 2 * KV_W]], axis=0)
    mu = rw_mu[l]
    row = lambda v: v.reshape(1, -1)
    wts = dict(
        w_in_t=w_t, w_in_b=w_b,
        mu_rkv=row(mu[:RKV_W]),
        mu_lora=row(jnp.pad(mu[RKV_W:], (0, LORA_PAD - LORA_W))),
        w0=row(rw_w0[l]), a0=row(rw_a0[l]), k_k=row(rw_k_k[l]), k_a=row(rw_k_a[l]), r_k=row(rw_r_k[l]),
        gn_g=row(rw_gn_g[l]), gn_b=row(rw_gn_b[l]),
        w_up=_pad_rows(rw_w_up[l], 0, LORA_PAD).astype(BF16),
        a_up=_pad_rows(rw_a_up[l], 96, LORA_PAD).astype(BF16),
        g_up=_pad_rows(rw_g_up[l], 192, LORA_PAD).astype(BF16),
        sinks=jnp.repeat(att_sinks[l], 128).reshape(1, 32 * 128),
        w_out=w_out[l].astype(BF16),
        ln1_g=row(ln1_g[l]), ln1_b=row(ln1_b[l]), ln2_g=row(ln2_g[l]), ln2_b=row(ln2_b[l]),
        peer_wq=peer_wq[l].astype(BF16),
        peer_keys=peer_keys[l].astype(BF16),
        peer_u=peer_u[l].astype(BF16),
        peer_vt=peer_v[l].T.astype(BF16),
    )

    chunk_p = 64
    y_p, pa_p, pb_p, s_p = _layer(x_prompt[0], 1, tp, chunk_p,
                                  jnp.zeros((1, SHIFT_W), F32),
                                  jnp.zeros((1, D_MODEL // HEAD_DIM, HEAD_DIM, HEAD_DIM), F32), None, wts)
    keep = min(WINDOW, tp)
    kv_p = pb_p[tp - keep:, COL_KV:]
    y_prompt = y_p[None]
    new_shift_prompt = pa_p[tp - 1, :SHIFT_W][None, None]
    new_wkv_prompt = s_p[None]
    new_k_prompt = kv_p[:, :KV_W].reshape(1, 1, keep, KV_W // HEAD_DIM, HEAD_DIM)
    new_v_prompt = kv_p[:, KV_W:].reshape(1, 1, keep, KV_W // HEAD_DIM, HEAD_DIM)

    ck = cache_k[l].reshape(bs, win, KV_W)
    cv = cache_v[l].reshape(bs, win, KV_W)
    y_s, pa_s, pb_s, s_s = _layer(x_sample.reshape(bs * ts, D_MODEL), bs, ts, ts,
                                  state_shift[l], state_wkv[l], (ck, cv), wts)
    y_sample = y_s.reshape(bs, ts, D_MODEL)
    new_shift_sample = pa_s.reshape(bs, ts, PROJ_A_W)[:, ts - 1, :SHIFT_W][None]
    new_wkv_sample = s_s[None]
    kv_s = pb_s.reshape(bs, ts, PROJ_B_W)[:, :, COL_KV:]
    new_k_sample = jnp.concatenate([ck, kv_s[:, :, :KV_W]], axis=1)[:, ts:].reshape(
        1, bs, win, KV_W // HEAD_DIM, HEAD_DIM)
    new_v_sample = jnp.concatenate([cv, kv_s[:, :, KV_W:]], axis=1)[:, ts:].reshape(
        1, bs, win, KV_W // HEAD_DIM, HEAD_DIM)

    return (y_prompt, y_sample, new_shift_prompt, new_wkv_prompt, new_k_prompt, new_v_prompt,
            new_shift_sample, new_wkv_sample, new_k_sample, new_v_sample)
```

```python
import functools

import numpy as np
import jax
import jax.numpy as jnp
from jax import lax
from jax.experimental import pallas as pl
from jax.experimental.pallas import tpu as pltpu

F32, BF16 = jnp.float32, jnp.bfloat16

D_MODEL = 2048
HEAD_DIM = 64
HEADS_PER_GROUP = 2
GROUP_W = HEADS_PER_GROUP * HEAD_DIM
N_GROUPS = D_MODEL // GROUP_W
RKV_W = 3 * D_MODEL
LORA_W = 96 + 96 + 256
LORA_PAD = 512
SHIFT_W = RKV_W + LORA_W
KV_W = 256
GN_EPS = 64e-5
LN_EPS = 1e-5
WINDOW = 128
ATT_SCALE = HEAD_DIM ** -0.5
N_KEYS = 128
N_EXPERTS = N_KEYS * N_KEYS
PEER_HEADS = 8
PEER_TOPK = 16
DN_ALPHA = 2.0 ** 0.25

COL_LORA = RKV_W
PROJ_A_W = RKV_W + LORA_PAD
COL_Q = 0
COL_GATE = D_MODEL
COL_KV = 3 * D_MODEL
PROJ_B_W = COL_KV + 2 * KV_W

VMEM_LIMIT = 56 * 1024 * 1024

INPROJ_TM, INPROJ_TN = 1024, 1664
MERGE_TM = 256
ROUTE_TM = 512
PEER_TM, PEER_TE = 512, 512
ATTN_SAMPLE_SEQS = 16

_NN = (((1,), (0,)), ((), ()))
_NT = (((1,), (1,)), ((), ()))
_TN = (((0,), (0,)), ((), ()))


def _bdot(a, b, dn=_NN):
    return lax.dot_general(a, b, dn, preferred_element_type=F32)


def _dot(a, b, dn=_NN):
    return _bdot(a.astype(BF16), b.astype(BF16), dn)


def _split(x, n):
    parts, r = [], x
    for i in range(n):
        p = r.astype(BF16)
        parts.append(p)
        if i + 1 < n:
            r = r - p.astype(F32)
    return parts


def _dot3(a, b, dn=_NN):
    a1, a2 = _split(a, 2)
    b1, b2 = _split(b, 2)
    return (_bdot(a1, b2, dn) + _bdot(a2, b1, dn)) + _bdot(a1, b1, dn)


def _dot_xl(a_bf16, b, dn=_NN, passes=3):
    out = None
    for part in reversed(_split(b, passes)):
        t = _bdot(a_bf16, part, dn)
        out = t if out is None else out + t
    return out


def _dot_xr(a, b_bf16, dn=_NN, passes=3):
    out = None
    for part in reversed(_split(a, passes)):
        t = _bdot(part, b_bf16, dn)
        out = t if out is None else out + t
    return out


def _mm(passes):
    return _dot if passes == 1 else _dot3


def _sigmoid(x):
    return 1.0 / (1.0 + jnp.exp(-x))


def _iota(shape, dim):
    return lax.broadcasted_iota(jnp.int32, shape, dim)


def _head_of(idx):
    return jnp.right_shift(idx, 6)


def _ones_where(mask):
    return jnp.where(mask, 1.0, 0.0).astype(BF16)


def _params(sem):
    return pltpu.CompilerParams(dimension_semantics=sem, vmem_limit_bytes=VMEM_LIMIT)


def _inproj_body(x_ref, w_ref, o_ref, xb_ref):
    @pl.when(pl.program_id(1) == 0)
    def _():
        xb_ref[...] = x_ref[...].astype(BF16)

    o_ref[...] = _bdot(xb_ref[...], w_ref[...].astype(BF16), _NT)


def _inproj(x2d, w_t, n, tn=INPROJ_TN):
    assert n % tn == 0 and tn % 128 == 0
    m = x2d.shape[0]
    tm = min(m, INPROJ_TM)
    return pl.pallas_call(
        _inproj_body,
        grid=(m // tm, n // tn),
        in_specs=[pl.BlockSpec((tm, D_MODEL), lambda i, j: (i, 0)),
                  pl.BlockSpec((tn, D_MODEL), lambda i, j: (j, 0))],
        out_specs=pl.BlockSpec((tm, tn), lambda i, j: (i, j)),
        out_shape=jax.ShapeDtypeStruct((m, n), F32),
        scratch_shapes=[pltpu.VMEM((tm, D_MODEL), BF16)],
        compiler_params=_params(("parallel", "arbitrary")),
        name="inproj",
    )(x2d, w_t)


RWKV_PASSES = dict(seg=1, cum=3, lblk=1, inv=1, apply=1, state=1, gn=2)
ROWS = 64


def _rwkv_body(C, nb, ncs, nsteps, gpb, ps,
               r_ref, k_ref, v_ref, lo_ref, pr_ref, pk_ref, pv_ref, plo_ref, s0_ref,
               mur_ref, muk_ref, muv_ref, mulo_ref, w0_ref, a0_ref, kk_ref, ka_ref, rk_ref, gng_ref, gnb_ref,
               wup_ref, aup_ref, gup_ref,
               o_ref, sout_ref,
               sbd_ref, sh_ref, shl_ref):
    assert HEADS_PER_GROUP == 2 and HEADS_PER_GROUP * ROWS == GROUP_W and (ncs == 1 or nb == 1)
    c = pl.program_id(2)
    w = gpb * GROUP_W
    srows = ncs * ROWS
    log2c = C.bit_length() - 1
    st = HEADS_PER_GROUP * ROWS
    sq = (st, st)
    ri, ci = _iota(sq, 0), _iota(sq, 1)
    same_head = _head_of(ri) == _head_of(ci)
    tr, tc = ri & (ROWS - 1), ci & (ROWS - 1)
    blk = same_head & (jnp.right_shift(tr, log2c) == jnp.right_shift(tc, log2c))
    strict = blk & (tr > tc)
    incl = blk & (tr >= tc)
    eye = jnp.where(ri == ci, 1.0, 0.0)
    block_ones = _ones_where(same_head)
    r64, c64 = _iota((ROWS, ROWS), 0), _iota((ROWS, ROWS), 1)
    tri_seq = _ones_where((jnp.right_shift(r64, log2c) == jnp.right_shift(c64, log2c)) & (r64 >= c64))
    lane_head = _head_of(_iota((ROWS, GROUP_W), 1))
    first_head_rows = ri < HEAD_DIM

    def to_block_diag(s2):
        wide = jnp.concatenate([s2, jnp.zeros_like(s2)], axis=1)
        return jnp.where(first_head_rows, wide, pltpu.roll(wide, HEAD_DIM, axis=1))

    def from_block_diag(s):
        return jnp.where(first_head_rows, s, pltpu.roll(s, HEAD_DIM, axis=1))[:, :HEAD_DIM]
    mm_l, mm_i, mm_a, mm_s = _mm(ps["lblk"]), _mm(ps["inv"]), _mm(ps["apply"]), _mm(ps["state"])

    @pl.when(c == 0)
    def _init():
        if nb == 1:
            sh_ref[7:8, 0:w] = pr_ref[0:1, :]
            sh_ref[7:8, w:2 * w] = pk_ref[0:1, :]
            sh_ref[7:8, 2 * w:3 * w] = pv_ref[0:1, :]
            shl_ref[7:8, :] = plo_ref[0:1, :]
        for b in range(nb):
            for gi in range(gpb):
                s0 = s0_ref[b, gi * HEADS_PER_GROUP:(gi + 1) * HEADS_PER_GROUP].reshape(GROUP_W, HEAD_DIM)
                sbd_ref[b, gi] = to_block_diag(s0)

    def lerp(x, ref, lo, hi, prev_ref, mu):
        ref[pl.ds(8, srows), lo:hi] = x
        xs = ref[pl.ds(7, srows), lo:hi]
        ref[7:8, lo:hi] = x[srows - 1:srows, :]
        if nb > 1:
            first = (_iota(x.shape, 0) & (C - 1)) == 0
            xs = jnp.where(first, prev_ref[...], xs)
        return x + (xs - x) * mu

    r = lerp(r_ref[...], sh_ref, 0, w, pr_ref, mur_ref[...])
    k = lerp(k_ref[...], sh_ref, w, 2 * w, pk_ref, muk_ref[...])
    v = lerp(v_ref[...], sh_ref, 2 * w, 3 * w, pv_ref, muv_ref[...])
    lo = lerp(lo_ref[...], shl_ref, 0, LORA_PAD, plo_ref, mulo_ref[...])

    wl = w0_ref[...] + _dot(jnp.tanh(lo), wup_ref[...])
    w_log = -(jnp.maximum(-wl, 0.0) + jnp.log1p(jnp.exp(-jnp.abs(wl)))) - 0.5
    logw_all = -jnp.exp(w_log)
    a_all = _sigmoid(a0_ref[...] + _dot(lo, aup_ref[...]))
    gate_all = _dot(_sigmoid(lo), gup_ref[...])
    kk_all = k * kk_ref[...]
    k2_all = k * (1.0 + (a_all - 1.0) * ka_ref[...])

    def stack(x):
        return jnp.concatenate([jnp.where(lane_head == j, x, 0.0) for j in range(HEADS_PER_GROUP)], axis=0)

    def fold(x):
        return x[0:ROWS] + x[ROWS:2 * ROWS]

    gs = range(ncs * gpb)
    sls = [slice(gi * GROUP_W, (gi + 1) * GROUP_W) for _ in range(ncs) for gi in range(gpb)]
    rws = [slice(ck * ROWS, (ck + 1) * ROWS) for ck in range(ncs) for _ in range(gpb)]
    rg = [r[rw, sl] for rw, sl in zip(rws, sls)]
    vg = [v[rw, sl] for rw, sl in zip(rws, sls)]
    k2 = [k2_all[rw, sl] for rw, sl in zip(rws, sls)]
    kk = [kk_all[rw, sl] for rw, sl in zip(rws, sls)]
    ss = [_dot_xr(x * x, block_ones, passes=ps["seg"]) for x in kk]
    kkn = [x / jnp.maximum(jnp.sqrt(s2), 1e-12) for x, s2 in zip(kk, ss)]
    logw = [logw_all[rw, sl] for rw, sl in zip(rws, sls)]
    cum = [_dot_xl(tri_seq, x, passes=ps["cum"]) for x in logw]
    gam = [jnp.exp(x) for x in cum]
    ginv = [jnp.exp(-x) for x in cum]
    a4 = [-(jnp.exp(cm - lw) * kn) for cm, lw, kn in zip(cum, logw, kkn)]
    b4 = [kn * a_all[rw, sl] * gi_ for kn, rw, sl, gi_ in zip(kkn, rws, sls, ginv)]
    k4 = [x * gi_ for x, gi_ in zip(k2, ginv)]
    r4 = [gm * x for gm, x in zip(gam, rg)]

    single = all(ps[c_] == 1 for c_ in ("lblk", "inv", "apply", "state"))
    nar = (lambda x: x.astype(BF16)) if single else (lambda x: x)
    a_st = [nar(stack(x)) for x in a4]
    r_st = [nar(stack(x)) for x in r4]
    v_st = [nar(stack(x)) for x in vg]
    prod = [mm_l(jnp.concatenate([x, y_], axis=0), jnp.concatenate([z, q_], axis=0), _NT)
            for x, y_, z, q_ in zip(a_st, r_st, b4, k4)]
    low = _iota(sq, 1) < HEAD_DIM

    def diag_blocks(x, mask):
        xr = pltpu.roll(x, HEAD_DIM, axis=1)
        return jnp.where(mask, jnp.where(low, x, xr), 0.0), jnp.where(mask, jnp.where(low, xr, x), 0.0)

    l_bk = [diag_blocks(x[:st], strict) for x in prod]
    m_bk = [diag_blocks(x[st:], incl) for x in prod]
    l_b, l_k = [x[0] for x in l_bk], [nar(x[1]) for x in l_bk]
    m_b, m_k = [nar(x[0]) for x in m_bk], [nar(x[1]) for x in m_bk]
    t = [nar(eye + x) for x in l_b]
    l_bn = [nar(x) for x in l_b]
    pw = [nar(mm_i(x, x)) for x in l_bn]
    cov = 2
    while cov < C:
        if 2 * cov >= C:
            t = [nar(x + mm_i(p_, x)) for x, p_ in zip(t, pw)]
        else:
            both = [mm_i(p_, jnp.concatenate([p_, x], axis=1)) for x, p_ in zip(t, pw)]
            pw = [nar(x[:, :st]) for x in both]
            t = [nar(x + y_[:, st:]) for x, y_ in zip(t, both)]
        cov *= 2
    lm_v = [mm_a(jnp.concatenate([x, y_], axis=0), z) for x, y_, z in zip(l_k, m_k, v_st)]
    mkv4 = [fold(x[st:]) for x in lm_v]
    t_av = [mm_a(x, jnp.concatenate([y_, nar(z[:st])], axis=1)) for x, y_, z in zip(t, a_st, lm_v)]
    wm4 = [fold(x[:, :st]) for x in t_av]
    uk4 = [fold(x[:, st:]) for x in t_av]

    us = [[None] * nb for _ in gs]
    ys = [[None] * nb for _ in gs]
    for ck in range(ncs):
        cgs = range(ck * gpb, (ck + 1) * gpb)
        for b in range(nb):
            rows = slice(b * C, (b + 1) * C)
            s_old = {gi: sbd_ref[b, gi - ck * gpb] for gi in cgs}
            uy = {gi: mm_s(jnp.concatenate([wm4[gi][rows], r4[gi][rows]], axis=0), s_old[gi], _NT) for gi in cgs}
            for gi in cgs:
                us[gi][b] = uy[gi][:C] + uk4[gi][rows]
                ys[gi][b] = uy[gi][C:]
            ds = {gi: mm_s(jnp.concatenate([us[gi][b], vg[gi][rows]], axis=0),
                           jnp.concatenate([b4[gi][rows], k4[gi][rows]], axis=0), _TN) for gi in cgs}
            for gi in cgs:
                s_new = (s_old[gi] + jnp.where(same_head, ds[gi], 0.0)) * gam[gi][(b + 1) * C - 1:(b + 1) * C, :]
                sbd_ref[b, gi - ck * gpb] = s_new
                if ck == ncs - 1:
                    @pl.when(c == nsteps - 1)
                    def _fin():
                        hs = slice((gi - ck * gpb) * HEADS_PER_GROUP, (gi - ck * gpb + 1) * HEADS_PER_GROUP)
                        sout_ref[b, hs] = from_block_diag(s_new).reshape(HEADS_PER_GROUP, HEAD_DIM, HEAD_DIM)

    u = [x[0] if nb == 1 else jnp.concatenate(x, axis=0) for x in us]
    ysum = [x[0] if nb == 1 else jnp.concatenate(x, axis=0) for x in ys]
    yb = [fold(mm_s(x, stack(y_))) for x, y_ in zip(m_b, u)]
    y = [x + y_ + z for x, y_, z in zip(ysum, mkv4, yb)]

    ym = [_dot_xr(x, block_ones, passes=ps["gn"]) * (1.0 / HEAD_DIM) for x in y]
    yc = [x - y_ for x, y_ in zip(y, ym)]
    yv = [_dot_xr(x * x, block_ones, passes=ps["gn"]) * (1.0 / HEAD_DIM) for x in yc]
    bonus = [_dot_xr(rg[gi] * k2[gi] * rk_ref[:, sls[gi]], block_ones, passes=ps["gn"]) * vg[gi] for gi in gs]
    for gi in gs:
        sl, rw = sls[gi], rws[gi]
        yn = yc[gi] * lax.rsqrt(yv[gi] + GN_EPS) * gng_ref[:, sl] + gnb_ref[:, sl]
        o_ref[rw, sl] = (yn + bonus[gi]) * gate_all[rw, sl]


def _rwkv(proj, nseq, nchunk, C, prev_shift, state0, prm, gpb=None, ncs=None):
    nb = ROWS // C
    if gpb is None:
        gpb = N_GROUPS if nb == 1 else N_GROUPS // 2
    if ncs is None:
        ncs = 2 if nb == 1 and nchunk % 2 == 0 else 1
    assert nb * C == ROWS and nseq % nb == 0 and (nb == 1 or nchunk == 1) and nchunk % ncs == 0
    rows = nseq * nchunk * C
    nsteps = nchunk // ncs
    srows = ncs * ROWS
    w = gpb * GROUP_W
    sec = D_MODEL // w
    prows = 8 if nb == 1 else C
    pblk = 8 if nb == 1 else srows
    prev = jnp.zeros((nseq, prows, SHIFT_W), F32).at[:, 0].set(prev_shift).reshape(nseq * prows, SHIFT_W)
    prev_rkv = prev[:, :RKV_W]
    prev_lora = jnp.pad(prev[:, RKV_W:], ((0, 0), (0, LORA_PAD - LORA_W)))

    def tok(off):
        return pl.BlockSpec((srows, w), lambda s, g, c: (s * nsteps + c, off + g))

    def prv(off):
        return pl.BlockSpec((pblk, w), lambda s, g, c: (s, off + g))

    def vec(off=0):
        return pl.BlockSpec((1, w), lambda s, g, c: (0, off + g))

    up = pl.BlockSpec((LORA_PAD, w), lambda s, g, c: (0, g))
    in_specs = [
        tok(0), tok(sec), tok(2 * sec),
        pl.BlockSpec((srows, LORA_PAD), lambda s, g, c: (s * nsteps + c, COL_LORA // LORA_PAD)),
        prv(0), prv(sec), prv(2 * sec),
        pl.BlockSpec((pblk, LORA_PAD), lambda s, g, c: (s, 0)),
        pl.BlockSpec((nb, gpb * HEADS_PER_GROUP, HEAD_DIM, HEAD_DIM), lambda s, g, c: (s, g, 0, 0)),
        vec(0), vec(sec), vec(2 * sec),
        pl.BlockSpec((1, LORA_PAD), lambda s, g, c: (0, 0)),
        vec(), vec(), vec(), vec(), vec(), vec(), vec(),
        up, up, up,
    ]
    out_specs = [pl.BlockSpec((srows, w), lambda s, g, c: (s * nsteps + c, g)),
                 pl.BlockSpec((nb, gpb * HEADS_PER_GROUP, HEAD_DIM, HEAD_DIM), lambda s, g, c: (s, g, 0, 0))]
    out_shape = [jax.ShapeDtypeStruct((rows, D_MODEL), F32),
                 jax.ShapeDtypeStruct((nseq, D_MODEL // HEAD_DIM, HEAD_DIM, HEAD_DIM), F32)]
    operands = [proj, proj, proj, proj, prev_rkv, prev_rkv, prev_rkv, prev_lora, state0,
                prm["mu_rkv"], prm["mu_rkv"], prm["mu_rkv"], prm["mu_lora"],
                prm["w0"], prm["a0"], prm["k_k"], prm["k_a"], prm["r_k"], prm["gn_g"], prm["gn_b"],
                prm["w_up"], prm["a_up"], prm["g_up"]]
    return pl.pallas_call(
        functools.partial(_rwkv_body, C, nb, ncs, nsteps, gpb, RWKV_PASSES),
        grid=(nseq // nb, N_GROUPS // gpb, nsteps),
        in_specs=in_specs,
        out_specs=out_specs,
        out_shape=out_shape,
        scratch_shapes=[pltpu.VMEM((nb, gpb, GROUP_W, GROUP_W), F32),
                        pltpu.VMEM((srows + 8, 3 * w), F32),
                        pltpu.VMEM((srows + 8, LORA_PAD), F32)],
        compiler_params=_params(("arbitrary", "arbitrary", "arbitrary")),
        name="rwkv",
    )(*operands)


def _attn_rows(q, kcat, vcat, sink_ref, first_key, store, npr):
    m = q.shape[0]
    kt_all = kcat.T
    qi = _iota((npr * m, 2 * WINDOW), 0) & (m - 1)
    ki = _iota((npr * m, 2 * WINDOW), 1)
    allowed = (ki >= qi + 1) & (ki <= qi + WINDOW) & (ki >= first_key)
    lane = _iota((2 * WINDOW, 128), 1)
    ones_k = jnp.ones((2 * WINDOW, 128), BF16)
    zk = jnp.zeros((HEAD_DIM, 2 * WINDOW), F32)
    nkv = KV_W // HEAD_DIM
    k2s, v2s = [], []
    for kv in range(nkv):
        kt = kt_all[kv * HEAD_DIM:(kv + 1) * HEAD_DIM, :]
        k2s.append(jnp.concatenate([jnp.concatenate([kt, zk], axis=1),
                                    jnp.concatenate([zk, kt], axis=1)], axis=0).astype(BF16))
        slab = vcat[:, (kv // 2) * 128:(kv // 2 + 1) * 128]
        rolled = pltpu.roll(slab, HEAD_DIM, axis=1)
        lo_src, hi_src = (slab, rolled) if kv % 2 == 0 else (rolled, slab)
        v2s.append(jnp.concatenate([jnp.where(lane < HEAD_DIM, lo_src, 0.0),
                                    jnp.where(lane >= HEAD_DIM, hi_src, 0.0)], axis=0).astype(BF16))
    items = [(kv, p0) for kv in range(nkv) for p0 in range(0, 4, npr)]
    cols = [[kv * 512 + (p0 + i) * 128 for i in range(npr)] for kv, p0 in items]
    scores = [_bdot((jnp.concatenate([q[:, col:col + 128] for col in cs], axis=0) * ATT_SCALE).astype(BF16),
                    k2s[kv]) for (kv, _), cs in zip(items, cols)]
    probs = []
    for (kv, p0), s in zip(items, scores):
        ps = []
        for hb in range(2):
            sink = jnp.concatenate(
                [jnp.broadcast_to(sink_ref[:, (kv * 8 + (p0 + i) * 2 + hb) * 128:
                                           (kv * 8 + (p0 + i) * 2 + hb) * 128 + 1], (m, 1))
                 for i in range(npr)], axis=0)
            sh = jnp.where(allowed, s[:, hb * 256:(hb + 1) * 256], -jnp.inf)
            mx = jnp.maximum(jnp.max(sh, axis=1, keepdims=True), sink)
            e = jnp.exp(sh - mx)
            inv = 1.0 / (_dot_xr(e, ones_k, passes=2) + jnp.exp(sink - mx))
            ps += [e[:, :128] * inv, e[:, 128:] * inv]
        probs.append(jnp.concatenate(ps, axis=1).astype(BF16))
    outs = [_bdot(p, v2s[kv]) for (kv, _), p in zip(items, probs)]
    for cs, o in zip(cols, outs):
        for i, col in enumerate(cs):
            store(col, o[i * m:(i + 1) * m])


def _attn_prompt_body(q_ref, kvc_ref, kvp_ref, sink_ref, o_ref):
    n = pl.program_id(0)
    kvc = kvc_ref[...]
    kvp = kvp_ref[...]
    kcat = jnp.concatenate([kvp[:, :KV_W], kvc[:, :KV_W]], axis=0)
    vcat = jnp.concatenate([kvp[:, KV_W:], kvc[:, KV_W:]], axis=0)

    def store(col, val):
        o_ref[:, col:col + 128] = val

    _attn_rows(q_ref[...], kcat, vcat, sink_ref, jnp.where(n > 0, 0, WINDOW), store, npr=1)


def _attn_prompt(proj, sinks_e):
    t = proj.shape[0]
    nb = t // WINDOW
    qb, kvb = COL_Q // D_MODEL, COL_KV // (2 * KV_W)
    return pl.pallas_call(
        _attn_prompt_body,
        grid=(nb,),
        in_specs=[pl.BlockSpec((WINDOW, D_MODEL), lambda n: (n, qb)),
                  pl.BlockSpec((WINDOW, 2 * KV_W), lambda n: (n, kvb)),
                  pl.BlockSpec((WINDOW, 2 * KV_W), lambda n: (jnp.maximum(n - 1, 0), kvb)),
                  pl.BlockSpec((1, 32 * 128), lambda n: (0, 0))],
        out_specs=pl.BlockSpec((WINDOW, D_MODEL), lambda n: (n, 0)),
        out_shape=jax.ShapeDtypeStruct((t, D_MODEL), F32),
        compiler_params=_params(("parallel",)),
        name="attn_prompt",
    )(proj, proj, proj, sinks_e)


def _attn_sample_body(L, bt, q_ref, kvn_ref, ck_ref, cv_ref, sink_ref, o_ref):
    pad = jnp.zeros((WINDOW - L, KV_W), F32)

    def one(b, carry):
        rows = pl.ds(pl.multiple_of(b * L, L), L)
        kvn = kvn_ref[rows, :]
        kcat = jnp.concatenate([ck_ref[b], kvn[:, :KV_W], pad], axis=0)
        vcat = jnp.concatenate([cv_ref[b], kvn[:, KV_W:], pad], axis=0)

        def store(col, val):
            o_ref[rows, col:col + 128] = val

        _attn_rows(q_ref[rows, :], kcat, vcat, sink_ref, 0, store, npr=4)
        return carry

    lax.fori_loop(0, bt, one, 0, unroll=2)


def _attn_sample(proj, ck, cv, sinks_e, L, bt=ATTN_SAMPLE_SEQS):
    rows = proj.shape[0]
    nb = rows // L
    bt = min(bt, nb)
    qb, kvb = COL_Q // D_MODEL, COL_KV // (2 * KV_W)
    return pl.pallas_call(
        functools.partial(_attn_sample_body, L, bt),
        grid=(nb // bt,),
        in_specs=[pl.BlockSpec((bt * L, D_MODEL), lambda i: (i, qb)),
                  pl.BlockSpec((bt * L, 2 * KV_W), lambda i: (i, kvb)),
                  pl.BlockSpec((bt, WINDOW, KV_W), lambda i: (i, 0, 0)),
                  pl.BlockSpec((bt, WINDOW, KV_W), lambda i: (i, 0, 0)),
                  pl.BlockSpec((1, 32 * 128), lambda i: (0, 0))],
        out_specs=pl.BlockSpec((bt * L, D_MODEL), lambda i: (i, 0)),
        out_shape=jax.ShapeDtypeStruct((rows, D_MODEL), F32),
        compiler_params=_params(("parallel",)),
        name="attn_sample",
    )(proj, proj, ck, cv, sinks_e)


def _layer_norm(x, g, b):
    mu = jnp.mean(x, axis=-1, keepdims=True)
    xc = x - mu
    var = jnp.mean(xc * xc, axis=-1, keepdims=True)
    return xc * lax.rsqrt(var + LN_EPS) * g + b


def _merge_body(ga_ref, gb_ref, oa_ref, ob_ref, x_ref, wout_ref, g_ref, b_ref, h_ref, ht_ref):
    tm = x_ref.shape[0]
    sub = min(128, tm)
    ys = []

    def finish(i):
        rows = slice(i * sub, (i + 1) * sub)
        h = _layer_norm(DN_ALPHA * x_ref[rows, :] + ys[i], g_ref[...], b_ref[...])
        h_ref[rows, :] = h
        ht_ref[:, rows] = h.T.astype(BF16)

    for i in range(tm // sub):
        rows = slice(i * sub, (i + 1) * sub)
        mixed = _sigmoid(ga_ref[rows, :]) * oa_ref[rows, :] + _sigmoid(gb_ref[rows, :]) * ob_ref[rows, :]
        ys.append(_bdot(mixed.astype(BF16), wout_ref[...]))
        if i > 0:
            finish(i - 1)
    finish(tm // sub - 1)


def _merge(proj, o_a, o_b, x2d, wout_bf16, ln_g, ln_b, tm=MERGE_TM):
    m = x2d.shape[0]
    tm = min(tm, m)
    gab = COL_GATE // D_MODEL
    row = lambda i: (i, 0)
    return pl.pallas_call(
        _merge_body,
        grid=(m // tm,),
        in_specs=[pl.BlockSpec((tm, D_MODEL), lambda i: (i, gab)),
                  pl.BlockSpec((tm, D_MODEL), lambda i: (i, gab + 1)),
                  pl.BlockSpec((tm, D_MODEL), row),
                  pl.BlockSpec((tm, D_MODEL), row),
                  pl.BlockSpec((tm, D_MODEL), row),
                  pl.BlockSpec((D_MODEL, D_MODEL), lambda i: (0, 0)),
                  pl.BlockSpec((1, D_MODEL), lambda i: (0, 0)),
                  pl.BlockSpec((1, D_MODEL), lambda i: (0, 0))],
        out_specs=[pl.BlockSpec((tm, D_MODEL), row),
                   pl.BlockSpec((D_MODEL, tm), lambda i: (0, i))],
        out_shape=[jax.ShapeDtypeStruct((m, D_MODEL), F32),
                   jax.ShapeDtypeStruct((D_MODEL, m), BF16)],
        compiler_params=_params(("parallel",)),
        name="merge",
    )(proj, proj, o_a, o_b, x2d, wout_bf16, ln_g, ln_b)


def _top_rows(x, count):
    rows = _iota(x.shape, 0)
    out = []
    for _ in range(count):
        mx = jnp.max(x, axis=0, keepdims=True)
        out.append(mx)
        first = jnp.min(jnp.where(x == mx, rows, x.shape[0]), axis=0, keepdims=True)
        x = jnp.where(rows == first, -jnp.inf, x)
    return out


def _bitonic_network(n):
    comps = []
    k = 2
    while k <= n:
        j = k // 2
        while j >= 1:
            comps += [(i, i ^ j, (i & k) == 0) for i in range(n) if (i ^ j) > i]
            j //= 2
        k *= 2
    return comps


def _top16_of_128(xs):
    nt = N_KEYS // 8
    tiles = [[x[8 * j:8 * (j + 1)] for j in range(nt)] for x in xs]
    for i, l, desc in _bitonic_network(nt):
        for t in tiles:
            hi, lo = jnp.maximum(t[i], t[l]), jnp.minimum(t[i], t[l])
            t[i], t[l] = (hi, lo) if desc else (lo, hi)
    sub = _iota(tiles[0][0].shape, 0)
    outs = [[] for _ in xs]
    for r in range(PEER_TOPK):
        for t, o in zip(tiles, outs):
            mx = jnp.max(t[0], axis=0, keepdims=True)
            o.append(mx)
            first = jnp.min(jnp.where(t[0] == mx, sub, 8), axis=0, keepdims=True)
            pop = sub == first
            for j in range(PEER_TOPK - r - 1):
                t[j] = jnp.where(pop, t[j + 1], t[j])
    return outs


def _route_body(tm, h_ref, wq_ref, keys_ref, s0_ref, s1_ref, aux_ref, q_s):
    q = _bdot(h_ref[...].astype(BF16), wq_ref[...])
    for hc in range(2 * PEER_HEADS):
        q_s[hc] = q[:, hc * N_KEYS:(hc + 1) * N_KEYS]
    pairs = [(i, j) for i in range(PEER_TOPK) for j in range(PEER_TOPK) if (i + 1) * (j + 1) <= PEER_TOPK]
    npad = -len(pairs) % 8
    neg = jnp.full((1, tm), -jnp.inf, F32)
    zero = jnp.zeros((1, tm), F32)

    def head(hh, carry):
        s0 = _bdot(keys_ref[hh, 0], q_s[2 * hh].astype(BF16), _NT)
        s1 = _bdot(keys_ref[hh, 1], q_s[2 * hh + 1].astype(BF16), _NT)
        s0_ref[hh] = s0
        for lb in range(tm // 128):
            s1_ref[hh, lb] = s1[:, lb * 128:(lb + 1) * 128]
        top0, top1 = _top16_of_128([s0, s1])
        cand = jnp.concatenate([top0[i] + top1[j] for i, j in pairs] + [neg] * npad, axis=0)
        best = _top_rows(cand, PEER_TOPK)
        mx = best[0]
        z = zero
        for bsum in best:
            z = z + jnp.exp(bsum - mx)
        aux_ref[hh] = jnp.concatenate([best[-1], top0[0], top1[0], 1.0 / z, zero, zero, zero, zero], axis=0)
        return carry

    lax.fori_loop(0, PEER_HEADS, head, 0, unroll=2)


def _route(h, wq_bf16, keys_bf16, tm=ROUTE_TM):
    m = h.shape[0]
    tm = min(tm, m)
    tok3 = lambda i: (0, 0, i)
    return pl.pallas_call(
        functools.partial(_route_body, tm),
        grid=(m // tm,),
        in_specs=[pl.BlockSpec((tm, D_MODEL), lambda i: (i, 0)),
                  pl.BlockSpec((D_MODEL, D_MODEL), lambda i: (0, 0)),
                  pl.BlockSpec((PEER_HEADS, 2, N_KEYS, N_KEYS), lambda i: (0, 0, 0, 0))],
        out_specs=[pl.BlockSpec((PEER_HEADS, N_KEYS, tm), tok3),
                   pl.BlockSpec((PEER_HEADS, tm // 128, N_KEYS, 128), lambda i: (0, i, 0, 0)),
                   pl.BlockSpec((PEER_HEADS, 8, tm), tok3)],
        out_shape=[jax.ShapeDtypeStruct((PEER_HEADS, N_KEYS, m), F32),
                   jax.ShapeDtypeStruct((PEER_HEADS, m // 128, N_KEYS, 128), F32),
                   jax.ShapeDtypeStruct((PEER_HEADS, 8, m), F32)],
        scratch_shapes=[pltpu.VMEM((2 * PEER_HEADS, tm, N_KEYS), F32)],
        compiler_params=_params(("parallel",)),
        name="peer_route",
    )(h, wq_bf16, keys_bf16)


def _peer_body(te, ne, ht_ref, u_ref, vt_ref, s0_ref, s1_ref, aux_ref, h_ref, g_ref, b_ref, y_ref,
               acc_ref, e1_ref, cf_ref, st0_ref, st1_ref, w0_ref, w1_ref):
    j = pl.program_id(1)
    tm = acc_ref.shape[1]

    @pl.when(j == 0)
    def _():
        acc_ref[...] = jnp.zeros_like(acc_ref)
        for ref in (st0_ref, st1_ref, w0_ref, w1_ref):
            ref[...] = jnp.zeros_like(ref)
        for hh in range(PEER_HEADS):
            aux = aux_ref[hh]
            for lb in range(tm // 128):
                e1_ref[hh, lb] = jnp.exp(s1_ref[hh, lb] - aux[2:3, lb * 128:(lb + 1) * 128])
            cf_ref[hh] = jnp.exp(s0_ref[hh] - aux[1:2, :]) * aux[3:4, :]

    def stages(st_new, st_old, w_new, w_old):
        jb = j - 1
        live = (jb >= 0) & (jb < ne)
        jb_c = jnp.clip(jb, 0, ne - 1)
        taus = [jnp.where(live, aux_ref[hh, 0:1, :], jnp.inf) for hh in range(PEER_HEADS)]
        ncc = te // N_KEYS
        s0rows = [[s0_ref[hh, pl.ds(jb_c * ncc + cc, 1), :] for hh in range(PEER_HEADS)] for cc in range(ncc)]
        cfrows = [[cf_ref[hh, pl.ds(jb_c * ncc + cc, 1), :] for hh in range(PEER_HEADS)] for cc in range(ncc)]
        tw = min(256, tm)
        mxu_pieces, gate_tiles = [], []
        for t0 in range(0, tm, tw):
            ts = slice(t0, t0 + tw)
            lbs = range(t0 // 128, (t0 + tw) // 128)

            def mix(mr, ts=ts, lbs=lbs):
                acc_ref[mr, ts] += _bdot(vt_ref[mr, :], jnp.concatenate([w_old[lb] for lb in lbs], axis=1))

            def score(er, t0=t0, ts=ts, lbs=lbs):
                st = _bdot(u_ref[er, :], ht_ref[:, ts])
                for lb in lbs:
                    st_new[lb, er, :] = st[:, lb * 128 - t0:(lb + 1) * 128 - t0]

            for q in range(4):
                mxu_pieces.append(functools.partial(mix, slice(q * (D_MODEL // 4), (q + 1) * (D_MODEL // 4))))
            for q in range(2):
                mxu_pieces.append(functools.partial(score, slice(q * (te // 2), (q + 1) * (te // 2))))

            def gate(lb, cc):
                ls = slice(lb * 128, (lb + 1) * 128)
                rows = slice(cc * N_KEYS, (cc + 1) * N_KEYS)
                gsum = None
                for hh in range(PEER_HEADS):
                    sel = (s1_ref[hh, lb] + s0rows[cc][hh][:, ls]) >= taus[hh][:, ls]
                    term = jnp.where(sel, e1_ref[hh, lb] * cfrows[cc][hh][:, ls], 0.0)
                    gsum = term if gsum is None else gsum + term
                so = st_old[lb, rows, :]
                act = 0.5 * so * (1.0 + lax.erf(so * np.float32(np.sqrt(0.5))))
                w_new[lb, rows, :] = (gsum * act).astype(BF16)

            for lb in lbs:
                for cc in range(ncc):
                    gate_tiles.append(functools.partial(gate, lb, cc))
        for i in range(max(len(mxu_pieces), len(gate_tiles))):
            if i < len(gate_tiles):
                gate_tiles[i]()
            if i < len(mxu_pieces):
                mxu_pieces[i]()

    @pl.when(j % 2 == 0)
    def _():
        stages(st0_ref, st1_ref, w1_ref, w0_ref)

    @pl.when(j % 2 == 1)
    def _():
        stages(st1_ref, st0_ref, w0_ref, w1_ref)

    @pl.when(j == ne + 1)
    def _():
        out = acc_ref[...].T
        y_ref[...] = _layer_norm(DN_ALPHA * h_ref[...] + out, g_ref[...], b_ref[...])


def _peer(ht, u_bf16, vt_bf16, s0t, s1t, aux, h, ln_g, ln_b, tm=PEER_TM, te=PEER_TE):
    m = h.shape[0]
    tm = min(tm, m)
    ne = N_EXPERTS // te
    tok3 = lambda i, j: (0, 0, i)
    return pl.pallas_call(
        functools.partial(_peer_body, te, ne),
        grid=(m // tm, ne + 2),
        in_specs=[pl.BlockSpec((D_MODEL, tm), lambda i, j: (0, i)),
                  pl.BlockSpec((te, D_MODEL), lambda i, j: (jnp.minimum(j, ne - 1), 0)),
                  pl.BlockSpec((D_MODEL, te), lambda i, j: (0, jnp.clip(j - 2, 0, ne - 1))),
                  pl.BlockSpec((PEER_HEADS, N_KEYS, tm), tok3),
                  pl.BlockSpec((PEER_HEADS, tm // 128, N_KEYS, 128), lambda i, j: (0, i, 0, 0)),
                  pl.BlockSpec((PEER_HEADS, 8, tm), tok3),
                  pl.BlockSpec((tm, D_MODEL), lambda i, j: (i, 0)),
                  pl.BlockSpec((1, D_MODEL), lambda i, j: (0, 0)),
                  pl.BlockSpec((1, D_MODEL), lambda i, j: (0, 0))],
        out_specs=pl.BlockSpec((tm, D_MODEL), lambda i, j: (i, 0)),
        out_shape=jax.ShapeDtypeStruct((m, D_MODEL), F32),
        scratch_shapes=[pltpu.VMEM((D_MODEL, tm), F32),
                        pltpu.VMEM((PEER_HEADS, tm // 128, N_KEYS, 128), F32),
                        pltpu.VMEM((PEER_HEADS, N_KEYS, tm), F32),
                        pltpu.VMEM((tm // 128, te, 128), F32),
                        pltpu.VMEM((tm // 128, te, 128), F32),
                        pltpu.VMEM((tm // 128, te, 128), BF16),
                        pltpu.VMEM((tm // 128, te, 128), BF16)],
        compiler_params=_params(("parallel", "arbitrary")),
        name="peer_dense",
    )(ht, u_bf16, vt_bf16, s0t, s1t, aux, h, ln_g, ln_b)


def _pad_rows(w, lo, total):
    return jnp.zeros((total, w.shape[1]), w.dtype).at[lo:lo + w.shape[0]].set(w)


def _layer(x2d, nseq, seq_len, chunk, prev_shift, state0, cache, wts):
    proj_a = _inproj(x2d, wts["w_in_t"], PROJ_A_W)
    proj_b = _inproj(x2d, wts["w_in_b"], PROJ_B_W)
    o_a, s_last = _rwkv(proj_a, nseq, seq_len // chunk, chunk, prev_shift, state0, wts)
    if cache is None:
        o_b = _attn_prompt(proj_b, wts["sinks"])
    else:
        o_b = _attn_sample(proj_b, cache[0], cache[1], wts["sinks"], seq_len)
    h, ht = _merge(proj_b, o_a, o_b, x2d, wts["w_out"], wts["ln1_g"], wts["ln1_b"])
    s0t, s1t, aux = _route(h, wts["peer_wq"], wts["peer_keys"])
    y = _peer(ht, wts["peer_u"], wts["peer_vt"], s0t, s1t, aux, h, wts["ln2_g"], wts["ln2_b"])
    return y, proj_a, proj_b, s_last


def kernel(x_prompt, x_sample, state_shift, state_wkv, cache_k, cache_v, w_in, rw_mu, rw_w0, rw_w_up, rw_a0,
           rw_a_up, rw_g_up, rw_k_k, rw_k_a, rw_r_k, rw_gn_g, rw_gn_b, att_sinks, w_out, ln1_g, ln1_b,
           peer_wq, peer_keys, peer_u, peer_v, ln2_g, ln2_b):
    depth = w_in.shape[0]
    assert depth == 1
    l = 0
    bp, tp, _ = x_prompt.shape
    bs, ts, _ = x_sample.shape
    assert bp == 1
    win = cache_k.shape[2]
    assert win == WINDOW and tp % WINDOW == 0 and ts <= 8

    w_t = jnp.transpose(w_in[l]).astype(BF16)
    q0 = SHIFT_W
    k0 = q0 + D_MODEL
    g0 = k0 + 2 * KV_W
    w_b = jnp.concatenate([w_t[q0:q0 + D_MODEL], w_t[g0:g0 + 2 * D_MODEL], w_t[k0:k0 + 2 * KV_W]], axis=0)
    mu = rw_mu[l]
    row = lambda v: v.reshape(1, -1)
    wts = dict(
        w_in_t=w_t, w_in_b=w_b,
        mu_rkv=row(mu[:RKV_W]),
        mu_lora=row(jnp.pad(mu[RKV_W:], (0, LORA_PAD - LORA_W))),
        w0=row(rw_w0[l]), a0=row(rw_a0[l]), k_k=row(rw_k_k[l]), k_a=row(rw_k_a[l]), r_k=row(rw_r_k[l]),
        gn_g=row(rw_gn_g[l]), gn_b=row(rw_gn_b[l]),
        w_up=_pad_rows(rw_w_up[l], 0, LORA_PAD).astype(BF16),
        a_up=_pad_rows(rw_a_up[l], 96, LORA_PAD).astype(BF16),
        g_up=_pad_rows(rw_g_up[l], 192, LORA_PAD).astype(BF16),
        sinks=jnp.repeat(att_sinks[l], 128).reshape(1, 32 * 128),
        w_out=w_out[l].astype(BF16),
        ln1_g=row(ln1_g[l]), ln1_b=row(ln1_b[l]), ln2_g=row(ln2_g[l]), ln2_b=row(ln2_b[l]),
        peer_wq=peer_wq[l].astype(BF16),
        peer_keys=peer_keys[l].astype(BF16),
        peer_u=peer_u[l].astype(BF16),
        peer_vt=peer_v[l].T.astype(BF16),
    )

    chunk_p = 64
    y_p, pa_p, pb_p, s_p = _layer(x_prompt[0], 1, tp, chunk_p,
                                  jnp.zeros((1, SHIFT_W), F32),
                                  jnp.zeros((1, D_MODEL // HEAD_DIM, HEAD_DIM, HEAD_DIM), F32), None, wts)
    keep = min(WINDOW, tp)
    kv_p = pb_p[tp - keep:, COL_KV:]
    y_prompt = y_p[None]
    new_shift_prompt = pa_p[tp - 1, :SHIFT_W][None, None]
    new_wkv_prompt = s_p[None]
    new_k_prompt = kv_p[:, :KV_W].reshape(1, 1, keep, KV_W // HEAD_DIM, HEAD_DIM)
    new_v_prompt = kv_p[:, KV_W:].reshape(1, 1, keep, KV_W // HEAD_DIM, HEAD_DIM)

    ck = cache_k[l].reshape(bs, win, KV_W)
    cv = cache_v[l].reshape(bs, win, KV_W)
    y_s, pa_s, pb_s, s_s = _layer(x_sample.reshape(bs * ts, D_MODEL), bs, ts, ts,
                                  state_shift[l], state_wkv[l], (ck, cv), wts)
    y_sample = y_s.reshape(bs, ts, D_MODEL)
    new_shift_sample = pa_s.reshape(bs, ts, PROJ_A_W)[:, ts - 1, :SHIFT_W][None]
    new_wkv_sample = s_s[None]
    kv_s = pb_s.reshape(bs, ts, PROJ_B_W)[:, :, COL_KV:]
    new_k_sample = jnp.concatenate([ck, kv_s[:, :, :KV_W]], axis=1)[:, ts:].reshape(
        1, bs, win, KV_W // HEAD_DIM, HEAD_DIM)
    new_v_sample = jnp.concatenate([cv, kv_s[:, :, KV_W:]], axis=1)[:, ts:].reshape(
        1, bs, win, KV_W // HEAD_DIM, HEAD_DIM)

    return (y_prompt, y_sample, new_shift_prompt, new_wkv_prompt, new_k_prompt, new_v_prompt,
            new_shift_sample, new_wkv_sample, new_k_sample, new_v_sample)
```
